```python
import math
import jax, jax.numpy as jnp
from jax import lax
import numpy as np

D_MODEL = 2048
BATCH = 4
SEQ = 4096
DEPTH = 4

GRID_W = 64
CTX_LEN = 256
EPS = 1e-6
CONV_K = 3
CHUNK = 128
N_BRANCH = 4
BRANCH_WIDTH = D_MODEL // 2
SSD_INNER = BRANCH_WIDTH
SSD_HEAD_DIM = 64
SSD_HEADS = SSD_INNER // SSD_HEAD_DIM
SSD_GROUPS = 4
SSD_STATE = 128
SSD_CONV_CH = SSD_INNER + 2 * SSD_GROUPS * SSD_STATE
POOL_WIDTH = BRANCH_WIDTH
POOL_WINDOWS = (2, 4, 8, 16)
N_POOL = len(POOL_WINDOWS)
POOL_GROUP = POOL_WIDTH // N_POOL
SC_WIDTH = BRANCH_WIDTH
RET_V = BRANCH_WIDTH
RET_HEADS = 8
RET_V_HEAD = RET_V // RET_HEADS
RET_K_HEAD = RET_V_HEAD // 2
RET_QK = RET_HEADS * RET_K_HEAD
ROPE_BASE = 10000.0
D_FF = 256 * ((8 * D_MODEL // 3 + 255) // 256)
SPLIT_SIZES = (SSD_INNER, SSD_CONV_CH, SSD_HEADS, POOL_WIDTH, SC_WIDTH, SC_WIDTH, SC_WIDTH,
               RET_QK, RET_QK, RET_V, RET_V)
IN_COLS = sum(SPLIT_SIZES)
SPLIT_OFFSETS = tuple(int(o) for o in np.cumsum(SPLIT_SIZES)[:-1])

kernel_name = 'hybrid_ssd_pool_shortconv_retention_dit'


def rmsnorm(x, w):
    xf = x.astype(jnp.float32)
    y = xf * lax.rsqrt(jnp.mean(jnp.square(xf), axis=-1, keepdims=True) + EPS)
    return (y * w).astype(x.dtype)


def head_layernorm(y):
    yf = y.astype(jnp.float32)
    mu = jnp.mean(yf, axis=-1, keepdims=True)
    var = jnp.mean(jnp.square(yf - mu), axis=-1, keepdims=True)
    return ((yf - mu) * lax.rsqrt(var + EPS)).astype(y.dtype)


def modulate(h, shift, scale):
    return h * (1.0 + scale) + shift


def dwconv_centered(u, w):
    k_taps = w.shape[0]
    pad = k_taps // 2
    n = u.shape[1]
    up = jnp.pad(u, ((0, 0), (pad, pad), (0, 0)))
    y = up[:, 0:n] * w[0]
    for j in range(1, k_taps):
        y = y + up[:, j:j + n] * w[j]
    return y


def chunked_scan(q, k, v, log_a, h0):
    bsz, n, nh, dn = q.shape
    dp = v.shape[-1]
    nc = n // CHUNK
    f32 = jnp.float32
    qc = q.astype(f32).reshape(bsz, nc, CHUNK, nh, dn)
    kc = k.astype(f32).reshape(bsz, nc, CHUNK, nh, dn)
    vc = v.astype(f32).reshape(bsz, nc, CHUNK, nh, dp)
    cum = jnp.cumsum(log_a.astype(f32).reshape(bsz, nc, CHUNK, nh), axis=2)
    causal = jnp.tril(jnp.ones((CHUNK, CHUNK), dtype=bool))[None, None, :, :, None]
    seg = cum[:, :, :, None, :] - cum[:, :, None, :, :]
    decay = jnp.exp(jnp.where(causal, seg, -jnp.inf))
    scores = jnp.einsum('bcihn,bcjhn->bcijh', qc, kc) * decay
    y_intra = jnp.einsum('bcijh,bcjhp->bcihp', scores, vc)
    last = cum[:, :, -1:, :]
    w_in = jnp.exp(last - cum)
    states = jnp.einsum('bcjhn,bcjhp->bchnp', kc * w_in[..., None], vc)
    chunk_decay = jnp.exp(last[:, :, 0, :])

    def step(h, inp):
        s, d = inp
        return h * d[:, :, None, None] + s, h

    h_final, h_enter = lax.scan(step, h0.astype(f32),
                                (jnp.moveaxis(states, 1, 0), jnp.moveaxis(chunk_decay, 1, 0)))
    h_enter = jnp.moveaxis(h_enter, 0, 1)
    y_inter = jnp.einsum('bcihn,bchnp->bcihp', qc, h_enter) * jnp.exp(cum)[..., None]
    y = (y_intra + y_inter).reshape(bsz, n, nh, dp)
    return y.astype(q.dtype), h_final


def bidir_scan(q, k, v_f, v_b, la_f, la_b, h0_f, h0_b):
    y_f, h_f = chunked_scan(q, k, v_f, la_f, h0_f)
    fl = lambda a: jnp.flip(a, axis=1)
    y_b, h_b = chunked_scan(fl(q), fl(k), fl(v_b), fl(la_b), h0_b)
    return y_f + fl(y_b), h_f, h_b


def axial_rope(t, rows):
    dh = t.shape[-1]
    half = dh // 2
    quarter = half // 2
    f32 = jnp.float32
    row = jnp.repeat(jnp.arange(rows), GRID_W).astype(f32)
    col = jnp.tile(jnp.arange(GRID_W), rows).astype(f32)
    inv = ROPE_BASE ** (-jnp.arange(quarter, dtype=f32) / quarter)

    def rot(xp, pos):
        ang = pos[:, None] * inv[None, :]
        cos = jnp.cos(ang)[None, :, None, :]
        sin = jnp.sin(ang)[None, :, None, :]
        x1 = xp[..., :quarter].astype(f32)
        x2 = xp[..., quarter:].astype(f32)
        return jnp.concatenate([x1 * cos - x2 * sin, x1 * sin + x2 * cos], axis=-1)

    return jnp.concatenate([rot(t[..., :half], row), rot(t[..., half:], col)], axis=-1).astype(t.dtype)


def multiscale_pool(u, pool_w, pool_scale):
    bsz, n, _ = u.shape
    f32 = jnp.float32
    g = u.reshape(bsz, n, N_POOL, POOL_GROUP).astype(f32)
    cs = jnp.pad(jnp.cumsum(g, axis=1), ((0, 0), (1, 0), (0, 0), (0, 0)))
    t = jnp.arange(n)
    means = []
    for gi, win in enumerate(POOL_WINDOWS):
        lo = jnp.clip(t - win // 2, 0, n)
        hi = jnp.clip(t - win // 2 + win, 0, n)
        csg = cs[:, :, gi]
        s = csg[:, hi] - csg[:, lo]
        means.append(s / (hi - lo).astype(f32)[None, :, None])
    pooled = (jnp.stack(means, axis=2) - g).astype(u.dtype)
    y = jnp.einsum('blgc,gcd->blgd', pooled, pool_w)
    return y.reshape(bsz, n, POOL_WIDTH) * pool_scale


def stream_features(h, lp, rows):
    bsz, n, _ = h.shape
    f32 = jnp.float32
    u = h @ lp['w_in']
    z, xbc, dt_raw, pool_in, sc_b, sc_c, sc_x, rq, rk, rv, rg = jnp.split(u, SPLIT_OFFSETS, axis=-1)
    xbc = jax.nn.silu(dwconv_centered(xbc, lp['ssd_conv_w']) + lp['ssd_conv_b'])
    xs, bm, cm = jnp.split(xbc, (SSD_INNER, SSD_INNER + SSD_GROUPS * SSD_STATE), axis=-1)
    xh = xs.reshape(bsz, n, SSD_HEADS, SSD_HEAD_DIM)
    rep = SSD_HEADS // SSD_GROUPS
    bm = jnp.repeat(bm.reshape(bsz, n, SSD_GROUPS, SSD_STATE), rep, axis=2)
    cm = jnp.repeat(cm.reshape(bsz, n, SSD_GROUPS, SSD_STATE), rep, axis=2)
    dt = jax.nn.softplus(dt_raw.astype(f32)[:, :, None, :] + lp['ssd_dt_bias'].astype(f32))
    a = -jnp.exp(lp['ssd_a_log'].astype(f32))
    la = dt * a
    v = xh[:, :, None] * dt[..., None]
    ssd = (cm, bm, v[:, :, 0], v[:, :, 1], la[:, :, 0], la[:, :, 1])
    q = rq.reshape(bsz, n, RET_HEADS, RET_K_HEAD) * (RET_K_HEAD ** -0.5)
    k = rk.reshape(bsz, n, RET_HEADS, RET_K_HEAD)
    if rows is not None:
        q = axial_rope(q, rows)
        k = axial_rope(k, rows)
    vr = rv.reshape(bsz, n, RET_HEADS, RET_V_HEAD)
    la_r = jnp.broadcast_to(jax.nn.log_sigmoid(lp['ret_decay_logit'].astype(f32)), (bsz, n, 2, RET_HEADS))
    ret = (q, k, vr, vr, la_r[:, :, 0], la_r[:, :, 1])
    return dict(ssd=ssd, ret=ret, xh=xh, z=z, pool_in=pool_in, sc=(sc_b, sc_c, sc_x), rg=rg)


def mixer_out(h, f, y_ssd_scan, y_ret_scan, lp):
    bsz, n, _ = h.shape
    y = (y_ssd_scan + f['xh'] * lp['ssd_d'][:, None]).reshape(bsz, n, SSD_INNER)
    y_ssd = rmsnorm(y * jax.nn.silu(f['z']), lp['ssd_norm_w'])
    y_pool = multiscale_pool(f['pool_in'], lp['pool_w'], lp['pool_scale'])
    sc_b, sc_c, sc_x = f['sc']
    y_sc = sc_b * dwconv_centered(sc_c * sc_x, lp['sconv_w'])
    y_ret = jax.nn.silu(f['rg']) * head_layernorm(y_ret_scan).reshape(bsz, n, RET_V)
    merged = None
    for i, yb in enumerate((y_ssd, y_pool, y_sc, y_ret)):
        gate = jax.nn.sigmoid(h @ lp['w_gate'][i] + lp['b_gate'][i])
        term = gate * (yb @ lp['w_branch'][i])
        merged = term if merged is None else merged + term
    return merged @ lp['w_o']


def conv_ffn(h, lp):
    u = dwconv_centered(h @ lp['ffn_up'], lp['ffn_conv_w']) + lp['ffn_conv_b']
    a, b = jnp.split(u, 2, axis=-1)
    return (jax.nn.silu(a) * b) @ lp['ffn_down']


def setup_inputs(seed: int = 0) -> dict:
    key = jax.random.key(seed)
    k = jax.random.split(key, 28)
    f32 = jnp.float32
    L, D = DEPTH, D_MODEL

    def nrm(kk, shape, scale):
        return jax.random.normal(kk, shape, f32) * scale

    def gain(kk, shape):
        return 1.0 + nrm(kk, shape, 0.02)

    dt0 = jnp.exp(jax.random.uniform(k[11], (L, 2, SSD_HEADS), f32, math.log(1e-3), math.log(1e-1)))
    gam = 1.0 - jnp.exp2(-(5.0 + jnp.arange(RET_HEADS, dtype=f32)))
    gam_logit = jnp.log(gam) - jnp.log1p(-gam)
    return {
        'x': nrm(k[0], (BATCH, SEQ, D), 1.0),
        'c': nrm(k[1], (BATCH, D), 1.0),
        'ctx': nrm(k[2], (BATCH, CTX_LEN, D), 1.0),
        'c_ctx': nrm(k[3], (D,), 1.0),
        'w_mod': nrm(k[4], (L, D, 6 * D), 0.5 * D ** -0.5),
        'b_mod': nrm(k[5], (L, 6 * D), 0.01),
        'norm1_w': gain(k[6], (L, D)),
        'w_in': nrm(k[7], (L, D, IN_COLS), D ** -0.5),
        'ssd_conv_w': nrm(k[8], (L, CONV_K, SSD_CONV_CH), CONV_K ** -0.5),
        'ssd_conv_b': nrm(k[9], (L, SSD_CONV_CH), 0.01),
        'ssd_a_log': jnp.log(jax.random.uniform(k[10], (L, 2, SSD_HEADS), f32, 1.0, 16.0)),
        'ssd_dt_bias': dt0 + jnp.log(-jnp.expm1(-dt0)),
        'ssd_d': 1.0 + nrm(k[12], (L, SSD_HEADS), 0.1),
        'ssd_norm_w': gain(k[13], (L, SSD_INNER)),
        'pool_w': nrm(k[14], (L, N_POOL, POOL_GROUP, POOL_GROUP), POOL_GROUP ** -0.5),
        'pool_scale': gain(k[15], (L, POOL_WIDTH)),
        'sconv_w': nrm(k[16], (L, CONV_K, SC_WIDTH), CONV_K ** -0.5),
        'ret_decay_logit': jnp.broadcast_to(gam_logit, (L, 2, RET_HEADS)) + nrm(k[17], (L, 2, RET_HEADS), 0.05),
        'w_branch': nrm(k[18], (L, N_BRANCH, BRANCH_WIDTH, D), BRANCH_WIDTH ** -0.5),
        'w_gate': nrm(k[19], (L, N_BRANCH, D, D), D ** -0.5),
        'b_gate': nrm(k[20], (L, N_BRANCH, D), 0.01),
        'w_o': nrm(k[21], (L, D, D), D ** -0.5),
        'norm2_w': gain(k[22], (L, D)),
        'ffn_up': nrm(k[23], (L, D, 2 * D_FF), D ** -0.5),
        'ffn_conv_w': nrm(k[24], (L, CONV_K, 2 * D_FF), CONV_K ** -0.5),
        'ffn_conv_b': nrm(k[25], (L, 2 * D_FF), 0.01),
        'ffn_down': nrm(k[26], (L, D_FF, D), D_FF ** -0.5),
        'final_norm_w': gain(k[27], (D,)),
    }


def reference(x, c, ctx, c_ctx, w_mod, b_mod, norm1_w, w_in, ssd_conv_w, ssd_conv_b, ssd_a_log,
              ssd_dt_bias, ssd_d, ssd_norm_w, pool_w, pool_scale, sconv_w, ret_decay_logit,
              w_branch, w_gate, b_gate, w_o, norm2_w, ffn_up, ffn_conv_w, ffn_conv_b, ffn_down,
              final_norm_w):
    bsz, n_lat, _ = x.shape
    rows = n_lat // GRID_W
    xl, xc = x, ctx
    ssd0 = jnp.zeros((bsz, SSD_HEADS, SSD_STATE, SSD_HEAD_DIM), jnp.float32)
    ret0 = jnp.zeros((bsz, RET_HEADS, RET_K_HEAD, RET_V_HEAD), jnp.float32)
    for l in range(DEPTH):
        lp = dict(w_in=w_in[l], ssd_conv_w=ssd_conv_w[l], ssd_conv_b=ssd_conv_b[l],
                  ssd_a_log=ssd_a_log[l], ssd_dt_bias=ssd_dt_bias[l], ssd_d=ssd_d[l],
                  ssd_norm_w=ssd_norm_w[l], pool_w=pool_w[l], pool_scale=pool_scale[l],
                  sconv_w=sconv_w[l], ret_decay_logit=ret_decay_logit[l], w_branch=w_branch[l],
                  w_gate=w_gate[l], b_gate=b_gate[l], w_o=w_o[l], ffn_up=ffn_up[l],
                  ffn_conv_w=ffn_conv_w[l], ffn_conv_b=ffn_conv_b[l], ffn_down=ffn_down[l])
        sh1, sc1, g1, sh2, sc2, g2 = jnp.split((jax.nn.silu(c) @ w_mod[l] + b_mod[l])[:, None, :], 6, axis=-1)
        csh1, csc1, cg1, csh2, csc2, cg2 = jnp.split(jax.nn.silu(c_ctx) @ w_mod[l] + b_mod[l], 6, axis=-1)
        hl = modulate(rmsnorm(xl, norm1_w[l]), sh1, sc1)
        hc = modulate(rmsnorm(xc, norm1_w[l]), csh1, csc1)
        fl = stream_features(hl, lp, rows)
        fc = stream_features(hc, lp, None)
        yc_ssd, hf_ssd, hb_ssd = bidir_scan(*fc['ssd'], ssd0, ssd0)
        yc_ret, hf_ret, hb_ret = bidir_scan(*fc['ret'], ret0, ret0)
        yl_ssd, _, _ = bidir_scan(*fl['ssd'], hf_ssd, hb_ssd)
        yl_ret, _, _ = bidir_scan(*fl['ret'], hf_ret, hb_ret)
        xl = xl + g1 * mixer_out(hl, fl, yl_ssd, yl_ret, lp)
        xl = xl + g2 * conv_ffn(modulate(rmsnorm(xl, norm2_w[l]), sh2, sc2), lp)
        if l < DEPTH - 1:
            xc = xc + cg1 * mixer_out(hc, fc, yc_ssd, yc_ret, lp)
            xc = xc + cg2 * conv_ffn(modulate(rmsnorm(xc, norm2_w[l]), csh2, csc2), lp)
    return rmsnorm(xl, final_norm_w)
```

```python
import functools
import math

import jax
import jax.numpy as jnp
from jax import lax
from jax.experimental import pallas as pl
from jax.experimental.pallas import tpu as pltpu

F32 = jnp.float32
BF16 = jnp.bfloat16

D_MODEL = 2048
GRID_W = 64
EPS = 1e-6
CHUNK = 128
BRANCH_WIDTH = D_MODEL // 2
SSD_INNER = BRANCH_WIDTH
SSD_HEAD_DIM = 64
SSD_HEADS = SSD_INNER // SSD_HEAD_DIM
SSD_GROUPS = 4
SSD_STATE = 128
SSD_CONV_CH = SSD_INNER + 2 * SSD_GROUPS * SSD_STATE
POOL_WINDOWS = (2, 4, 8, 16)
POOL_GROUP = BRANCH_WIDTH // len(POOL_WINDOWS)
RET_HEADS = 8
RET_V_HEAD = BRANCH_WIDTH // RET_HEADS
RET_K_HEAD = RET_V_HEAD // 2
RET_QK = RET_HEADS * RET_K_HEAD
ROPE_BASE = 10000.0
D_FF = 256 * ((8 * D_MODEL // 3 + 255) // 256)

LANES = 128
HALO = 16
VMEM_LIMIT = 56 * 1024 * 1024

U_Z = 0
U_XBC = 1024
U_POOL = 3072
U_SCB = 4096
U_SCC = 5120
U_SCX = 6144
U_RQ = 7168
U_RK = 7680
U_RV = 8192
U_RG = 9216
U_COLS = 10240


def _params(sem):
    return pltpu.CompilerParams(dimension_semantics=sem, vmem_limit_bytes=VMEM_LIMIT)


def _silu(v):
    return v * jax.nn.sigmoid(v)


def _pick(cands, *ns):
    for c in cands:
        if all(n % c == 0 for n in ns):
            return c
    raise ValueError(f"no tile in {cands} divides {ns}")


def _mod_const(v, n):
    if n & (n - 1) == 0:
        return v & (n - 1)
    return lax.rem(v, n)


def _seq_pos(r0, rows, bc, ctx, seq):
    in_ctx = r0 < bc
    g = r0 + lax.broadcasted_iota(jnp.int32, (rows, 1), 0)
    pos = jnp.where(in_ctx, _mod_const(g, ctx), _mod_const(g - bc, seq))
    n = jnp.where(in_ctx, ctx, seq)
    return pos, n


def _mod_kernel(c_ref, w_ref, b_ref, o_ref):
    a = _silu(c_ref[...]).astype(BF16)
    o_ref[...] = jnp.dot(a, w_ref[...].astype(BF16), preferred_element_type=F32) + b_ref[...]


def _mod_call(cvec, w_mod, b_mod):
    nl, d, n6 = w_mod.shape
    r = cvec.shape[0]
    tn = 1024
    return pl.pallas_call(
        _mod_kernel,
        grid=(nl, n6 // tn),
        in_specs=[pl.BlockSpec((r, d), lambda l, j: (0, 0)),
                  pl.BlockSpec((None, d, tn), lambda l, j: (l, 0, j)),
                  pl.BlockSpec((None, 1, tn), lambda l, j: (l, 0, j))],
        out_specs=pl.BlockSpec((None, r, tn), lambda l, j: (l, 0, j)),
        out_shape=jax.ShapeDtypeStruct((nl, r, n6), F32),
        compiler_params=_params(("parallel", "parallel")),
        name="mod_vectors",
    )(cvec, w_mod, b_mod.reshape(nl, 1, n6))


def _normed(x, nw, sh, sc):
    y = x * lax.rsqrt(jnp.mean(jnp.square(x), axis=-1, keepdims=True) + EPS)
    return (y * nw) * (1.0 + sc) + sh


def _in_proj_kernel(x_ref, nw_ref, sh_ref, sc_ref, w_ref, wdt_ref, u_ref, h_ref, dt_ref):
    @pl.when(pl.program_id(1) == 0)
    def _():
        h = _normed(x_ref[...], nw_ref[...], sh_ref[...], sc_ref[...]).astype(BF16)
        h_ref[...] = h
        dt_ref[...] = jnp.dot(h, wdt_ref[...], preferred_element_type=F32)

    u_ref[...] = jnp.dot(h_ref[...], w_ref[...], preferred_element_type=F32).astype(BF16)


def _up_proj_kernel(x_ref, nw_ref, sh_ref, sc_ref, w_ref, u_ref, h_s):
    @pl.when(pl.program_id(1) == 0)
    def _():
        h_s[...] = _normed(x_ref[...], nw_ref[...], sh_ref[...], sc_ref[...]).astype(BF16)

    u_ref[...] = jnp.dot(h_s[...], w_ref[...], preferred_element_type=F32).astype(BF16)


def _mod_spec(l, k, tn, row_of, col_of):
    return pl.BlockSpec((None, None, None, 1, tn),
                        lambda i, j: (l, row_of(i), k, 0, col_of(j)))


def _in_proj_call(x, norm_w, mod5, w, wdt, l, k_sh, k_sc, tm, row_of):
    t, d = x.shape
    n = w.shape[-1]
    tn = 1024
    zero = lambda j: 0
    return pl.pallas_call(
        _in_proj_kernel,
        grid=(t // tm, n // tn),
        in_specs=[pl.BlockSpec((tm, d), lambda i, j: (i, 0)),
                  pl.BlockSpec((None, 1, d), lambda i, j: (l, 0, 0)),
                  _mod_spec(l, k_sh, d, row_of, zero),
                  _mod_spec(l, k_sc, d, row_of, zero),
                  pl.BlockSpec((None, d, tn), lambda i, j: (l, 0, j)),
                  pl.BlockSpec((None, d, LANES), lambda i, j: (l, 0, 0))],
        out_specs=[pl.BlockSpec((tm, tn), lambda i, j: (i, j)),
                   pl.BlockSpec((tm, d), lambda i, j: (i, 0)),
                   pl.BlockSpec((tm, LANES), lambda i, j: (i, 0))],
        out_shape=[jax.ShapeDtypeStruct((t, n), BF16),
                   jax.ShapeDtypeStruct((t, d), BF16),
                   jax.ShapeDtypeStruct((t, LANES), F32)],
        compiler_params=_params(("parallel", "arbitrary")),
        name="in_proj",
    )(x, norm_w, mod5, mod5, w, wdt)


def _up_proj_call(x, norm_w, mod5, w, l, k_sh, k_sc, tm, row_of):
    t, d = x.shape
    n = w.shape[-1]
    tn = 1024
    zero = lambda j: 0
    return pl.pallas_call(
        _up_proj_kernel,
        grid=(t // tm, n // tn),
        in_specs=[pl.BlockSpec((tm, d), lambda i, j: (i, 0)),
                  pl.BlockSpec((None, 1, d), lambda i, j: (l, 0, 0)),
                  _mod_spec(l, k_sh, d, row_of, zero),
                  _mod_spec(l, k_sc, d, row_of, zero),
                  pl.BlockSpec((None, d, tn), lambda i, j: (l, 0, j))],
        out_specs=pl.BlockSpec((tm, tn), lambda i, j: (i, j)),
        out_shape=jax.ShapeDtypeStruct((t, n), BF16),
        scratch_shapes=[pltpu.VMEM((tm, d), BF16)],
        compiler_params=_params(("parallel", "arbitrary")),
        name="ffn_up_proj",
    )(x, norm_w, mod5, mod5, w)


def _merge_kernel(h_ref, y0_ref, y1_ref, y2_ref, y3_ref, wg_ref, bg_ref, wb_ref, o_ref):
    h = h_ref[...]
    acc = None
    for i, y_ref in enumerate((y0_ref, y1_ref, y2_ref, y3_ref)):
        gate = jax.nn.sigmoid(jnp.dot(h, wg_ref[i], preferred_element_type=F32) + bg_ref[i])
        term = gate * jnp.dot(y_ref[...], wb_ref[i], preferred_element_type=F32)
        acc = term if acc is None else acc + term
    o_ref[...] = acc.astype(BF16)


def _merge_call(h, ys, wg, bg, wb, l, tm):
    t, d = h.shape
    bw = ys[0].shape[-1]
    tn = 256
    nb = len(ys)
    return pl.pallas_call(
        _merge_kernel,
        grid=(t // tm, d // tn),
        in_specs=[pl.BlockSpec((tm, d), lambda i, j: (i, 0))]
                 + [pl.BlockSpec((tm, bw), lambda i, j: (i, 0)) for _ in ys]
                 + [pl.BlockSpec((None, nb, d, tn), lambda i, j: (l, 0, 0, j)),
                    pl.BlockSpec((None, nb, 1, tn), lambda i, j: (l, 0, 0, j)),
                    pl.BlockSpec((None, nb, bw, tn), lambda i, j: (l, 0, 0, j))],
        out_specs=pl.BlockSpec((tm, tn), lambda i, j: (i, j)),
        out_shape=jax.ShapeDtypeStruct((t, d), BF16),
        compiler_params=_params(("parallel", "arbitrary")),
        name="branch_merge",
    )(h, *ys, wg, bg, wb)


def _res_kernel(a_ref, w_ref, x_ref, g_ref, o_ref):
    o_ref[...] = x_ref[...] + g_ref[...] * jnp.dot(a_ref[...], w_ref[...],
                                                   preferred_element_type=F32)


def _res_call(a, w, x, mod5, l, k_gate, tm, row_of, name):
    t, kdim = a.shape
    d = x.shape[-1]
    tn = 512
    return pl.pallas_call(
        _res_kernel,
        grid=(t // tm, d // tn),
        in_specs=[pl.BlockSpec((tm, kdim), lambda i, j: (i, 0)),
                  pl.BlockSpec((None, kdim, tn), lambda i, j: (l, 0, j)),
                  pl.BlockSpec((tm, tn), lambda i, j: (i, j)),
                  _mod_spec(l, k_gate, tn, row_of, lambda j: j)],
        out_specs=pl.BlockSpec((tm, tn), lambda i, j: (i, j)),
        out_shape=jax.ShapeDtypeStruct((t, d), F32),
        compiler_params=_params(("parallel", "arbitrary")),
        name=name,
    )(a, w, x, mod5)


def _final_norm_kernel(x_ref, w_ref, o_ref):
    x = x_ref[...]
    o_ref[...] = (x * lax.rsqrt(jnp.mean(jnp.square(x), axis=-1, keepdims=True) + EPS)) * w_ref[...]


def _final_norm_call(x, w, row0, rows, tm):
    d = x.shape[-1]
    off = row0 // tm
    return pl.pallas_call(
        _final_norm_kernel,
        grid=(rows // tm,),
        in_specs=[pl.BlockSpec((tm, d), lambda i: (i + off, 0)),
                  pl.BlockSpec((1, d), lambda i: (0, 0))],
        out_specs=pl.BlockSpec((tm, d), lambda i: (i, 0)),
        out_shape=jax.ShapeDtypeStruct((rows, d), F32),
        compiler_params=_params(("parallel",)),
        name="final_norm",
    )(x, w)


def _halo_specs(rt, cols, col_idx, t):
    per = rt // HALO
    last = t // HALO - 1
    return [pl.BlockSpec((rt, cols), lambda i, j: (i, col_idx(j))),
            pl.BlockSpec((HALO, cols), lambda i, j: (jnp.maximum(i * per - 1, 0), col_idx(j))),
            pl.BlockSpec((HALO, cols), lambda i, j: (jnp.minimum((i + 1) * per, last), col_idx(j)))]


def _shift_pm1(x, prev_row, next_row, pos, n):
    rt = x.shape[0]
    row = lax.broadcasted_iota(jnp.int32, (rt, 1), 0)
    xm1 = jnp.where(row == 0, prev_row, pltpu.roll(x, 1, 0))
    xm1 = jnp.where(pos == 0, 0.0, xm1)
    xp1 = jnp.where(row == rt - 1, next_row, pltpu.roll(x, rt - 1, 0))
    xp1 = jnp.where(pos == n - 1, 0.0, xp1)
    return xm1, xp1


def _conv3(x, prev_row, next_row, w, pos, n):
    xm1, xp1 = _shift_pm1(x, prev_row, next_row, pos, n)
    return xm1 * w[0:1] + x * w[1:2] + xp1 * w[2:3]


def _ssd_conv_kernel(x_ref, p_ref, n_ref, w_ref, b_ref, o_ref, *, bc, ctx, seq):
    rt = x_ref.shape[0]
    pos, n = _seq_pos(pl.program_id(0) * rt, rt, bc, ctx, seq)
    x = x_ref[...].astype(F32)
    prev_row = p_ref[...].astype(F32)[HALO - 1:HALO]
    next_row = n_ref[...].astype(F32)[0:1]
    y = _conv3(x, prev_row, next_row, w_ref[...], pos, n) + b_ref[...]
    o_ref[...] = _silu(y).astype(BF16)


def _ssd_conv_call(u, w, b, l, rt, bc, ctx, seq):
    t = u.shape[0]
    cols = 1024
    nj = SSD_CONV_CH // cols
    base = U_XBC // cols
    return pl.pallas_call(
        functools.partial(_ssd_conv_kernel, bc=bc, ctx=ctx, seq=seq),
        grid=(t // rt, nj),
        in_specs=_halo_specs(rt, cols, lambda j: base + j, t)
                 + [pl.BlockSpec((None, 3, cols), lambda i, j: (l, 0, j)),
                    pl.BlockSpec((None, 1, cols), lambda i, j: (l, 0, j))],
        out_specs=pl.BlockSpec((rt, cols), lambda i, j: (i, j)),
        out_shape=jax.ShapeDtypeStruct((t, SSD_CONV_CH), BF16),
        compiler_params=_params(("parallel", "parallel")),
        name="ssd_conv",
    )(u, u, u, w, b)


def _sconv_kernel(b_ref, c_ref, cp_ref, cn_ref, x_ref, xp_ref, xn_ref, w_ref, o_ref, *, bc, ctx, seq):
    rt = b_ref.shape[0]
    pos, n = _seq_pos(pl.program_id(0) * rt, rt, bc, ctx, seq)
    cx = c_ref[...].astype(F32) * x_ref[...].astype(F32)
    prev_row = (cp_ref[...].astype(F32) * xp_ref[...].astype(F32))[HALO - 1:HALO]
    next_row = (cn_ref[...].astype(F32) * xn_ref[...].astype(F32))[0:1]
    y = b_ref[...].astype(F32) * _conv3(cx, prev_row, next_row, w_ref[...], pos, n)
    o_ref[...] = y.astype(BF16)


def _sconv_call(u, w, l, rt, bc, ctx, seq):
    t = u.shape[0]
    cols = BRANCH_WIDTH
    return pl.pallas_call(
        functools.partial(_sconv_kernel, bc=bc, ctx=ctx, seq=seq),
        grid=(t // rt, 1),
        in_specs=[pl.BlockSpec((rt, cols), lambda i, j: (i, U_SCB // cols))]
                 + _halo_specs(rt, cols, lambda j: U_SCC // cols, t)
                 + _halo_specs(rt, cols, lambda j: U_SCX // cols, t)
                 + [pl.BlockSpec((None, 3, cols), lambda i, j: (l, 0, 0))],
        out_specs=pl.BlockSpec((rt, cols), lambda i, j: (i, 0)),
        out_shape=jax.ShapeDtypeStruct((t, cols), BF16),
        compiler_params=_params(("parallel", "parallel")),
        name="short_conv",
    )(u, u, u, u, u, u, u, w)


def _ffn_conv_kernel(a_ref, ap_ref, an_ref, b_ref, bp_ref, bn_ref, wa_ref, wb_ref, ba_ref, bb_ref,
                     o_ref, *, bc, ctx, seq):
    rt = a_ref.shape[0]
    pos, n = _seq_pos(pl.program_id(0) * rt, rt, bc, ctx, seq)

    def conv(m_ref, p_ref, n_ref, w_ref, bias_ref):
        x = m_ref[...].astype(F32)
        prev_row = p_ref[...].astype(F32)[HALO - 1:HALO]
        next_row = n_ref[...].astype(F32)[0:1]
        return _conv3(x, prev_row, next_row, w_ref[...], pos, n) + bias_ref[...]

    a = conv(a_ref, ap_ref, an_ref, wa_ref, ba_ref)
    b = conv(b_ref, bp_ref, bn_ref, wb_ref, bb_ref)
    o_ref[...] = (_silu(a) * b).astype(BF16)


def _ffn_conv_call(u2, w, b, l, rt, bc, ctx, seq):
    t = u2.shape[0]
    cols = 512
    nj = D_FF // cols
    return pl.pallas_call(
        functools.partial(_ffn_conv_kernel, bc=bc, ctx=ctx, seq=seq),
        grid=(t // rt, nj),
        in_specs=_halo_specs(rt, cols, lambda j: j, t)
                 + _halo_specs(rt, cols, lambda j: nj + j, t)
                 + [pl.BlockSpec((None, 3, cols), lambda i, j: (l, 0, j)),
                    pl.BlockSpec((None, 3, cols), lambda i, j: (l, 0, nj + j)),
                    pl.BlockSpec((None, 1, cols), lambda i, j: (l, 0, j)),
                    pl.BlockSpec((None, 1, cols), lambda i, j: (l, 0, nj + j))],
        out_specs=pl.BlockSpec((rt, cols), lambda i, j: (i, j)),
        out_shape=jax.ShapeDtypeStruct((t, D_FF), BF16),
        compiler_params=_params(("parallel", "parallel")),
        name="ffn_conv_gate",
    )(u2, u2, u2, u2, u2, u2, w, w, b, b)


POOL_ROWS = 256


def _pool_kernel(x_ref, p_ref, n_ref, w_ref, s_ref, o_ref, *, bc, ctx, seq):
    rt = POOL_ROWS
    r0 = pl.program_id(0) * rt
    pos, n = _seq_pos(r0, rt, bc, ctx, seq)
    in_ctx = r0 < bc
    p0 = jnp.where(in_ctx, _mod_const(r0, ctx), _mod_const(r0 - bc, seq))
    has_prev = p0 != 0
    has_next = p0 + rt != n
    ext_rows = rt + 2 * HALO
    for gi, win in enumerate(POOL_WINDOWS):
        cs = slice(gi * POOL_GROUP, (gi + 1) * POOL_GROUP)
        g = x_ref[:, cs].astype(F32)
        prev = jnp.where(has_prev, p_ref[:, cs].astype(F32), 0.0)
        nxt = jnp.where(has_next, n_ref[:, cs].astype(F32), 0.0)
        ext = jnp.concatenate([prev, g, nxt], axis=0)
        s = ext + pltpu.roll(ext, 1, 0)
        half = 1
        while 2 * half < win:
            s = pltpu.roll(s, half, 0) + pltpu.roll(s, ext_rows - half, 0)
            half *= 2
        s = s[HALO:HALO + rt]
        lo = jnp.clip(pos - win // 2, 0, n)
        hi = jnp.clip(pos - win // 2 + win, 0, n)
        mean = s / (hi - lo).astype(F32)
        pooled = (mean - g).astype(BF16)
        y = jnp.dot(pooled, w_ref[gi], preferred_element_type=F32) * s_ref[:, cs]
        o_ref[:, cs] = y.astype(BF16)


def _pool_call(u, w, scale, l, bc, ctx, seq):
    t = u.shape[0]
    cols = BRANCH_WIDTH
    rt = POOL_ROWS
    ng = len(POOL_WINDOWS)
    return pl.pallas_call(
        functools.partial(_pool_kernel, bc=bc, ctx=ctx, seq=seq),
        grid=(t // rt, 1),
        in_specs=_halo_specs(rt, cols, lambda j: U_POOL // cols, t)
                 + [pl.BlockSpec((None, ng, POOL_GROUP, POOL_GROUP), lambda i, j: (l, 0, 0, 0)),
                    pl.BlockSpec((None, 1, cols), lambda i, j: (l, 0, 0))],
        out_specs=pl.BlockSpec((rt, cols), lambda i, j: (i, 0)),
        out_shape=jax.ShapeDtypeStruct((t, cols), BF16),
        compiler_params=_params(("parallel", "parallel")),
        name="multiscale_pool",
    )(u, u, u, w, scale)


def _scan_positions(ncc, nk):
    def kpos(sw, s):
        back = jnp.where(s < ncc, ncc - 1 - s, nk - 1 - (s - ncc))
        return jnp.where(sw == 0, back, s)
    return kpos


def _chunk_row(b, k, ncc, ncl, nb):
    return jnp.where(k < ncc, b * ncc + k, nb * ncc + b * ncl + (k - ncc))


def _tri_masks():
    row = lax.broadcasted_iota(jnp.int32, (CHUNK, CHUNK), 0)
    col = lax.broadcasted_iota(jnp.int32, (CHUNK, CHUNK), 1)
    return row, col, col <= row, col >= row


def _colb(x, h):
    return jnp.broadcast_to(x[:, h:h + 1], (x.shape[0], LANES))


def _ssd_scan_kernel(xbc_ref, dt_ref, z_ref, dtb_ref, alog_ref, dexp_ref, nw_ref, o_ref,
                     hb_all, hf_s, hb_s, y_s, *, ncc, nk):
    sw = pl.program_id(1)
    s = pl.program_id(2)
    kpos = _scan_positions(ncc, nk)(sw, s)
    row, col, lower, upper = _tri_masks()
    lane_lo = col < SSD_HEAD_DIM
    row_lo1 = lane_lo[0:1]

    dt_raw = dt_ref[...]
    a = -jnp.exp(alog_ref[...])
    dtf = jax.nn.softplus(dt_raw + dtb_ref[0:1])
    dtb = jax.nn.softplus(dt_raw + dtb_ref[1:2])
    laf = dtf * a[0:1]
    lab = dtb * a[1:2]
    hp = lax.Precision.HIGHEST
    cf = jnp.dot(lower.astype(F32), laf, precision=hp, preferred_element_type=F32)
    rb = jnp.dot(upper.astype(F32), lab, precision=hp, preferred_element_type=F32)

    xs = xbc_ref[:, 0:SSD_INNER]
    xs32 = xs.astype(F32)
    n_pairs = SSD_HEADS // 2
    gw = SSD_INNER // SSD_GROUPS

    def pair_cols(x, p):
        return jnp.where(lane_lo, _colb(x, 2 * p), _colb(x, 2 * p + 1))

    def pair_row(x_row, p):
        return jnp.where(row_lo1, jnp.broadcast_to(x_row[:, 2 * p:2 * p + 1], (1, LANES)),
                         jnp.broadcast_to(x_row[:, 2 * p + 1:2 * p + 2], (1, LANES)))

    def state_update(h_s, w_tok, dec_row):
        vw = jnp.concatenate([xs32[:, p * LANES:(p + 1) * LANES] * pair_cols(w_tok, p)
                              for p in range(n_pairs)], axis=1).astype(BF16)
        dec = jnp.concatenate([pair_row(dec_row, p) for p in range(n_pairs)], axis=1)
        for g in range(SSD_GROUPS):
            bm = xbc_ref[:, SSD_INNER + g * SSD_STATE:SSD_INNER + (g + 1) * SSD_STATE]
            st = lax.dot_general(bm, vw[:, g * gw:(g + 1) * gw], (((0,), (0,)), ((), ())),
                                 preferred_element_type=F32)
            h_s[:, g * gw:(g + 1) * gw] = h_s[:, g * gw:(g + 1) * gw] * dec[:, g * gw:(g + 1) * gw] + st

    @pl.when(sw == 0)
    def _():
        @pl.when(s == 0)
        def _():
            hb_s[...] = jnp.zeros_like(hb_s)

        hb_all[kpos] = hb_s[...].astype(BF16)
        total = rb[0:1]
        state_update(hb_s, jnp.exp(total - rb) * dtb, jnp.exp(total))

    @pl.when(sw == 1)
    def _():
        @pl.when(s == 0)
        def _():
            hf_s[...] = jnp.zeros_like(hf_s)

        cft, rbt, dtft, dtbt = cf.T, rb.T, dtf.T, dtb.T
        ecf = jnp.exp(cf)
        erb = jnp.exp(rb)
        hb_in = hb_all[kpos]
        heads_per_group = SSD_HEADS // SSD_GROUPS
        for g in range(SSD_GROUPS):
            bm = xbc_ref[:, SSD_INNER + g * SSD_STATE:SSD_INNER + (g + 1) * SSD_STATE]
            cm = xbc_ref[:, SSD_INNER + SSD_GROUPS * SSD_STATE + g * SSD_STATE:
                         SSD_INNER + SSD_GROUPS * SSD_STATE + (g + 1) * SSD_STATE]
            sc = lax.dot_general(cm, bm, (((1,), (1,)), ((), ())), preferred_element_type=F32)
            gf = jnp.dot(cm, hf_s[:, g * gw:(g + 1) * gw].astype(BF16), preferred_element_type=F32)
            gb = jnp.dot(cm, hb_in[:, g * gw:(g + 1) * gw], preferred_element_type=F32)
            for pp in range(heads_per_group // 2):
                p = g * (heads_per_group // 2) + pp
                xs_pair = xs[:, p * LANES:(p + 1) * LANES]
                res = []
                for h in (2 * p, 2 * p + 1):
                    seg_f = _colb(cf, h) - cft[h:h + 1]
                    seg_b = _colb(rb, h) - rbt[h:h + 1]
                    dec_f = jnp.where(lower, jnp.exp(jnp.where(lower, seg_f, 0.0)), 0.0)
                    dec_b = jnp.where(upper, jnp.exp(jnp.where(upper, seg_b, 0.0)), 0.0)
                    m = sc * (dec_f * dtft[h:h + 1] + dec_b * dtbt[h:h + 1])
                    res.append(jnp.dot(m.astype(BF16), xs_pair, preferred_element_type=F32))
                y_intra = jnp.where(lane_lo, res[0], res[1])
                lo = pp * LANES
                y_inter = gf[:, lo:lo + LANES] * pair_cols(ecf, p) + gb[:, lo:lo + LANES] * pair_cols(erb, p)
                y_s[:, p * LANES:(p + 1) * LANES] = y_intra + y_inter

        last = cf[CHUNK - 1:CHUNK]
        state_update(hf_s, jnp.exp(last - cf) * dtf, jnp.exp(last))

        y = (y_s[...] + xs32 * dexp_ref[...]) * _silu(z_ref[...].astype(F32))
        y = y * lax.rsqrt(jnp.mean(jnp.square(y), axis=-1, keepdims=True) + EPS)
        o_ref[...] = (y * nw_ref[...]).astype(BF16)


def _ssd_scan_call(xbc, dt, u, dt_bias, a_log, d_exp, norm_w, l, nb, ctx, seq):
    t = xbc.shape[0]
    ncc, ncl = ctx // CHUNK, seq // CHUNK
    nk = ncc + ncl
    kpos = _scan_positions(ncc, nk)

    def in_row(b, sw, s):
        return _chunk_row(b, kpos(sw, s), ncc, ncl, nb)

    def out_row(b, sw, s):
        return _chunk_row(b, jnp.where(sw == 0, 0, s), ncc, ncl, nb)

    return pl.pallas_call(
        functools.partial(_ssd_scan_kernel, ncc=ncc, nk=nk),
        grid=(nb, 2, nk),
        in_specs=[pl.BlockSpec((CHUNK, SSD_CONV_CH), lambda b, sw, s: (in_row(b, sw, s), 0)),
                  pl.BlockSpec((CHUNK, LANES), lambda b, sw, s: (in_row(b, sw, s), 0)),
                  pl.BlockSpec((CHUNK, SSD_INNER), lambda b, sw, s: (in_row(b, sw, s), U_Z // SSD_INNER)),
                  pl.BlockSpec((None, 2, LANES), lambda b, sw, s: (l, 0, 0)),
                  pl.BlockSpec((None, 2, LANES), lambda b, sw, s: (l, 0, 0)),
                  pl.BlockSpec((None, 1, SSD_INNER), lambda b, sw, s: (l, 0, 0)),
                  pl.BlockSpec((None, 1, SSD_INNER), lambda b, sw, s: (l, 0, 0))],
        out_specs=pl.BlockSpec((CHUNK, SSD_INNER), lambda b, sw, s: (out_row(b, sw, s), 0)),
        out_shape=jax.ShapeDtypeStruct((t, SSD_INNER), BF16),
        scratch_shapes=[pltpu.VMEM((nk, SSD_STATE, SSD_INNER), BF16),
                        pltpu.VMEM((SSD_STATE, SSD_INNER), F32),
                        pltpu.VMEM((SSD_STATE, SSD_INNER), F32),
                        pltpu.VMEM((CHUNK, SSD_INNER), F32)],
        compiler_params=_params(("parallel", "arbitrary", "arbitrary")),
        name="ssd_scan",
    )(xbc, dt, u, dt_bias, a_log, d_exp, norm_w)


def _ret_scan_kernel(q_ref, k_ref, v_ref, g_ref, cos_ref, sin_ref, dl_ref, o_ref,
                     hb_all, hf_s, hb_s, *, ncc, nk):
    sw = pl.program_id(1)
    s = pl.program_id(2)
    kpos = _scan_positions(ncc, nk)(sw, s)
    row, col, lower, upper = _tri_masks()
    lane_lo = col < RET_K_HEAD
    la = jax.nn.log_sigmoid(dl_ref[...])
    rowf = row.astype(F32)
    diff = (row - col).astype(F32)
    n_pairs = RET_HEADS // 2

    cosv = cos_ref[...]
    sinv = sin_ref[...]
    first16 = (col & 31) < 16

    def rope(x):
        swapped = jnp.where(first16, pltpu.roll(x, LANES - 16, 1), pltpu.roll(x, 16, 1))
        return x * cosv + swapped * sinv

    def head_scalar(d, h):
        return la[d:d + 1, h:h + 1]

    def pair_k(p):
        return rope(k_ref[:, p * LANES:(p + 1) * LANES].astype(F32))

    def state_update(h_s, d, w_exponent):
        for p in range(n_pairs):
            kp = pair_k(p)
            acc = None
            for h in (2 * p, 2 * p + 1):
                mask = lane_lo if h % 2 == 0 else jnp.logical_not(lane_lo)
                km = jnp.where(mask, kp, 0.0).astype(BF16)
                w = jnp.exp(w_exponent * head_scalar(d, h))
                vw = (v_ref[:, h * RET_V_HEAD:(h + 1) * RET_V_HEAD].astype(F32) * w).astype(BF16)
                st = lax.dot_general(km, vw, (((0,), (0,)), ((), ())), preferred_element_type=F32)
                acc = st if acc is None else acc + st
            dec = jnp.where(row < RET_K_HEAD, jnp.exp(float(CHUNK) * head_scalar(d, 2 * p)),
                            jnp.exp(float(CHUNK) * head_scalar(d, 2 * p + 1)))
            rs = slice(p * LANES, (p + 1) * LANES)
            h_s[rs, :] = h_s[rs, :] * dec + acc

    @pl.when(sw == 0)
    def _():
        @pl.when(s == 0)
        def _():
            hb_s[...] = jnp.zeros_like(hb_s)

        hb_all[kpos] = hb_s[...].astype(BF16)
        state_update(hb_s, 1, rowf)

    @pl.when(sw == 1)
    def _():
        @pl.when(s == 0)
        def _():
            hf_s[...] = jnp.zeros_like(hf_s)

        hb_in = hb_all[kpos]
        for p in range(n_pairs):
            qp = rope(q_ref[:, p * LANES:(p + 1) * LANES].astype(F32) * (RET_K_HEAD ** -0.5))
            kp = pair_k(p).astype(BF16)
            rs = slice(p * LANES, (p + 1) * LANES)
            hf_pair = hf_s[rs, :].astype(BF16)
            hb_pair = hb_in[rs, :]
            for h in (2 * p, 2 * p + 1):
                mask = lane_lo if h % 2 == 0 else jnp.logical_not(lane_lo)
                qm = jnp.where(mask, qp, 0.0).astype(BF16)
                sc = lax.dot_general(qm, kp, (((1,), (1,)), ((), ())), preferred_element_type=F32)
                laf = head_scalar(0, h)
                lab = head_scalar(1, h)
                dec_f = jnp.where(lower, jnp.exp(jnp.where(lower, diff, 0.0) * laf), 0.0)
                dec_b = jnp.where(upper, jnp.exp(jnp.where(upper, -diff, 0.0) * lab), 0.0)
                m = (sc * (dec_f + dec_b)).astype(BF16)
                vs = slice(h * RET_V_HEAD, (h + 1) * RET_V_HEAD)
                y = jnp.dot(m, v_ref[:, vs], preferred_element_type=F32)
                y = y + jnp.dot(qm, hf_pair, preferred_element_type=F32) * jnp.exp((rowf + 1.0) * laf)
                y = y + jnp.dot(qm, hb_pair, preferred_element_type=F32) * jnp.exp((float(CHUNK) - rowf) * lab)
                mu = jnp.mean(y, axis=-1, keepdims=True)
                yc = y - mu
                var = jnp.mean(jnp.square(yc), axis=-1, keepdims=True)
                yn = yc * lax.rsqrt(var + EPS)
                o_ref[:, vs] = (_silu(g_ref[:, vs].astype(F32)) * yn).astype(BF16)

        state_update(hf_s, 0, float(CHUNK - 1) - rowf)


def _ret_scan_call(u, cos_t, sin_t, decay_logit, l, nb, ctx, seq):
    t = u.shape[0]
    ncc, ncl = ctx // CHUNK, seq // CHUNK
    nk = ncc + ncl
    kpos = _scan_positions(ncc, nk)

    def in_row(b, sw, s):
        return _chunk_row(b, kpos(sw, s), ncc, ncl, nb)

    def out_row(b, sw, s):
        return _chunk_row(b, jnp.where(sw == 0, 0, s), ncc, ncl, nb)

    bw = BRANCH_WIDTH
    return pl.pallas_call(
        functools.partial(_ret_scan_kernel, ncc=ncc, nk=nk),
        grid=(nb, 2, nk),
        in_specs=[pl.BlockSpec((CHUNK, RET_QK), lambda b, sw, s: (in_row(b, sw, s), U_RQ // RET_QK)),
                  pl.BlockSpec((CHUNK, RET_QK), lambda b, sw, s: (in_row(b, sw, s), U_RK // RET_QK)),
                  pl.BlockSpec((CHUNK, bw), lambda b, sw, s: (in_row(b, sw, s), U_RV // bw)),
                  pl.BlockSpec((CHUNK, bw), lambda b, sw, s: (in_row(b, sw, s), U_RG // bw)),
                  pl.BlockSpec((CHUNK, LANES), lambda b, sw, s: (kpos(sw, s), 0)),
                  pl.BlockSpec((CHUNK, LANES), lambda b, sw, s: (kpos(sw, s), 0)),
                  pl.BlockSpec((None, 2, LANES), lambda b, sw, s: (l, 0, 0))],
        out_specs=pl.BlockSpec((CHUNK, bw), lambda b, sw, s: (out_row(b, sw, s), 0)),
        out_shape=jax.ShapeDtypeStruct((t, bw), BF16),
        scratch_shapes=[pltpu.VMEM((nk, RET_QK, RET_V_HEAD), BF16),
                        pltpu.VMEM((RET_QK, RET_V_HEAD), F32),
                        pltpu.VMEM((RET_QK, RET_V_HEAD), F32)],
        compiler_params=_params(("parallel", "arbitrary", "arbitrary")),
        name="retention_scan",
    )(u, u, u, u, cos_t, sin_t, decay_logit)


def _rope_tables(ctx, seq):
    quarter = RET_K_HEAD // 4
    t = jnp.arange(seq)
    rowp = (t // GRID_W).astype(F32)
    colp = (t % GRID_W).astype(F32)
    inv = ROPE_BASE ** (-jnp.arange(quarter, dtype=F32) / quarter)
    ang_r = rowp[:, None] * inv[None, :]
    ang_c = colp[:, None] * inv[None, :]

    def blocks(ang):
        c, s = jnp.cos(ang), jnp.sin(ang)
        return jnp.concatenate([c, c], axis=-1), jnp.concatenate([-s, s], axis=-1)

    cr, sr = blocks(ang_r)
    cc, sc = blocks(ang_c)
    cos_h = jnp.concatenate([cr, cc], axis=-1)
    sin_h = jnp.concatenate([sr, sc], axis=-1)
    cos_l = jnp.concatenate([cos_h, cos_h], axis=-1)
    sin_l = jnp.concatenate([sin_h, sin_h], axis=-1)
    cos_t = jnp.concatenate([jnp.ones((ctx, LANES), F32), cos_l], axis=0)
    sin_t = jnp.concatenate([jnp.zeros((ctx, LANES), F32), sin_l], axis=0)
    return cos_t, sin_t


def _pad_lanes(a):
    return jnp.pad(a, [(0, 0)] * (a.ndim - 1) + [(0, LANES - a.shape[-1])])


def kernel(x, c, ctx, c_ctx, w_mod, b_mod, norm1_w, w_in, ssd_conv_w, ssd_conv_b, ssd_a_log,
           ssd_dt_bias, ssd_d, ssd_norm_w, pool_w, pool_scale, sconv_w, ret_decay_logit,
           w_branch, w_gate, b_gate, w_o, norm2_w, ffn_up, ffn_conv_w, ffn_conv_b, ffn_down,
           final_norm_w):
    nb, seq, d = x.shape
    nctx = ctx.shape[1]
    depth = w_mod.shape[0]
    assert d == D_MODEL and seq % GRID_W == 0
    assert nctx % POOL_ROWS == 0 and seq % POOL_ROWS == 0
    bc = nb * nctx
    t = bc + nb * seq
    tm = _pick((1024, 512, 256), bc, seq)
    rt = _pick((512, 256), bc, seq)

    def row_of(i):
        r = i * tm
        return jnp.where(r < bc, 0, 1 + (r - bc) // seq)

    sizes = (SSD_INNER, SSD_CONV_CH, SSD_HEADS, BRANCH_WIDTH, BRANCH_WIDTH, BRANCH_WIDTH,
             BRANCH_WIDTH, RET_QK, RET_QK, BRANCH_WIDTH, BRANCH_WIDTH)
    offs = [0]
    for sz in sizes:
        offs.append(offs[-1] + sz)
    parts = [w_in[:, :, offs[i]:offs[i + 1]] for i in range(len(sizes))]
    w_main = jnp.concatenate(parts[:2] + parts[3:], axis=-1).astype(BF16)
    w_dt = _pad_lanes(parts[2]).astype(BF16)
    assert w_main.shape[-1] == U_COLS
    wg_b = w_gate.astype(BF16)
    wb_b = w_branch.astype(BF16)
    wo_b = w_o.astype(BF16)
    up_b = ffn_up.astype(BF16)
    down_b = ffn_down.astype(BF16)
    poolw_b = pool_w.astype(BF16)

    nrows = 8 * ((1 + nb + 7) // 8)
    cvec = jnp.zeros((nrows, d), F32).at[0].set(c_ctx).at[1:1 + nb].set(c)
    mod = _mod_call(cvec, w_mod, b_mod)
    mod5 = mod.reshape(depth, nrows, 6, 1, d)

    cos_t, sin_t = _rope_tables(nctx, seq)
    dt_bias_p = _pad_lanes(ssd_dt_bias)
    a_log_p = _pad_lanes(ssd_a_log)
    decay_p = _pad_lanes(ret_decay_logit)
    d_exp = jnp.repeat(ssd_d, SSD_HEAD_DIM, axis=-1)[:, None, :]

    r3 = lambda a: a[:, None, :]
    xs = jnp.concatenate([ctx.reshape(bc, d), x.reshape(nb * seq, d)], axis=0)
    for l in range(depth):
        u, h, dt = _in_proj_call(xs, r3(norm1_w), mod5, w_main, w_dt, l, 0, 1, tm, row_of)
        xbc = _ssd_conv_call(u, ssd_conv_w, r3(ssd_conv_b), l, rt, bc, nctx, seq)
        y_ssd = _ssd_scan_call(xbc, dt, u, dt_bias_p, a_log_p, d_exp, r3(ssd_norm_w), l, nb, nctx, seq)
        y_pool = _pool_call(u, poolw_b, r3(pool_scale), l, bc, nctx, seq)
        y_sc = _sconv_call(u, sconv_w, l, rt, bc, nctx, seq)
        y_ret = _ret_scan_call(u, cos_t, sin_t, decay_p, l, nb, nctx, seq)
        merged = _merge_call(h, (y_ssd, y_pool, y_sc, y_ret), wg_b, b_gate[:, :, None, :], wb_b, l, tm)
        xs = _res_call(merged, wo_b, xs, mod5, l, 2, tm, row_of, "out_proj_residual")
        u2 = _up_proj_call(xs, r3(norm2_w), mod5, up_b, l, 3, 4, tm, row_of)
        gact = _ffn_conv_call(u2, ffn_conv_w, r3(ffn_conv_b), l, rt, bc, nctx, seq)
        xs = _res_call(gact, down_b, xs, mod5, l, 5, tm, row_of, "ffn_down_residual")
    out = _final_norm_call(xs, final_norm_w[None, :], bc, nb * seq, tm)
    return out.reshape(nb, seq, d)
```

```python
import functools
import math

import jax
import jax.numpy as jnp
from jax import lax
from jax.experimental import pallas as pl
from jax.experimental.pallas import tpu as pltpu

F32 = jnp.float32
BF16 = jnp.bfloat16

D_MODEL = 2048
GRID_W = 64
EPS = 1e-6
CHUNK = 128
BRANCH_WIDTH = D_MODEL // 2
SSD_INNER = BRANCH_WIDTH
SSD_HEAD_DIM = 64
SSD_HEADS = SSD_INNER // SSD_HEAD_DIM
SSD_GROUPS = 4
SSD_STATE = 128
SSD_CONV_CH = SSD_INNER + 2 * SSD_GROUPS * SSD_STATE
POOL_WINDOWS = (2, 4, 8, 16)
POOL_GROUP = BRANCH_WIDTH // len(POOL_WINDOWS)
RET_HEADS = 8
RET_V_HEAD = BRANCH_WIDTH // RET_HEADS
RET_K_HEAD = RET_V_HEAD // 2
RET_QK = RET_HEADS * RET_K_HEAD
ROPE_BASE = 10000.0
D_FF = 256 * ((8 * D_MODEL // 3 + 255) // 256)

LANES = 128
HALO = 16
VMEM_LIMIT = 56 * 1024 * 1024

U_Z = 0
U_XBC = 1024
U_POOL = 3072
U_SCB = 4096
U_SCC = 5120
U_SCX = 6144
U_RQ = 7168
U_RK = 7680
U_RV = 8192
U_RG = 9216
U_COLS = 10240


def _params(sem):
    return pltpu.CompilerParams(dimension_semantics=sem, vmem_limit_bytes=VMEM_LIMIT)


def _silu(v):
    return v * jax.nn.sigmoid(v)


def _pick(cands, *ns):
    for c in cands:
        if all(n % c == 0 for n in ns):
            return c
    raise ValueError(f"no tile in {cands} divides {ns}")


def _mod_const(v, n):
    if n & (n - 1) == 0:
        return v & (n - 1)
    return lax.rem(v, n)


def _seq_pos(r0, rows, bc, ctx, seq):
    in_ctx = r0 < bc
    g = r0 + lax.broadcasted_iota(jnp.int32, (rows, 1), 0)
    pos = jnp.where(in_ctx, _mod_const(g, ctx), _mod_const(g - bc, seq))
    n = jnp.where(in_ctx, ctx, seq)
    return pos, n


def _mod_kernel(c_ref, w_ref, b_ref, o_ref):
    a = _silu(c_ref[...]).astype(BF16)
    o_ref[...] = jnp.dot(a, w_ref[...].astype(BF16), preferred_element_type=F32) + b_ref[...]


def _mod_call(cvec, w_mod, b_mod):
    nl, d, n6 = w_mod.shape
    r = cvec.shape[0]
    tn = 1024
    return pl.pallas_call(
        _mod_kernel,
        grid=(nl, n6 // tn),
        in_specs=[pl.BlockSpec((r, d), lambda l, j: (0, 0)),
                  pl.BlockSpec((None, d, tn), lambda l, j: (l, 0, j)),
                  pl.BlockSpec((None, 1, tn), lambda l, j: (l, 0, j))],
        out_specs=pl.BlockSpec((None, r, tn), lambda l, j: (l, 0, j)),
        out_shape=jax.ShapeDtypeStruct((nl, r, n6), F32),
        compiler_params=_params(("parallel", "parallel")),
        name="mod_vectors",
    )(cvec, w_mod, b_mod.reshape(nl, 1, n6))


def _normed(x, nw, sh, sc):
    y = x * lax.rsqrt(jnp.mean(jnp.square(x), axis=-1, keepdims=True) + EPS)
    return (y * nw) * (1.0 + sc) + sh


def _in_proj_kernel(x_ref, nw_ref, sh_ref, sc_ref, w_ref, wdt_ref, u_ref, h_ref, dt_ref):
    @pl.when(pl.program_id(1) == 0)
    def _():
        h = _normed(x_ref[...], nw_ref[...], sh_ref[...], sc_ref[...]).astype(BF16)
        h_ref[...] = h
        dt_ref[...] = jnp.dot(h, wdt_ref[...], preferred_element_type=F32)

    u_ref[...] = jnp.dot(h_ref[...], w_ref[...], preferred_element_type=F32).astype(BF16)


def _mod_spec(l, k, tn, row_fn, col_fn):
    return pl.BlockSpec((None, None, None, 1, tn),
                        lambda *g: (l, row_fn(*g), k, 0, col_fn(*g)))


def _mod_row(tile, bc, seq):
    def row(i):
        r = i * tile
        return jnp.where(r < bc, 0, 1 + (r - bc) // seq)
    return row


def _in_proj_call(x, norm_w, mod5, w, wdt, l, k_sh, k_sc, tm, row_of):
    t, d = x.shape
    n = w.shape[-1]
    tn = 1024
    zero = lambda i, j: 0
    row_of = (lambda f: lambda i, j: f(i))(row_of)
    return pl.pallas_call(
        _in_proj_kernel,
        grid=(t // tm, n // tn),
        in_specs=[pl.BlockSpec((tm, d), lambda i, j: (i, 0)),
                  pl.BlockSpec((None, 1, d), lambda i, j: (l, 0, 0)),
                  _mod_spec(l, k_sh, d, row_of, zero),
                  _mod_spec(l, k_sc, d, row_of, zero),
                  pl.BlockSpec((None, d, tn), lambda i, j: (l, 0, j)),
                  pl.BlockSpec((None, d, LANES), lambda i, j: (l, 0, 0))],
        out_specs=[pl.BlockSpec((tm, tn), lambda i, j: (i, j)),
                   pl.BlockSpec((tm, d), lambda i, j: (i, 0)),
                   pl.BlockSpec((tm, LANES), lambda i, j: (i, 0))],
        out_shape=[jax.ShapeDtypeStruct((t, n), BF16),
                   jax.ShapeDtypeStruct((t, d), BF16),
                   jax.ShapeDtypeStruct((t, LANES), F32)],
        compiler_params=_params(("parallel", "arbitrary")),
        name="in_proj",
    )(x, norm_w, mod5, mod5, w, wdt)


def _merge_kernel(h_ref, y0_ref, y1_ref, y2_ref, y3_ref, wg_ref, bg_ref, wb_ref, o_ref):
    h = h_ref[...]
    acc = None
    for i, y_ref in enumerate((y0_ref, y1_ref, y2_ref, y3_ref)):
        gate = jax.nn.sigmoid(jnp.dot(h, wg_ref[i], preferred_element_type=F32) + bg_ref[i])
        term = gate * jnp.dot(y_ref[...], wb_ref[i], preferred_element_type=F32)
        acc = term if acc is None else acc + term
    o_ref[...] = acc.astype(BF16)


def _merge_call(h, ys, wg, bg, wb, l, tm):
    t, d = h.shape
    bw = ys[0].shape[-1]
    tn = 256
    nb = len(ys)
    return pl.pallas_call(
        _merge_kernel,
        grid=(t // tm, d // tn),
        in_specs=[pl.BlockSpec((tm, d), lambda i, j: (i, 0))]
                 + [pl.BlockSpec((tm, bw), lambda i, j: (i, 0)) for _ in ys]
                 + [pl.BlockSpec((None, nb, d, tn), lambda i, j: (l, 0, 0, j)),
                    pl.BlockSpec((None, nb, 1, tn), lambda i, j: (l, 0, 0, j)),
                    pl.BlockSpec((None, nb, bw, tn), lambda i, j: (l, 0, 0, j))],
        out_specs=pl.BlockSpec((tm, tn), lambda i, j: (i, j)),
        out_shape=jax.ShapeDtypeStruct((t, d), BF16),
        compiler_params=_params(("parallel", "arbitrary")),
        name="branch_merge",
    )(h, *ys, wg, bg, wb)


def _res_kernel(a_ref, w_ref, x_ref, g_ref, o_ref):
    o_ref[...] = x_ref[...] + g_ref[...] * jnp.dot(a_ref[...], w_ref[...],
                                                   preferred_element_type=F32)


def _res_call(a, w, x, mod5, l, k_gate, tm, row_of, name):
    t, kdim = a.shape
    d = x.shape[-1]
    tn = 512
    return pl.pallas_call(
        _res_kernel,
        grid=(t // tm, d // tn),
        in_specs=[pl.BlockSpec((tm, kdim), lambda i, j: (i, 0)),
                  pl.BlockSpec((None, kdim, tn), lambda i, j: (l, 0, j)),
                  pl.BlockSpec((tm, tn), lambda i, j: (i, j)),
                  _mod_spec(l, k_gate, tn, lambda i, j: row_of(i), lambda i, j: j)],
        out_specs=pl.BlockSpec((tm, tn), lambda i, j: (i, j)),
        out_shape=jax.ShapeDtypeStruct((t, d), F32),
        compiler_params=_params(("parallel", "arbitrary")),
        name=name,
    )(a, w, x, mod5)


EPI_ROWS = 128


def _outproj_norm_kernel(a_ref, w_ref, x_ref, g_ref, nw_ref, sh_ref, sc_ref, xo_ref, ho_ref, y0, y1):
    s = pl.program_id(0)
    tm = a_ref.shape[0]

    @pl.when(s == 0)
    def _():
        y1[...] = jnp.zeros_like(y1)

    def step(y_cur, y_prev):
        for r in range(0, tm, EPI_ROWS):
            rs = slice(r, r + EPI_ROWS)
            y_cur[rs, :] = jnp.dot(a_ref[rs, :], w_ref[...], preferred_element_type=F32)
            xn = x_ref[rs, :] + g_ref[...] * y_prev[rs, :]
            xo_ref[rs, :] = xn
            ho_ref[rs, :] = _normed(xn, nw_ref[...], sh_ref[...], sc_ref[...]).astype(BF16)

    @pl.when(s % 2 == 0)
    def _():
        step(y0, y1)

    @pl.when(s % 2 == 1)
    def _():
        step(y1, y0)


def _outproj_norm_call(a, w, x, mod5, norm_w, l, k_gate, k_sh, k_sc, tm, bc, seq):
    t, kdim = a.shape
    d = x.shape[-1]
    ni = t // tm
    row = _mod_row(tm, bc, seq)
    cur = lambda s: jnp.minimum(s, ni - 1)
    prev = lambda s: jnp.maximum(s - 1, 0)
    zero = lambda s: 0
    return pl.pallas_call(
        _outproj_norm_kernel,
        grid=(ni + 1,),
        in_specs=[pl.BlockSpec((tm, kdim), lambda s: (cur(s), 0)),
                  pl.BlockSpec((None, kdim, d), lambda s: (l, 0, 0), pipeline_mode=pl.Buffered(1)),
                  pl.BlockSpec((tm, d), lambda s: (prev(s), 0)),
                  _mod_spec(l, k_gate, d, lambda s: row(prev(s)), zero),
                  pl.BlockSpec((None, 1, d), lambda s: (l, 0, 0)),
                  _mod_spec(l, k_sh, d, lambda s: row(prev(s)), zero),
                  _mod_spec(l, k_sc, d, lambda s: row(prev(s)), zero)],
        out_specs=[pl.BlockSpec((tm, d), lambda s: (prev(s), 0)),
                   pl.BlockSpec((tm, d), lambda s: (prev(s), 0))],
        out_shape=[jax.ShapeDtypeStruct((t, d), F32),
                   jax.ShapeDtypeStruct((t, d), BF16)],
        scratch_shapes=[pltpu.VMEM((tm, d), F32), pltpu.VMEM((tm, d), F32)],
        compiler_params=_params(("arbitrary",)),
        name="out_proj_residual_norm",
    )(a, w, x, mod5, norm_w, mod5, mod5)


FFN_PIECE_ROWS = 64
FFN_SLABS = 4
SEQ_ALIGN = 256


def _ffn_gate_kernel(h_ref, hp_ref, hn_ref, wa_ref, wb_ref, cwa_ref, cwb_ref, ba_ref, bb_ref, o_ref,
                     h_s, a0, b0, a1, b1, *, nj, bc, ctx, seq):
    s = pl.program_id(0)
    tm = h_ref.shape[0]

    @pl.when(s == 0)
    def _():
        a1[...] = jnp.zeros_like(a1)
        b1[...] = jnp.zeros_like(b1)

    @pl.when(s % nj == 0)
    def _():
        h_s[0:HALO, :] = hp_ref[...]
        h_s[HALO:HALO + tm, :] = h_ref[...]
        h_s[HALO + tm:, :] = hn_ref[...]

    r0 = (jnp.maximum(s - 1, 0) // nj) * tm
    in_ctx = r0 < bc
    tn = o_ref.shape[1]
    rows_ext = tm + 2 * HALO
    pr = FFN_PIECE_ROWS
    row_first = lax.broadcasted_iota(jnp.int32, (pr, 1), 0) == 0
    row_last = lax.broadcasted_iota(jnp.int32, (pr, 1), 0) == pr - 1

    def seq_start(g):
        return jnp.where(in_ctx, _mod_const(g, ctx), _mod_const(g - bc, seq)) == 0

    def conv_piece(src, r, cs, w_ref, bias_ref, kill_prev, kill_next):
        base = HALO + r
        xm1 = src[base - 1:base - 1 + pr, cs]
        x00 = src[base:base + pr, cs]
        xp1 = src[base + 1:base + 1 + pr, cs]
        if kill_prev is not None:
            xm1 = jnp.where(kill_prev, 0.0, xm1)
        if kill_next is not None:
            xp1 = jnp.where(kill_next, 0.0, xp1)
        return xm1 * w_ref[0:1, cs] + x00 * w_ref[1:2, cs] + xp1 * w_ref[2:3, cs] + bias_ref[:, cs]

    def epilogue(a_prev, b_prev, lo, hi):
        for r in range(lo, hi, pr):
            kill_prev = kill_next = None
            if r % SEQ_ALIGN == 0:
                kill_prev = jnp.logical_and(row_first, seq_start(r0 + r))
            if (r + pr) % SEQ_ALIGN == 0:
                kill_next = jnp.logical_and(row_last, seq_start(r0 + r + pr))
            for c in range(0, tn, LANES):
                cs = slice(c, c + LANES)
                a = conv_piece(a_prev, r, cs, cwa_ref, ba_ref, kill_prev, kill_next)
                b = conv_piece(b_prev, r, cs, cwb_ref, bb_ref, kill_prev, kill_next)
                o_ref[r:r + pr, cs] = (_silu(a) * b).astype(BF16)

    def step(a_cur, b_cur, a_prev, b_prev):
        nslab = FFN_SLABS
        for k in range(nslab):
            m_lo = (k * rows_ext // nslab) // HALO * HALO
            m_hi = rows_ext if k == nslab - 1 else ((k + 1) * rows_ext // nslab) // HALO * HALO
            h = h_s[m_lo:m_hi, :]
            a_cur[m_lo:m_hi, :] = jnp.dot(h, wa_ref[...], preferred_element_type=F32)
            b_cur[m_lo:m_hi, :] = jnp.dot(h, wb_ref[...], preferred_element_type=F32)
            epilogue(a_prev, b_prev, k * tm // nslab, (k + 1) * tm // nslab)

    @pl.when(s % 2 == 0)
    def _():
        step(a0, b0, a1, b1)

    @pl.when(s % 2 == 1)
    def _():
        step(a1, b1, a0, b0)


def _ffn_gate_call(h, w, cw, cb, l, tm, bc, ctx, seq):
    t, d = h.shape
    tn = 512
    nj = D_FF // tn
    ni = t // tm
    per = tm // HALO
    last = t // HALO - 1
    ci = lambda s: jnp.minimum(s // nj, ni - 1)
    cj = lambda s: s % nj
    pi = lambda s: jnp.maximum(s - 1, 0) // nj
    pj = lambda s: jnp.maximum(s - 1, 0) % nj
    return pl.pallas_call(
        functools.partial(_ffn_gate_kernel, nj=nj, bc=bc, ctx=ctx, seq=seq),
        grid=(ni * nj + 1,),
        in_specs=[pl.BlockSpec((tm, d), lambda s: (ci(s), 0)),
                  pl.BlockSpec((HALO, d), lambda s: (jnp.maximum(ci(s) * per - 1, 0), 0)),
                  pl.BlockSpec((HALO, d), lambda s: (jnp.minimum((ci(s) + 1) * per, last), 0)),
                  pl.BlockSpec((None, d, tn), lambda s: (l, 0, cj(s))),
                  pl.BlockSpec((None, d, tn), lambda s: (l, 0, nj + cj(s))),
                  pl.BlockSpec((None, 3, tn), lambda s: (l, 0, pj(s))),
                  pl.BlockSpec((None, 3, tn), lambda s: (l, 0, nj + pj(s))),
                  pl.BlockSpec((None, 1, tn), lambda s: (l, 0, pj(s))),
                  pl.BlockSpec((None, 1, tn), lambda s: (l, 0, nj + pj(s)))],
        out_specs=pl.BlockSpec((tm, tn), lambda s: (pi(s), pj(s))),
        out_shape=jax.ShapeDtypeStruct((t, D_FF), BF16),
        scratch_shapes=[pltpu.VMEM((tm + 2 * HALO, d), BF16)]
                       + [pltpu.VMEM((tm + 2 * HALO, tn), F32) for _ in range(4)],
        compiler_params=_params(("arbitrary",)),
        name="ffn_up_conv_gate",
    )(h, h, h, w, w, cw, cw, cb, cb)


def _final_norm_kernel(x_ref, w_ref, o_ref):
    x = x_ref[...]
    o_ref[...] = (x * lax.rsqrt(jnp.mean(jnp.square(x), axis=-1, keepdims=True) + EPS)) * w_ref[...]


def _final_norm_call(x, w, row0, rows, tm):
    d = x.shape[-1]
    off = row0 // tm
    return pl.pallas_call(
        _final_norm_kernel,
        grid=(rows // tm,),
        in_specs=[pl.BlockSpec((tm, d), lambda i: (i + off, 0)),
                  pl.BlockSpec((1, d), lambda i: (0, 0))],
        out_specs=pl.BlockSpec((tm, d), lambda i: (i, 0)),
        out_shape=jax.ShapeDtypeStruct((rows, d), F32),
        compiler_params=_params(("parallel",)),
        name="final_norm",
    )(x, w)


def _halo_specs(rt, cols, col_idx, t):
    per = rt // HALO
    last = t // HALO - 1
    return [pl.BlockSpec((rt, cols), lambda i, j: (i, col_idx(j))),
            pl.BlockSpec((HALO, cols), lambda i, j: (jnp.maximum(i * per - 1, 0), col_idx(j))),
            pl.BlockSpec((HALO, cols), lambda i, j: (jnp.minimum((i + 1) * per, last), col_idx(j)))]


def _shift_pm1(x, prev_row, next_row, pos, n):
    rt = x.shape[0]
    row = lax.broadcasted_iota(jnp.int32, (rt, 1), 0)
    xm1 = jnp.where(row == 0, prev_row, pltpu.roll(x, 1, 0))
    xm1 = jnp.where(pos == 0, 0.0, xm1)
    xp1 = jnp.where(row == rt - 1, next_row, pltpu.roll(x, rt - 1, 0))
    xp1 = jnp.where(pos == n - 1, 0.0, xp1)
    return xm1, xp1


def _conv3(x, prev_row, next_row, w, pos, n):
    xm1, xp1 = _shift_pm1(x, prev_row, next_row, pos, n)
    return xm1 * w[0:1] + x * w[1:2] + xp1 * w[2:3]


def _ssd_conv_kernel(x_ref, p_ref, n_ref, w_ref, b_ref, o_ref, *, bc, ctx, seq):
    rt = x_ref.shape[0]
    pos, n = _seq_pos(pl.program_id(0) * rt, rt, bc, ctx, seq)
    x = x_ref[...].astype(F32)
    prev_row = p_ref[...].astype(F32)[HALO - 1:HALO]
    next_row = n_ref[...].astype(F32)[0:1]
    y = _conv3(x, prev_row, next_row, w_ref[...], pos, n) + b_ref[...]
    o_ref[...] = _silu(y).astype(BF16)


def _ssd_conv_call(u, w, b, l, rt, bc, ctx, seq):
    t = u.shape[0]
    cols = 1024
    nj = SSD_CONV_CH // cols
    base = U_XBC // cols
    return pl.pallas_call(
        functools.partial(_ssd_conv_kernel, bc=bc, ctx=ctx, seq=seq),
        grid=(t // rt, nj),
        in_specs=_halo_specs(rt, cols, lambda j: base + j, t)
                 + [pl.BlockSpec((None, 3, cols), lambda i, j: (l, 0, j)),
                    pl.BlockSpec((None, 1, cols), lambda i, j: (l, 0, j))],
        out_specs=pl.BlockSpec((rt, cols), lambda i, j: (i, j)),
        out_shape=jax.ShapeDtypeStruct((t, SSD_CONV_CH), BF16),
        compiler_params=_params(("parallel", "parallel")),
        name="ssd_conv",
    )(u, u, u, w, b)


def _sconv_kernel(b_ref, c_ref, cp_ref, cn_ref, x_ref, xp_ref, xn_ref, w_ref, o_ref, *, bc, ctx, seq):
    rt = b_ref.shape[0]
    pos, n = _seq_pos(pl.program_id(0) * rt, rt, bc, ctx, seq)
    cx = c_ref[...].astype(F32) * x_ref[...].astype(F32)
    prev_row = (cp_ref[...].astype(F32) * xp_ref[...].astype(F32))[HALO - 1:HALO]
    next_row = (cn_ref[...].astype(F32) * xn_ref[...].astype(F32))[0:1]
    y = b_ref[...].astype(F32) * _conv3(cx, prev_row, next_row, w_ref[...], pos, n)
    o_ref[...] = y.astype(BF16)


def _sconv_call(u, w, l, rt, bc, ctx, seq):
    t = u.shape[0]
    cols = BRANCH_WIDTH
    return pl.pallas_call(
        functools.partial(_sconv_kernel, bc=bc, ctx=ctx, seq=seq),
        grid=(t // rt, 1),
        in_specs=[pl.BlockSpec((rt, cols), lambda i, j: (i, U_SCB // cols))]
                 + _halo_specs(rt, cols, lambda j: U_SCC // cols, t)
                 + _halo_specs(rt, cols, lambda j: U_SCX // cols, t)
                 + [pl.BlockSpec((None, 3, cols), lambda i, j: (l, 0, 0))],
        out_specs=pl.BlockSpec((rt, cols), lambda i, j: (i, 0)),
        out_shape=jax.ShapeDtypeStruct((t, cols), BF16),
        compiler_params=_params(("parallel", "parallel")),
        name="short_conv",
    )(u, u, u, u, u, u, u, w)


POOL_ROWS = 256


def _pool_kernel(x_ref, p_ref, n_ref, w_ref, s_ref, o_ref, *, bc, ctx, seq):
    rt = POOL_ROWS
    r0 = pl.program_id(0) * rt
    pos, n = _seq_pos(r0, rt, bc, ctx, seq)
    in_ctx = r0 < bc
    p0 = jnp.where(in_ctx, _mod_const(r0, ctx), _mod_const(r0 - bc, seq))
    has_prev = p0 != 0
    has_next = p0 + rt != n
    ext_rows = rt + 2 * HALO
    for gi, win in enumerate(POOL_WINDOWS):
        cs = slice(gi * POOL_GROUP, (gi + 1) * POOL_GROUP)
        g = x_ref[:, cs].astype(F32)
        prev = jnp.where(has_prev, p_ref[:, cs].astype(F32), 0.0)
        nxt = jnp.where(has_next, n_ref[:, cs].astype(F32), 0.0)
        ext = jnp.concatenate([prev, g, nxt], axis=0)
        s = ext + pltpu.roll(ext, 1, 0)
        half = 1
        while 2 * half < win:
            s = pltpu.roll(s, half, 0) + pltpu.roll(s, ext_rows - half, 0)
            half *= 2
        s = s[HALO:HALO + rt]
        lo = jnp.clip(pos - win // 2, 0, n)
        hi = jnp.clip(pos - win // 2 + win, 0, n)
        mean = s / (hi - lo).astype(F32)
        pooled = (mean - g).astype(BF16)
        y = jnp.dot(pooled, w_ref[gi], preferred_element_type=F32) * s_ref[:, cs]
        o_ref[:, cs] = y.astype(BF16)


def _pool_call(u, w, scale, l, bc, ctx, seq):
    t = u.shape[0]
    cols = BRANCH_WIDTH
    rt = POOL_ROWS
    ng = len(POOL_WINDOWS)
    return pl.pallas_call(
        functools.partial(_pool_kernel, bc=bc, ctx=ctx, seq=seq),
        grid=(t // rt, 1),
        in_specs=_halo_specs(rt, cols, lambda j: U_POOL // cols, t)
                 + [pl.BlockSpec((None, ng, POOL_GROUP, POOL_GROUP), lambda i, j: (l, 0, 0, 0)),
                    pl.BlockSpec((None, 1, cols), lambda i, j: (l, 0, 0))],
        out_specs=pl.BlockSpec((rt, cols), lambda i, j: (i, 0)),
        out_shape=jax.ShapeDtypeStruct((t, cols), BF16),
        compiler_params=_params(("parallel", "parallel")),
        name="multiscale_pool",
    )(u, u, u, w, scale)


def _scan_positions(ncc, nk):
    def kpos(sw, s):
        back = jnp.where(s < ncc, ncc - 1 - s, nk - 1 - (s - ncc))
        return jnp.where(sw == 0, back, s)
    return kpos


def _chunk_row(b, k, ncc, ncl, nb):
    return jnp.where(k < ncc, b * ncc + k, nb * ncc + b * ncl + (k - ncc))


def _tri_masks():
    row = lax.broadcasted_iota(jnp.int32, (CHUNK, CHUNK), 0)
    col = lax.broadcasted_iota(jnp.int32, (CHUNK, CHUNK), 1)
    return row, col, col <= row, col >= row


def _colb(x, h):
    return jnp.broadcast_to(x[:, h:h + 1], (x.shape[0], LANES))


def _ssd_scan_kernel(xbc_ref, dt_ref, z_ref, dtb_ref, alog_ref, dexp_ref, nw_ref, o_ref,
                     hb_all, hf_s, hb_s, y_s, *, ncc, nk):
    sw = pl.program_id(1)
    s = pl.program_id(2)
    kpos = _scan_positions(ncc, nk)(sw, s)
    row, col, lower, upper = _tri_masks()
    lane_lo = col < SSD_HEAD_DIM
    row_lo1 = lane_lo[0:1]

    dt_raw = dt_ref[...]
    a = -jnp.exp(alog_ref[...])
    dtf = jax.nn.softplus(dt_raw + dtb_ref[0:1])
    dtb = jax.nn.softplus(dt_raw + dtb_ref[1:2])
    laf = dtf * a[0:1]
    lab = dtb * a[1:2]
    hp = lax.Precision.HIGHEST
    cf = jnp.dot(lower.astype(F32), laf, precision=hp, preferred_element_type=F32)
    rb = jnp.dot(upper.astype(F32), lab, precision=hp, preferred_element_type=F32)

    xs = xbc_ref[:, 0:SSD_INNER]
    xs32 = xs.astype(F32)
    n_pairs = SSD_HEADS // 2
    gw = SSD_INNER // SSD_GROUPS

    def pair_cols(x, p):
        return jnp.where(lane_lo, _colb(x, 2 * p), _colb(x, 2 * p + 1))

    def pair_row(x_row, p):
        return jnp.where(row_lo1, jnp.broadcast_to(x_row[:, 2 * p:2 * p + 1], (1, LANES)),
                         jnp.broadcast_to(x_row[:, 2 * p + 1:2 * p + 2], (1, LANES)))

    def state_update(h_s, w_tok, dec_row):
        vw = jnp.concatenate([xs32[:, p * LANES:(p + 1) * LANES] * pair_cols(w_tok, p)
                              for p in range(n_pairs)], axis=1).astype(BF16)
        dec = jnp.concatenate([pair_row(dec_row, p) for p in range(n_pairs)], axis=1)
        for g in range(SSD_GROUPS):
            bm = xbc_ref[:, SSD_INNER + g * SSD_STATE:SSD_INNER + (g + 1) * SSD_STATE]
            st = lax.dot_general(bm, vw[:, g * gw:(g + 1) * gw], (((0,), (0,)), ((), ())),
                                 preferred_element_type=F32)
            h_s[:, g * gw:(g + 1) * gw] = h_s[:, g * gw:(g + 1) * gw] * dec[:, g * gw:(g + 1) * gw] + st

    @pl.when(sw == 0)
    def _():
        @pl.when(s == 0)
        def _():
            hb_s[...] = jnp.zeros_like(hb_s)

        hb_all[kpos] = hb_s[...].astype(BF16)
        total = rb[0:1]
        state_update(hb_s, jnp.exp(total - rb) * dtb, jnp.exp(total))

    @pl.when(sw == 1)
    def _():
        @pl.when(s == 0)
        def _():
            hf_s[...] = jnp.zeros_like(hf_s)

        cft, rbt, dtft, dtbt = cf.T, rb.T, dtf.T, dtb.T
        ecf = jnp.exp(cf)
        erb = jnp.exp(rb)
        hb_in = hb_all[kpos]
        heads_per_group = SSD_HEADS // SSD_GROUPS
        for g in range(SSD_GROUPS):
            bm = xbc_ref[:, SSD_INNER + g * SSD_STATE:SSD_INNER + (g + 1) * SSD_STATE]
            cm = xbc_ref[:, SSD_INNER + SSD_GROUPS * SSD_STATE + g * SSD_STATE:
                         SSD_INNER + SSD_GROUPS * SSD_STATE + (g + 1) * SSD_STATE]
            sc = lax.dot_general(cm, bm, (((1,), (1,)), ((), ())), preferred_element_type=F32)
            gf = jnp.dot(cm, hf_s[:, g * gw:(g + 1) * gw].astype(BF16), preferred_element_type=F32)
            gb = jnp.dot(cm, hb_in[:, g * gw:(g + 1) * gw], preferred_element_type=F32)
            for pp in range(heads_per_group // 2):
                p = g * (heads_per_group // 2) + pp
                xs_pair = xs[:, p * LANES:(p + 1) * LANES]
                res = []
                for h in (2 * p, 2 * p + 1):
                    seg_f = _colb(cf, h) - cft[h:h + 1]
                    seg_b = _colb(rb, h) - rbt[h:h + 1]
                    dec_f = jnp.where(lower, jnp.exp(jnp.where(lower, seg_f, 0.0)), 0.0)
                    dec_b = jnp.where(upper, jnp.exp(jnp.where(upper, seg_b, 0.0)), 0.0)
                    m = sc * (dec_f * dtft[h:h + 1] + dec_b * dtbt[h:h + 1])
                    res.append(jnp.dot(m.astype(BF16), xs_pair, preferred_element_type=F32))
                y_intra = jnp.where(lane_lo, res[0], res[1])
                lo = pp * LANES
                y_inter = gf[:, lo:lo + LANES] * pair_cols(ecf, p) + gb[:, lo:lo + LANES] * pair_cols(erb, p)
                y_s[:, p * LANES:(p + 1) * LANES] = y_intra + y_inter

        last = cf[CHUNK - 1:CHUNK]
        state_update(hf_s, jnp.exp(last - cf) * dtf, jnp.exp(last))

        y = (y_s[...] + xs32 * dexp_ref[...]) * _silu(z_ref[...].astype(F32))
        y = y * lax.rsqrt(jnp.mean(jnp.square(y), axis=-1, keepdims=True) + EPS)
        o_ref[...] = (y * nw_ref[...]).astype(BF16)


def _ssd_scan_call(xbc, dt, u, dt_bias, a_log, d_exp, norm_w, l, nb, ctx, seq):
    t = xbc.shape[0]
    ncc, ncl = ctx // CHUNK, seq // CHUNK
    nk = ncc + ncl
    kpos = _scan_positions(ncc, nk)

    def in_row(b, sw, s):
        return _chunk_row(b, kpos(sw, s), ncc, ncl, nb)

    def out_row(b, sw, s):
        return _chunk_row(b, jnp.where(sw == 0, 0, s), ncc, ncl, nb)

    return pl.pallas_call(
        functools.partial(_ssd_scan_kernel, ncc=ncc, nk=nk),
        grid=(nb, 2, nk),
        in_specs=[pl.BlockSpec((CHUNK, SSD_CONV_CH), lambda b, sw, s: (in_row(b, sw, s), 0)),
                  pl.BlockSpec((CHUNK, LANES), lambda b, sw, s: (in_row(b, sw, s), 0)),
                  pl.BlockSpec((CHUNK, SSD_INNER), lambda b, sw, s: (in_row(b, sw, s), U_Z // SSD_INNER)),
                  pl.BlockSpec((None, 2, LANES), lambda b, sw, s: (l, 0, 0)),
                  pl.BlockSpec((None, 2, LANES), lambda b, sw, s: (l, 0, 0)),
                  pl.BlockSpec((None, 1, SSD_INNER), lambda b, sw, s: (l, 0, 0)),
                  pl.BlockSpec((None, 1, SSD_INNER), lambda b, sw, s: (l, 0, 0))],
        out_specs=pl.BlockSpec((CHUNK, SSD_INNER), lambda b, sw, s: (out_row(b, sw, s), 0)),
        out_shape=jax.ShapeDtypeStruct((t, SSD_INNER), BF16),
        scratch_shapes=[pltpu.VMEM((nk, SSD_STATE, SSD_INNER), BF16),
                        pltpu.VMEM((SSD_STATE, SSD_INNER), F32),
                        pltpu.VMEM((SSD_STATE, SSD_INNER), F32),
                        pltpu.VMEM((CHUNK, SSD_INNER), F32)],
        compiler_params=_params(("parallel", "arbitrary", "arbitrary")),
        name="ssd_scan",
    )(xbc, dt, u, dt_bias, a_log, d_exp, norm_w)


def _ret_scan_kernel(q_ref, k_ref, v_ref, g_ref, cos_ref, sin_ref, dl_ref, o_ref,
                     hb_all, hf_s, hb_s, *, ncc, nk):
    sw = pl.program_id(1)
    s = pl.program_id(2)
    kpos = _scan_positions(ncc, nk)(sw, s)
    row, col, lower, upper = _tri_masks()
    lane_lo = col < RET_K_HEAD
    la = jax.nn.log_sigmoid(dl_ref[...])
    rowf = row.astype(F32)
    diff = (row - col).astype(F32)
    n_pairs = RET_HEADS // 2

    cosv = cos_ref[...]
    sinv = sin_ref[...]
    first16 = (col & 31) < 16

    def rope(x):
        swapped = jnp.where(first16, pltpu.roll(x, LANES - 16, 1), pltpu.roll(x, 16, 1))
        return x * cosv + swapped * sinv

    def head_scalar(d, h):
        return la[d:d + 1, h:h + 1]

    def pair_k(p):
        return rope(k_ref[:, p * LANES:(p + 1) * LANES].astype(F32))

    def state_update(h_s, d, w_exponent):
        for p in range(n_pairs):
            kp = pair_k(p)
            acc = None
            for h in (2 * p, 2 * p + 1):
                mask = lane_lo if h % 2 == 0 else jnp.logical_not(lane_lo)
                km = jnp.where(mask, kp, 0.0).astype(BF16)
                w = jnp.exp(w_exponent * head_scalar(d, h))
                vw = (v_ref[:, h * RET_V_HEAD:(h + 1) * RET_V_HEAD].astype(F32) * w).astype(BF16)
                st = lax.dot_general(km, vw, (((0,), (0,)), ((), ())), preferred_element_type=F32)
                acc = st if acc is None else acc + st
            dec = jnp.where(row < RET_K_HEAD, jnp.exp(float(CHUNK) * head_scalar(d, 2 * p)),
                            jnp.exp(float(CHUNK) * head_scalar(d, 2 * p + 1)))
            rs = slice(p * LANES, (p + 1) * LANES)
            h_s[rs, :] = h_s[rs, :] * dec + acc

    @pl.when(sw == 0)
    def _():
        @pl.when(s == 0)
        def _():
            hb_s[...] = jnp.zeros_like(hb_s)

        hb_all[kpos] = hb_s[...].astype(BF16)
        state_update(hb_s, 1, rowf)

    @pl.when(sw == 1)
    def _():
        @pl.when(s == 0)
        def _():
            hf_s[...] = jnp.zeros_like(hf_s)

        hb_in = hb_all[kpos]
        for p in range(n_pairs):
            qp = rope(q_ref[:, p * LANES:(p + 1) * LANES].astype(F32) * (RET_K_HEAD ** -0.5))
            kp = pair_k(p).astype(BF16)
            rs = slice(p * LANES, (p + 1) * LANES)
            hf_pair = hf_s[rs, :].astype(BF16)
            hb_pair = hb_in[rs, :]
            for h in (2 * p, 2 * p + 1):
                mask = lane_lo if h % 2 == 0 else jnp.logical_not(lane_lo)
                qm = jnp.where(mask, qp, 0.0).astype(BF16)
                sc = lax.dot_general(qm, kp, (((1,), (1,)), ((), ())), preferred_element_type=F32)
                laf = head_scalar(0, h)
                lab = head_scalar(1, h)
                dec_f = jnp.where(lower, jnp.exp(jnp.where(lower, diff, 0.0) * laf), 0.0)
                dec_b = jnp.where(upper, jnp.exp(jnp.where(upper, -diff, 0.0) * lab), 0.0)
                m = (sc * (dec_f + dec_b)).astype(BF16)
                vs = slice(h * RET_V_HEAD, (h + 1) * RET_V_HEAD)
                y = jnp.dot(m, v_ref[:, vs], preferred_element_type=F32)
                y = y + jnp.dot(qm, hf_pair, preferred_element_type=F32) * jnp.exp((rowf + 1.0) * laf)
                y = y + jnp.dot(qm, hb_pair, preferred_element_type=F32) * jnp.exp((float(CHUNK) - rowf) * lab)
                mu = jnp.mean(y, axis=-1, keepdims=True)
                yc = y - mu
                var = jnp.mean(jnp.square(yc), axis=-1, keepdims=True)
                yn = yc * lax.rsqrt(var + EPS)
                o_ref[:, vs] = (_silu(g_ref[:, vs].astype(F32)) * yn).astype(BF16)

        state_update(hf_s, 0, float(CHUNK - 1) - rowf)


def _ret_scan_call(u, cos_t, sin_t, decay_logit, l, nb, ctx, seq):
    t = u.shape[0]
    ncc, ncl = ctx // CHUNK, seq // CHUNK
    nk = ncc + ncl
    kpos = _scan_positions(ncc, nk)

    def in_row(b, sw, s):
        return _chunk_row(b, kpos(sw, s), ncc, ncl, nb)

    def out_row(b, sw, s):
        return _chunk_row(b, jnp.where(sw == 0, 0, s), ncc, ncl, nb)

    bw = BRANCH_WIDTH
    return pl.pallas_call(
        functools.partial(_ret_scan_kernel, ncc=ncc, nk=nk),
        grid=(nb, 2, nk),
        in_specs=[pl.BlockSpec((CHUNK, RET_QK), lambda b, sw, s: (in_row(b, sw, s), U_RQ // RET_QK)),
                  pl.BlockSpec((CHUNK, RET_QK), lambda b, sw, s: (in_row(b, sw, s), U_RK // RET_QK)),
                  pl.BlockSpec((CHUNK, bw), lambda b, sw, s: (in_row(b, sw, s), U_RV // bw)),
                  pl.BlockSpec((CHUNK, bw), lambda b, sw, s: (in_row(b, sw, s), U_RG // bw)),
                  pl.BlockSpec((CHUNK, LANES), lambda b, sw, s: (kpos(sw, s), 0)),
                  pl.BlockSpec((CHUNK, LANES), lambda b, sw, s: (kpos(sw, s), 0)),
                  pl.BlockSpec((None, 2, LANES), lambda b, sw, s: (l, 0, 0))],
        out_specs=pl.BlockSpec((CHUNK, bw), lambda b, sw, s: (out_row(b, sw, s), 0)),
        out_shape=jax.ShapeDtypeStruct((t, bw), BF16),
        scratch_shapes=[pltpu.VMEM((nk, RET_QK, RET_V_HEAD), BF16),
                        pltpu.VMEM((RET_QK, RET_V_HEAD), F32),
                        pltpu.VMEM((RET_QK, RET_V_HEAD), F32)],
        compiler_params=_params(("parallel", "arbitrary", "arbitrary")),
        name="retention_scan",
    )(u, u, u, u, cos_t, sin_t, decay_logit)


def _rope_tables(ctx, seq):
    quarter = RET_K_HEAD // 4
    t = jnp.arange(seq)
    rowp = (t // GRID_W).astype(F32)
    colp = (t % GRID_W).astype(F32)
    inv = ROPE_BASE ** (-jnp.arange(quarter, dtype=F32) / quarter)
    ang_r = rowp[:, None] * inv[None, :]
    ang_c = colp[:, None] * inv[None, :]

    def blocks(ang):
        c, s = jnp.cos(ang), jnp.sin(ang)
        return jnp.concatenate([c, c], axis=-1), jnp.concatenate([-s, s], axis=-1)

    cr, sr = blocks(ang_r)
    cc, sc = blocks(ang_c)
    cos_h = jnp.concatenate([cr, cc], axis=-1)
    sin_h = jnp.concatenate([sr, sc], axis=-1)
    cos_l = jnp.concatenate([cos_h, cos_h], axis=-1)
    sin_l = jnp.concatenate([sin_h, sin_h], axis=-1)
    cos_t = jnp.concatenate([jnp.ones((ctx, LANES), F32), cos_l], axis=0)
    sin_t = jnp.concatenate([jnp.zeros((ctx, LANES), F32), sin_l], axis=0)
    return cos_t, sin_t


def _pad_lanes(a):
    return jnp.pad(a, [(0, 0)] * (a.ndim - 1) + [(0, LANES - a.shape[-1])])


def kernel(x, c, ctx, c_ctx, w_mod, b_mod, norm1_w, w_in, ssd_conv_w, ssd_conv_b, ssd_a_log,
           ssd_dt_bias, ssd_d, ssd_norm_w, pool_w, pool_scale, sconv_w, ret_decay_logit,
           w_branch, w_gate, b_gate, w_o, norm2_w, ffn_up, ffn_conv_w, ffn_conv_b, ffn_down,
           final_norm_w):
    nb, seq, d = x.shape
    nctx = ctx.shape[1]
    depth = w_mod.shape[0]
    assert d == D_MODEL and seq % GRID_W == 0
    assert nctx % POOL_ROWS == 0 and seq % POOL_ROWS == 0
    bc = nb * nctx
    t = bc + nb * seq
    tm = _pick((1024, 512, 256), bc, seq)
    rt = _pick((512, 256), bc, seq)

    def row_of(i):
        r = i * tm
        return jnp.where(r < bc, 0, 1 + (r - bc) // seq)

    sizes = (SSD_INNER, SSD_CONV_CH, SSD_HEADS, BRANCH_WIDTH, BRANCH_WIDTH, BRANCH_WIDTH,
             BRANCH_WIDTH, RET_QK, RET_QK, BRANCH_WIDTH, BRANCH_WIDTH)
    offs = [0]
    for sz in sizes:
        offs.append(offs[-1] + sz)
    parts = [w_in[:, :, offs[i]:offs[i + 1]] for i in range(len(sizes))]
    w_main = jnp.concatenate(parts[:2] + parts[3:], axis=-1).astype(BF16)
    w_dt = _pad_lanes(parts[2]).astype(BF16)
    assert w_main.shape[-1] == U_COLS
    wg_b = w_gate.astype(BF16)
    wb_b = w_branch.astype(BF16)
    wo_b = w_o.astype(BF16)
    up_b = ffn_up.astype(BF16)
    down_b = ffn_down.astype(BF16)
    poolw_b = pool_w.astype(BF16)

    nrows = 8 * ((1 + nb + 7) // 8)
    cvec = jnp.zeros((nrows, d), F32).at[0].set(c_ctx).at[1:1 + nb].set(c)
    mod = _mod_call(cvec, w_mod, b_mod)
    mod5 = mod.reshape(depth, nrows, 6, 1, d)

    cos_t, sin_t = _rope_tables(nctx, seq)
    dt_bias_p = _pad_lanes(ssd_dt_bias)
    a_log_p = _pad_lanes(ssd_a_log)
    decay_p = _pad_lanes(ret_decay_logit)
    d_exp = jnp.repeat(ssd_d, SSD_HEAD_DIM, axis=-1)[:, None, :]

    r3 = lambda a: a[:, None, :]
    xs = jnp.concatenate([ctx.reshape(bc, d), x.reshape(nb * seq, d)], axis=0)
    for l in range(depth):
        u, h, dt = _in_proj_call(xs, r3(norm1_w), mod5, w_main, w_dt, l, 0, 1, tm, row_of)
        xbc = _ssd_conv_call(u, ssd_conv_w, r3(ssd_conv_b), l, rt, bc, nctx, seq)
        y_ssd = _ssd_scan_call(xbc, dt, u, dt_bias_p, a_log_p, d_exp, r3(ssd_norm_w), l, nb, nctx, seq)
        y_pool = _pool_call(u, poolw_b, r3(pool_scale), l, bc, nctx, seq)
        y_sc = _sconv_call(u, sconv_w, l, rt, bc, nctx, seq)
        y_ret = _ret_scan_call(u, cos_t, sin_t, decay_p, l, nb, nctx, seq)
        merged = _merge_call(h, (y_ssd, y_pool, y_sc, y_ret), wg_b, b_gate[:, :, None, :], wb_b, l, tm)
        xs, h2 = _outproj_norm_call(merged, wo_b, xs, mod5, r3(norm2_w), l, 2, 3, 4, rt, bc, seq)
        gact = _ffn_gate_call(h2, up_b, ffn_conv_w, r3(ffn_conv_b), l, tm, bc, nctx, seq)
        xs = _res_call(gact, down_b, xs, mod5, l, 5, tm, row_of, "ffn_down_residual")
    out = _final_norm_call(xs, final_norm_w[None, :], bc, nb * seq, tm)
    return out.reshape(nb, seq, d)
```

```python
import functools
import math

import jax
import jax.numpy as jnp
from jax import lax
from jax.experimental import pallas as pl
from jax.experimental.pallas import tpu as pltpu

F32 = jnp.float32
BF16 = jnp.bfloat16

D_MODEL = 2048
GRID_W = 64
EPS = 1e-6
CHUNK = 128
BRANCH_WIDTH = D_MODEL // 2
SSD_INNER = BRANCH_WIDTH
SSD_HEAD_DIM = 64
SSD_HEADS = SSD_INNER // SSD_HEAD_DIM
SSD_GROUPS = 4
SSD_STATE = 128
SSD_CONV_CH = SSD_INNER + 2 * SSD_GROUPS * SSD_STATE
POOL_WINDOWS = (2, 4, 8, 16)
POOL_GROUP = BRANCH_WIDTH // len(POOL_WINDOWS)
RET_HEADS = 8
RET_V_HEAD = BRANCH_WIDTH // RET_HEADS
RET_K_HEAD = RET_V_HEAD // 2
RET_QK = RET_HEADS * RET_K_HEAD
ROPE_BASE = 10000.0
D_FF = 256 * ((8 * D_MODEL // 3 + 255) // 256)

LANES = 128
HALO = 16
VMEM_LIMIT = 56 * 1024 * 1024

U_Z = 0
U_XBC = 1024
U_POOL = 3072
U_SCB = 4096
U_SCC = 5120
U_SCX = 6144
U_RQ = 7168
U_RK = 7680
U_RV = 8192
U_RG = 9216
U_COLS = 10240


def _params(sem):
    return pltpu.CompilerParams(dimension_semantics=sem, vmem_limit_bytes=VMEM_LIMIT)


def _silu(v):
    return v * jax.nn.sigmoid(v)


def _pick(cands, *ns):
    for c in cands:
        if all(n % c == 0 for n in ns):
            return c
    raise ValueError(f"no tile in {cands} divides {ns}")


def _mod_const(v, n):
    if n & (n - 1) == 0:
        return v & (n - 1)
    return lax.rem(v, n)


def _seq_pos(r0, rows, bc, ctx, seq):
    in_ctx = r0 < bc
    g = r0 + lax.broadcasted_iota(jnp.int32, (rows, 1), 0)
    pos = jnp.where(in_ctx, _mod_const(g, ctx), _mod_const(g - bc, seq))
    n = jnp.where(in_ctx, ctx, seq)
    return pos, n


def _mod_kernel(c_ref, w_ref, b_ref, o_ref):
    a = _silu(c_ref[...]).astype(BF16)
    o_ref[...] = jnp.dot(a, w_ref[...].astype(BF16), preferred_element_type=F32) + b_ref[...]


def _mod_call(cvec, w_mod, b_mod):
    nl, d, n6 = w_mod.shape
    r = cvec.shape[0]
    tn = 1024
    return pl.pallas_call(
        _mod_kernel,
        grid=(nl, n6 // tn),
        in_specs=[pl.BlockSpec((r, d), lambda l, j: (0, 0)),
                  pl.BlockSpec((None, d, tn), lambda l, j: (l, 0, j)),
                  pl.BlockSpec((None, 1, tn), lambda l, j: (l, 0, j))],
        out_specs=pl.BlockSpec((None, r, tn), lambda l, j: (l, 0, j)),
        out_shape=jax.ShapeDtypeStruct((nl, r, n6), F32),
        compiler_params=_params(("parallel", "parallel")),
        name="mod_vectors",
    )(cvec, w_mod, b_mod.reshape(nl, 1, n6))


def _normed(x, nw, sh, sc):
    y = x * lax.rsqrt(jnp.mean(jnp.square(x), axis=-1, keepdims=True) + EPS)
    return (y * nw) * (1.0 + sc) + sh


def _in_proj_kernel(x_ref, nw_ref, sh_ref, sc_ref, w_ref, wdt_ref, u_ref, h_ref, dt_ref):
    @pl.when(pl.program_id(1) == 0)
    def _():
        h = _normed(x_ref[...], nw_ref[...], sh_ref[...], sc_ref[...]).astype(BF16)
        h_ref[...] = h
        dt_ref[...] = jnp.dot(h, wdt_ref[...], preferred_element_type=F32)

    u_ref[...] = jnp.dot(h_ref[...], w_ref[...], preferred_element_type=F32).astype(BF16)


def _mod_spec(l, k, tn, row_fn, col_fn):
    return pl.BlockSpec((None, None, None, 1, tn),
                        lambda *g: (l, row_fn(*g), k, 0, col_fn(*g)))


def _mod_row(tile, bc, seq):
    def row(i):
        r = i * tile
        return jnp.where(r < bc, 0, 1 + (r - bc) // seq)
    return row


def _in_proj_call(x, norm_w, mod5, w, wdt, l, k_sh, k_sc, tm, row_of):
    t, d = x.shape
    n = w.shape[-1]
    tn = 1024
    zero = lambda i, j: 0
    row_of = (lambda f: lambda i, j: f(i))(row_of)
    return pl.pallas_call(
        _in_proj_kernel,
        grid=(t // tm, n // tn),
        in_specs=[pl.BlockSpec((tm, d), lambda i, j: (i, 0)),
                  pl.BlockSpec((None, 1, d), lambda i, j: (l, 0, 0)),
                  _mod_spec(l, k_sh, d, row_of, zero),
                  _mod_spec(l, k_sc, d, row_of, zero),
                  pl.BlockSpec((None, d, tn), lambda i, j: (l, 0, j)),
                  pl.BlockSpec((None, d, LANES), lambda i, j: (l, 0, 0))],
        out_specs=[pl.BlockSpec((tm, tn), lambda i, j: (i, j)),
                   pl.BlockSpec((tm, d), lambda i, j: (i, 0)),
                   pl.BlockSpec((tm, LANES), lambda i, j: (i, 0))],
        out_shape=[jax.ShapeDtypeStruct((t, n), BF16),
                   jax.ShapeDtypeStruct((t, d), BF16),
                   jax.ShapeDtypeStruct((t, LANES), F32)],
        compiler_params=_params(("parallel", "arbitrary")),
        name="in_proj",
    )(x, norm_w, mod5, mod5, w, wdt)


def _merge_kernel(h_ref, y0_ref, y1_ref, y2_ref, y3_ref, wg_ref, bg_ref, wb_ref, o_ref):
    h = h_ref[...]
    acc = None
    for i, y_ref in enumerate((y0_ref, y1_ref, y2_ref, y3_ref)):
        gate = jax.nn.sigmoid(jnp.dot(h, wg_ref[i], preferred_element_type=F32) + bg_ref[i])
        term = gate * jnp.dot(y_ref[...], wb_ref[i], preferred_element_type=F32)
        acc = term if acc is None else acc + term
    o_ref[...] = acc.astype(BF16)


def _merge_call(h, ys, wg, bg, wb, l, tm):
    t, d = h.shape
    bw = ys[0].shape[-1]
    tn = 256
    nb = len(ys)
    return pl.pallas_call(
        _merge_kernel,
        grid=(t // tm, d // tn),
        in_specs=[pl.BlockSpec((tm, d), lambda i, j: (i, 0))]
                 + [pl.BlockSpec((tm, bw), lambda i, j: (i, 0)) for _ in ys]
                 + [pl.BlockSpec((None, nb, d, tn), lambda i, j: (l, 0, 0, j)),
                    pl.BlockSpec((None, nb, 1, tn), lambda i, j: (l, 0, 0, j)),
                    pl.BlockSpec((None, nb, bw, tn), lambda i, j: (l, 0, 0, j))],
        out_specs=pl.BlockSpec((tm, tn), lambda i, j: (i, j)),
        out_shape=jax.ShapeDtypeStruct((t, d), BF16),
        compiler_params=_params(("parallel", "arbitrary")),
        name="branch_merge",
    )(h, *ys, wg, bg, wb)


def _res_kernel(a_ref, w_ref, x_ref, g_ref, o_ref):
    o_ref[...] = x_ref[...] + g_ref[...] * jnp.dot(a_ref[...], w_ref[...],
                                                   preferred_element_type=F32)


def _res_call(a, w, x, mod5, l, k_gate, tm, row_of, name):
    t, kdim = a.shape
    d = x.shape[-1]
    tn = 512
    return pl.pallas_call(
        _res_kernel,
        grid=(t // tm, d // tn),
        in_specs=[pl.BlockSpec((tm, kdim), lambda i, j: (i, 0)),
                  pl.BlockSpec((None, kdim, tn), lambda i, j: (l, 0, j)),
                  pl.BlockSpec((tm, tn), lambda i, j: (i, j)),
                  _mod_spec(l, k_gate, tn, lambda i, j: row_of(i), lambda i, j: j)],
        out_specs=pl.BlockSpec((tm, tn), lambda i, j: (i, j)),
        out_shape=jax.ShapeDtypeStruct((t, d), F32),
        compiler_params=_params(("parallel", "arbitrary")),
        name=name,
    )(a, w, x, mod5)


EPI_ROWS = 128


def _outproj_norm_kernel(a_ref, w_ref, x_ref, g_ref, nw_ref, sh_ref, sc_ref, xo_ref, ho_ref, y0, y1):
    s = pl.program_id(0)
    tm = a_ref.shape[0]

    @pl.when(s == 0)
    def _():
        y1[...] = jnp.zeros_like(y1)

    def step(y_cur, y_prev):
        for r in range(0, tm, EPI_ROWS):
            rs = slice(r, r + EPI_ROWS)
            y_cur[rs, :] = jnp.dot(a_ref[rs, :], w_ref[...], preferred_element_type=F32)
            xn = x_ref[rs, :] + g_ref[...] * y_prev[rs, :]
            xo_ref[rs, :] = xn
            ho_ref[rs, :] = _normed(xn, nw_ref[...], sh_ref[...], sc_ref[...]).astype(BF16)

    @pl.when(s % 2 == 0)
    def _():
        step(y0, y1)

    @pl.when(s % 2 == 1)
    def _():
        step(y1, y0)


def _outproj_norm_call(a, w, x, mod5, norm_w, l, k_gate, k_sh, k_sc, tm, bc, seq):
    t, kdim = a.shape
    d = x.shape[-1]
    ni = t // tm
    row = _mod_row(tm, bc, seq)
    cur = lambda s: jnp.minimum(s, ni - 1)
    prev = lambda s: jnp.maximum(s - 1, 0)
    zero = lambda s: 0
    return pl.pallas_call(
        _outproj_norm_kernel,
        grid=(ni + 1,),
        in_specs=[pl.BlockSpec((tm, kdim), lambda s: (cur(s), 0)),
                  pl.BlockSpec((None, kdim, d), lambda s: (l, 0, 0), pipeline_mode=pl.Buffered(1)),
                  pl.BlockSpec((tm, d), lambda s: (prev(s), 0)),
                  _mod_spec(l, k_gate, d, lambda s: row(prev(s)), zero),
                  pl.BlockSpec((None, 1, d), lambda s: (l, 0, 0)),
                  _mod_spec(l, k_sh, d, lambda s: row(prev(s)), zero),
                  _mod_spec(l, k_sc, d, lambda s: row(prev(s)), zero)],
        out_specs=[pl.BlockSpec((tm, d), lambda s: (prev(s), 0)),
                   pl.BlockSpec((tm, d), lambda s: (prev(s), 0))],
        out_shape=[jax.ShapeDtypeStruct((t, d), F32),
                   jax.ShapeDtypeStruct((t, d), BF16)],
        scratch_shapes=[pltpu.VMEM((tm, d), F32), pltpu.VMEM((tm, d), F32)],
        compiler_params=_params(("arbitrary",)),
        name="out_proj_residual_norm",
    )(a, w, x, mod5, norm_w, mod5, mod5)


FFN_PIECE_ROWS = 64
FFN_SLABS = 4
SEQ_ALIGN = 256


def _ffn_gate_kernel(h_ref, hp_ref, hn_ref, wa_ref, wb_ref, cwa_ref, cwb_ref, ba_ref, bb_ref, o_ref,
                     h_s, a_s, b_s, *, nj, bc, ctx, seq):
    s = pl.program_id(0)
    tm = h_ref.shape[0]

    @pl.when(s == 0)
    def _():
        a_s[1] = jnp.zeros(a_s.shape[1:], F32)
        b_s[1] = jnp.zeros(b_s.shape[1:], F32)

    @pl.when(s % nj == 0)
    def _():
        h_s[0:HALO, :] = hp_ref[...]
        h_s[HALO:HALO + tm, :] = h_ref[...]
        h_s[HALO + tm:, :] = hn_ref[...]

    r0 = (jnp.maximum(s - 1, 0) // nj) * tm
    in_ctx = r0 < bc
    tn = o_ref.shape[1]
    rows_ext = tm + 2 * HALO
    pr = FFN_PIECE_ROWS
    row_first = lax.broadcasted_iota(jnp.int32, (pr, 1), 0) == 0
    row_last = lax.broadcasted_iota(jnp.int32, (pr, 1), 0) == pr - 1

    def seq_start(g):
        return jnp.where(in_ctx, _mod_const(g, ctx), _mod_const(g - bc, seq)) == 0

    def taps(w_ref, bias_ref, cs):
        return [jnp.broadcast_to(w_ref[j:j + 1, cs], (pr, LANES)) for j in range(3)] + \
               [jnp.broadcast_to(bias_ref[:, cs], (pr, LANES))]

    col_slabs = [slice(c, c + LANES) for c in range(0, tn, LANES)]
    taps_a = [taps(cwa_ref, ba_ref, cs) for cs in col_slabs]
    taps_b = [taps(cwb_ref, bb_ref, cs) for cs in col_slabs]

    def conv_piece(src, r, cs, tp, kill_prev, kill_next):
        base = HALO + r
        ext = src[base - 8:base + pr + 8, cs]
        xm1 = pltpu.roll(ext, 1, 0)[8:8 + pr]
        x00 = ext[8:8 + pr]
        xp1 = pltpu.roll(ext, pr + 15, 0)[8:8 + pr]
        if kill_prev is not None:
            xm1 = jnp.where(kill_prev, 0.0, xm1)
        if kill_next is not None:
            xp1 = jnp.where(kill_next, 0.0, xp1)
        return xm1 * tp[0] + x00 * tp[1] + xp1 * tp[2] + tp[3]

    def epilogue(a_prev, b_prev, lo, hi):
        for r in range(lo, hi, pr):
            kill_prev = kill_next = None
            if r % SEQ_ALIGN == 0:
                kill_prev = jnp.logical_and(row_first, seq_start(r0 + r))
            if (r + pr) % SEQ_ALIGN == 0:
                kill_next = jnp.logical_and(row_last, seq_start(r0 + r + pr))
            for ci, cs in enumerate(col_slabs):
                a = conv_piece(a_prev, r, cs, taps_a[ci], kill_prev, kill_next)
                b = conv_piece(b_prev, r, cs, taps_b[ci], kill_prev, kill_next)
                o_ref[r:r + pr, cs] = (_silu(a) * b).astype(BF16)

    def step(a_cur, b_cur, a_prev, b_prev):
        nslab = FFN_SLABS
        for k in range(nslab):
            m_lo = (k * rows_ext // nslab) // HALO * HALO
            m_hi = rows_ext if k == nslab - 1 else ((k + 1) * rows_ext // nslab) // HALO * HALO
            h = h_s[m_lo:m_hi, :]
            a_cur[m_lo:m_hi, :] = jnp.dot(h, wa_ref[...], preferred_element_type=F32)
            b_cur[m_lo:m_hi, :] = jnp.dot(h, wb_ref[...], preferred_element_type=F32)
            epilogue(a_prev, b_prev, k * tm // nslab, (k + 1) * tm // nslab)

    cur = s % 2
    step(a_s.at[cur], b_s.at[cur], a_s.at[1 - cur], b_s.at[1 - cur])


def _ffn_gate_call(h, w, cw, cb, l, tm, bc, ctx, seq):
    t, d = h.shape
    tn = 512
    nj = D_FF // tn
    ni = t // tm
    per = tm // HALO
    last = t // HALO - 1
    ci = lambda s: jnp.minimum(s // nj, ni - 1)
    cj = lambda s: s % nj
    pi = lambda s: jnp.maximum(s - 1, 0) // nj
    pj = lambda s: jnp.maximum(s - 1, 0) % nj
    return pl.pallas_call(
        functools.partial(_ffn_gate_kernel, nj=nj, bc=bc, ctx=ctx, seq=seq),
        grid=(ni * nj + 1,),
        in_specs=[pl.BlockSpec((tm, d), lambda s: (ci(s), 0)),
                  pl.BlockSpec((HALO, d), lambda s: (jnp.maximum(ci(s) * per - 1, 0), 0)),
                  pl.BlockSpec((HALO, d), lambda s: (jnp.minimum((ci(s) + 1) * per, last), 0)),
                  pl.BlockSpec((None, d, tn), lambda s: (l, 0, cj(s))),
                  pl.BlockSpec((None, d, tn), lambda s: (l, 0, nj + cj(s))),
                  pl.BlockSpec((None, 3, tn), lambda s: (l, 0, pj(s))),
                  pl.BlockSpec((None, 3, tn), lambda s: (l, 0, nj + pj(s))),
                  pl.BlockSpec((None, 1, tn), lambda s: (l, 0, pj(s))),
                  pl.BlockSpec((None, 1, tn), lambda s: (l, 0, nj + pj(s)))],
        out_specs=pl.BlockSpec((tm, tn), lambda s: (pi(s), pj(s))),
        out_shape=jax.ShapeDtypeStruct((t, D_FF), BF16),
        scratch_shapes=[pltpu.VMEM((tm + 2 * HALO, d), BF16)]
                       + [pltpu.VMEM((2, tm + 2 * HALO, tn), F32) for _ in range(2)],
        compiler_params=_params(("arbitrary",)),
        name="ffn_up_conv_gate",
    )(h, h, h, w, w, cw, cw, cb, cb)


def _final_norm_kernel(x_ref, w_ref, o_ref):
    x = x_ref[...]
    o_ref[...] = (x * lax.rsqrt(jnp.mean(jnp.square(x), axis=-1, keepdims=True) + EPS)) * w_ref[...]


def _final_norm_call(x, w, row0, rows, tm):
    d = x.shape[-1]
    off = row0 // tm
    return pl.pallas_call(
        _final_norm_kernel,
        grid=(rows // tm,),
        in_specs=[pl.BlockSpec((tm, d), lambda i: (i + off, 0)),
                  pl.BlockSpec((1, d), lambda i: (0, 0))],
        out_specs=pl.BlockSpec((tm, d), lambda i: (i, 0)),
        out_shape=jax.ShapeDtypeStruct((rows, d), F32),
        compiler_params=_params(("parallel",)),
        name="final_norm",
    )(x, w)


def _halo_specs(rt, cols, col_idx, t):
    per = rt // HALO
    last = t // HALO - 1
    return [pl.BlockSpec((rt, cols), lambda i, j: (i, col_idx(j))),
            pl.BlockSpec((HALO, cols), lambda i, j: (jnp.maximum(i * per - 1, 0), col_idx(j))),
            pl.BlockSpec((HALO, cols), lambda i, j: (jnp.minimum((i + 1) * per, last), col_idx(j)))]


def _shift_pm1(x, prev_row, next_row, pos, n):
    rt = x.shape[0]
    row = lax.broadcasted_iota(jnp.int32, (rt, 1), 0)
    xm1 = jnp.where(row == 0, prev_row, pltpu.roll(x, 1, 0))
    xm1 = jnp.where(pos == 0, 0.0, xm1)
    xp1 = jnp.where(row == rt - 1, next_row, pltpu.roll(x, rt - 1, 0))
    xp1 = jnp.where(pos == n - 1, 0.0, xp1)
    return xm1, xp1


def _conv3(x, prev_row, next_row, w, pos, n):
    xm1, xp1 = _shift_pm1(x, prev_row, next_row, pos, n)
    return xm1 * w[0:1] + x * w[1:2] + xp1 * w[2:3]


def _ssd_conv_kernel(x_ref, p_ref, n_ref, w_ref, b_ref, o_ref, *, bc, ctx, seq):
    rt = x_ref.shape[0]
    pos, n = _seq_pos(pl.program_id(0) * rt, rt, bc, ctx, seq)
    x = x_ref[...].astype(F32)
    prev_row = p_ref[...].astype(F32)[HALO - 1:HALO]
    next_row = n_ref[...].astype(F32)[0:1]
    y = _conv3(x, prev_row, next_row, w_ref[...], pos, n) + b_ref[...]
    o_ref[...] = _silu(y).astype(BF16)


def _ssd_conv_call(u, w, b, l, rt, bc, ctx, seq):
    t = u.shape[0]
    cols = 1024
    nj = SSD_CONV_CH // cols
    base = U_XBC // cols
    return pl.pallas_call(
        functools.partial(_ssd_conv_kernel, bc=bc, ctx=ctx, seq=seq),
        grid=(t // rt, nj),
        in_specs=_halo_specs(rt, cols, lambda j: base + j, t)
                 + [pl.BlockSpec((None, 3, cols), lambda i, j: (l, 0, j)),
                    pl.BlockSpec((None, 1, cols), lambda i, j: (l, 0, j))],
        out_specs=pl.BlockSpec((rt, cols), lambda i, j: (i, j)),
        out_shape=jax.ShapeDtypeStruct((t, SSD_CONV_CH), BF16),
        compiler_params=_params(("parallel", "parallel")),
        name="ssd_conv",
    )(u, u, u, w, b)


def _sconv_kernel(b_ref, c_ref, cp_ref, cn_ref, x_ref, xp_ref, xn_ref, w_ref, o_ref, *, bc, ctx, seq):
    rt = b_ref.shape[0]
    pos, n = _seq_pos(pl.program_id(0) * rt, rt, bc, ctx, seq)
    cx = c_ref[...].astype(F32) * x_ref[...].astype(F32)
    prev_row = (cp_ref[...].astype(F32) * xp_ref[...].astype(F32))[HALO - 1:HALO]
    next_row = (cn_ref[...].astype(F32) * xn_ref[...].astype(F32))[0:1]
    y = b_ref[...].astype(F32) * _conv3(cx, prev_row, next_row, w_ref[...], pos, n)
    o_ref[...] = y.astype(BF16)


def _sconv_call(u, w, l, rt, bc, ctx, seq):
    t = u.shape[0]
    cols = BRANCH_WIDTH
    return pl.pallas_call(
        functools.partial(_sconv_kernel, bc=bc, ctx=ctx, seq=seq),
        grid=(t // rt, 1),
        in_specs=[pl.BlockSpec((rt, cols), lambda i, j: (i, U_SCB // cols))]
                 + _halo_specs(rt, cols, lambda j: U_SCC // cols, t)
                 + _halo_specs(rt, cols, lambda j: U_SCX // cols, t)
                 + [pl.BlockSpec((None, 3, cols), lambda i, j: (l, 0, 0))],
        out_specs=pl.BlockSpec((rt, cols), lambda i, j: (i, 0)),
        out_shape=jax.ShapeDtypeStruct((t, cols), BF16),
        compiler_params=_params(("parallel", "parallel")),
        name="short_conv",
    )(u, u, u, u, u, u, u, w)


POOL_ROWS = 256


def _pool_kernel(x_ref, p_ref, n_ref, w_ref, s_ref, o_ref, *, bc, ctx, seq):
    rt = POOL_ROWS
    r0 = pl.program_id(0) * rt
    pos, n = _seq_pos(r0, rt, bc, ctx, seq)
    in_ctx = r0 < bc
    p0 = jnp.where(in_ctx, _mod_const(r0, ctx), _mod_const(r0 - bc, seq))
    has_prev = p0 != 0
    has_next = p0 + rt != n
    ext_rows = rt + 2 * HALO
    for gi, win in enumerate(POOL_WINDOWS):
        cs = slice(gi * POOL_GROUP, (gi + 1) * POOL_GROUP)
        g = x_ref[:, cs].astype(F32)
        prev = jnp.where(has_prev, p_ref[:, cs].astype(F32), 0.0)
        nxt = jnp.where(has_next, n_ref[:, cs].astype(F32), 0.0)
        ext = jnp.concatenate([prev, g, nxt], axis=0)
        s = ext + pltpu.roll(ext, 1, 0)
        half = 1
        while 2 * half < win:
            s = pltpu.roll(s, half, 0) + pltpu.roll(s, ext_rows - half, 0)
            half *= 2
        s = s[HALO:HALO + rt]
        lo = jnp.clip(pos - win // 2, 0, n)
        hi = jnp.clip(pos - win // 2 + win, 0, n)
        mean = s / (hi - lo).astype(F32)
        pooled = (mean - g).astype(BF16)
        y = jnp.dot(pooled, w_ref[gi], preferred_element_type=F32) * s_ref[:, cs]
        o_ref[:, cs] = y.astype(BF16)


def _pool_call(u, w, scale, l, bc, ctx, seq):
    t = u.shape[0]
    cols = BRANCH_WIDTH
    rt = POOL_ROWS
    ng = len(POOL_WINDOWS)
    return pl.pallas_call(
        functools.partial(_pool_kernel, bc=bc, ctx=ctx, seq=seq),
        grid=(t // rt, 1),
        in_specs=_halo_specs(rt, cols, lambda j: U_POOL // cols, t)
                 + [pl.BlockSpec((None, ng, POOL_GROUP, POOL_GROUP), lambda i, j: (l, 0, 0, 0)),
                    pl.BlockSpec((None, 1, cols), lambda i, j: (l, 0, 0))],
        out_specs=pl.BlockSpec((rt, cols), lambda i, j: (i, 0)),
        out_shape=jax.ShapeDtypeStruct((t, cols), BF16),
        compiler_params=_params(("parallel", "parallel")),
        name="multiscale_pool",
    )(u, u, u, w, scale)


def _scan_positions(ncc, nk):
    def kpos(sw, s):
        back = jnp.where(s < ncc, ncc - 1 - s, nk - 1 - (s - ncc))
        return jnp.where(sw == 0, back, s)
    return kpos


def _chunk_row(b, k, ncc, ncl, nb):
    return jnp.where(k < ncc, b * ncc + k, nb * ncc + b * ncl + (k - ncc))


def _tri_masks():
    row = lax.broadcasted_iota(jnp.int32, (CHUNK, CHUNK), 0)
    col = lax.broadcasted_iota(jnp.int32, (CHUNK, CHUNK), 1)
    return row, col, col <= row, col >= row


def _colb(x, h):
    return jnp.broadcast_to(x[:, h:h + 1], (x.shape[0], LANES))


MASKED = -1e30


def _ssd_scan_kernel(xbc_ref, dt_ref, z_ref, bias_ref, alog_ref, dexp_ref, nw_ref, e_ref, o_ref,
                     hb_all, hf_s, hb_s, y_s, vw_s, *, ncc, nk):
    sw = pl.program_id(1)
    s = pl.program_id(2)
    kpos = _scan_positions(ncc, nk)(sw, s)
    row, col, lower, upper = _tri_masks()
    lane_lo = col < SSD_HEAD_DIM
    row_lo1 = lane_lo[0:1]
    nh = SSD_HEADS
    n_pairs = nh // 2
    pairs_per_group = n_pairs // SSD_GROUPS
    gw = SSD_INNER // SSD_GROUPS

    dtt = dt_ref[...].T[0:nh]
    dt2 = jax.nn.softplus(jnp.concatenate([dtt, dtt], axis=0) + bias_ref[...])
    la2 = dt2 * (-jnp.exp(alog_ref[...]))
    tri2 = jnp.concatenate([upper.astype(F32), lower.astype(F32)], axis=1)
    cum2 = jnp.dot(la2, tri2, precision=lax.Precision.HIGHEST, preferred_element_type=F32)
    cum_f = cum2[0:nh, 0:CHUNK]
    cum_b = cum2[nh:2 * nh, CHUNK:2 * CHUNK]

    def store_state_operand(wt):
        hi = wt.astype(BF16).astype(F32)
        stack = jnp.concatenate([hi, wt - hi, jnp.zeros((CHUNK - 2 * nh, CHUNK), F32)], axis=0)
        w_full = jnp.dot(stack.T.astype(BF16), e_ref[...], preferred_element_type=F32)
        vw_s[...] = (xbc_ref[:, 0:SSD_INNER].astype(F32) * w_full).astype(BF16)

    def pair_scalar(x, p, j):
        return jnp.where(row_lo1, jnp.broadcast_to(x[2 * p:2 * p + 1, j:j + 1], (1, LANES)),
                         jnp.broadcast_to(x[2 * p + 1:2 * p + 2, j:j + 1], (1, LANES)))

    def state_update(h_s, g, dec_rows):
        gs = slice(g * gw, (g + 1) * gw)
        bm = xbc_ref[:, SSD_INNER + g * SSD_STATE:SSD_INNER + (g + 1) * SSD_STATE]
        st = lax.dot_general(bm, vw_s[:, gs], (((0,), (0,)), ((), ())), preferred_element_type=F32)
        h_s[:, gs] = h_s[:, gs] * jnp.concatenate(dec_rows, axis=1) + st

    @pl.when(sw == 0)
    def _():
        @pl.when(s == 0)
        def _():
            hb_s[...] = jnp.zeros_like(hb_s)

        hb_all[kpos] = hb_s[...].astype(BF16)
        store_state_operand(jnp.exp(cum_b[:, 0:1] - cum_b) * dt2[nh:2 * nh])
        for g in range(SSD_GROUPS):
            dec_rows = [jnp.exp(pair_scalar(cum_b, g * pairs_per_group + pp, 0))
                        for pp in range(pairs_per_group)]
            state_update(hb_s, g, dec_rows)

    @pl.when(sw == 1)
    def _():
        @pl.when(s == 0)
        def _():
            hf_s[...] = jnp.zeros_like(hf_s)

        hb_in = hb_all[kpos]
        store_state_operand(jnp.exp(cum_f[:, CHUNK - 1:CHUNK] - cum_f) * dt2[0:nh])
        cols = jnp.concatenate([cum_f, cum_b, jnp.zeros((CHUNK - 2 * nh, CHUNK), F32)], axis=0).T
        for g in range(SSD_GROUPS):
            gs = slice(g * gw, (g + 1) * gw)
            bm = xbc_ref[:, SSD_INNER + g * SSD_STATE:SSD_INNER + (g + 1) * SSD_STATE]
            cm = xbc_ref[:, SSD_INNER + SSD_GROUPS * SSD_STATE + g * SSD_STATE:
                         SSD_INNER + SSD_GROUPS * SSD_STATE + (g + 1) * SSD_STATE]
            sc = lax.dot_general(cm, bm, (((1,), (1,)), ((), ())), preferred_element_type=F32)
            hcat = jnp.concatenate([hf_s[:, gs].astype(BF16), hb_in[:, gs]], axis=1)
            gfb = jnp.dot(cm, hcat, preferred_element_type=F32)
            dec_rows = []
            for pp in range(pairs_per_group):
                p = g * pairs_per_group + pp
                ps = slice(p * LANES, (p + 1) * LANES)
                xs_pair = xbc_ref[:, ps]
                cfs, rbs, ms = [], [], []
                for h in (2 * p, 2 * p + 1):
                    cf_h = _colb(cols, h)
                    rb_h = _colb(cols, nh + h)
                    dec_f = jnp.exp(jnp.where(lower, cf_h - cum_f[h:h + 1], MASKED))
                    dec_b = jnp.exp(jnp.where(upper, rb_h - cum_b[h:h + 1], MASKED))
                    m = sc * (dec_f * dt2[h:h + 1] + dec_b * dt2[nh + h:nh + h + 1])
                    ms.append(m.astype(BF16))
                    cfs.append(cf_h)
                    rbs.append(rb_h)
                zero = jnp.zeros_like(xs_pair)
                x_split = jnp.concatenate([jnp.where(lane_lo, xs_pair, zero),
                                           jnp.where(lane_lo, zero, xs_pair)], axis=0)
                y_intra = jnp.dot(jnp.concatenate(ms, axis=1), x_split, preferred_element_type=F32)
                cf_p = jnp.where(lane_lo, cfs[0], cfs[1])
                rb_p = jnp.where(lane_lo, rbs[0], rbs[1])
                lo = pp * LANES
                y_inter = gfb[:, lo:lo + LANES] * jnp.exp(cf_p) + gfb[:, gw + lo:gw + lo + LANES] * jnp.exp(rb_p)
                y_s[:, ps] = y_intra + y_inter
                dec_rows.append(jnp.exp(pair_scalar(cum_f, p, CHUNK - 1)))
            state_update(hf_s, g, dec_rows)

        xs32 = xbc_ref[:, 0:SSD_INNER].astype(F32)
        y = (y_s[...] + xs32 * dexp_ref[...]) * _silu(z_ref[...].astype(F32))
        y = y * lax.rsqrt(jnp.mean(jnp.square(y), axis=-1, keepdims=True) + EPS)
        o_ref[...] = (y * nw_ref[...]).astype(BF16)


def _head_spread():
    r = jnp.arange(CHUNK)[:, None]
    c = jnp.arange(SSD_INNER)[None, :] // SSD_HEAD_DIM
    return ((r < 2 * SSD_HEADS) & (r % SSD_HEADS == c)).astype(BF16)


def _ssd_scan_call(xbc, dt, u, dt_bias, a_log, d_exp, norm_w, l, nb, ctx, seq):
    t = xbc.shape[0]
    ncc, ncl = ctx // CHUNK, seq // CHUNK
    nk = ncc + ncl
    kpos = _scan_positions(ncc, nk)

    def in_row(b, sw, s):
        return _chunk_row(b, kpos(sw, s), ncc, ncl, nb)

    def out_row(b, sw, s):
        return _chunk_row(b, jnp.where(sw == 0, 0, s), ncc, ncl, nb)

    return pl.pallas_call(
        functools.partial(_ssd_scan_kernel, ncc=ncc, nk=nk),
        grid=(nb, 2, nk),
        in_specs=[pl.BlockSpec((CHUNK, SSD_CONV_CH), lambda b, sw, s: (in_row(b, sw, s), 0)),
                  pl.BlockSpec((CHUNK, LANES), lambda b, sw, s: (in_row(b, sw, s), 0)),
                  pl.BlockSpec((CHUNK, SSD_INNER), lambda b, sw, s: (in_row(b, sw, s), U_Z // SSD_INNER)),
                  pl.BlockSpec((None, 2 * SSD_HEADS, CHUNK), lambda b, sw, s: (l, 0, 0)),
                  pl.BlockSpec((None, 2 * SSD_HEADS, CHUNK), lambda b, sw, s: (l, 0, 0)),
                  pl.BlockSpec((None, 1, SSD_INNER), lambda b, sw, s: (l, 0, 0)),
                  pl.BlockSpec((None, 1, SSD_INNER), lambda b, sw, s: (l, 0, 0)),
                  pl.BlockSpec((CHUNK, SSD_INNER), lambda b, sw, s: (0, 0))],
        out_specs=pl.BlockSpec((CHUNK, SSD_INNER), lambda b, sw, s: (out_row(b, sw, s), 0)),
        out_shape=jax.ShapeDtypeStruct((t, SSD_INNER), BF16),
        scratch_shapes=[pltpu.VMEM((nk, SSD_STATE, SSD_INNER), BF16),
                        pltpu.VMEM((SSD_STATE, SSD_INNER), F32),
                        pltpu.VMEM((SSD_STATE, SSD_INNER), F32),
                        pltpu.VMEM((CHUNK, SSD_INNER), F32),
                        pltpu.VMEM((CHUNK, SSD_INNER), BF16)],
        compiler_params=_params(("parallel", "arbitrary", "arbitrary")),
        name="ssd_scan",
    )(xbc, dt, u, dt_bias, a_log, d_exp, norm_w, _head_spread())


def _ret_scan_kernel(q_ref, k_ref, v_ref, g_ref, cos_ref, sin_ref, dl_ref, o_ref,
                     hb_all, hf_s, hb_s, dec_s, tab_s, *, ncc, nk):
    sw = pl.program_id(1)
    s = pl.program_id(2)
    kpos = _scan_positions(ncc, nk)(sw, s)
    row, col, lower, upper = _tri_masks()
    lane_lo = col < RET_K_HEAD
    la = jax.nn.log_sigmoid(dl_ref[...])
    rowf = row.astype(F32)
    diff = (row - col).astype(F32)
    n_pairs = RET_HEADS // 2

    cosv = cos_ref[...]
    sinv = sin_ref[...]
    first16 = (col & 31) < 16

    def rope(x):
        swapped = jnp.where(first16, pltpu.roll(x, LANES - 16, 1), pltpu.roll(x, 16, 1))
        return x * cosv + swapped * sinv

    def head_scalar(d, h):
        return la[d:d + 1, h:h + 1]

    def pair_k(p):
        return rope(k_ref[:, p * LANES:(p + 1) * LANES].astype(F32)).astype(BF16)

    def v_of(h):
        return v_ref[:, h * RET_V_HEAD:(h + 1) * RET_V_HEAD]

    def state_update(h_s, d, p, kp, w_tab):
        h0, h1 = 2 * p, 2 * p + 1
        vw = jnp.concatenate([(v_of(h0).astype(F32) * tab_s[w_tab, h0]).astype(BF16),
                              (v_of(h1).astype(F32) * tab_s[w_tab, h1]).astype(BF16)], axis=1)
        st = lax.dot_general(kp, vw, (((0,), (0,)), ((), ())), preferred_element_type=F32)
        top = row < RET_K_HEAD
        acc = jnp.where(top, st[:, 0:RET_V_HEAD], st[:, RET_V_HEAD:2 * RET_V_HEAD])
        dec = jnp.where(top, jnp.exp(float(CHUNK) * head_scalar(d, h0)),
                        jnp.exp(float(CHUNK) * head_scalar(d, h1)))
        rs = slice(p * LANES, (p + 1) * LANES)
        h_s[rs, :] = h_s[rs, :] * dec + acc

    @pl.when(jnp.logical_and(sw == 0, s == 0))
    def _():
        hb_s[...] = jnp.zeros_like(hb_s)
        for h in range(RET_HEADS):
            laf = head_scalar(0, h)
            lab = head_scalar(1, h)
            dec_s[h] = (jnp.exp(jnp.where(lower, diff * laf, MASKED))
                        + jnp.exp(jnp.where(upper, -diff * lab, MASKED)))
            tab_s[0, h] = jnp.exp((rowf + 1.0) * laf)
            tab_s[1, h] = jnp.exp((float(CHUNK) - rowf) * lab)
            tab_s[2, h] = jnp.exp((float(CHUNK - 1) - rowf) * laf)
            tab_s[3, h] = jnp.exp(rowf * lab)

    @pl.when(sw == 0)
    def _():
        hb_all[kpos] = hb_s[...].astype(BF16)
        for p in range(n_pairs):
            state_update(hb_s, 1, p, pair_k(p), 3)

    @pl.when(sw == 1)
    def _():
        @pl.when(s == 0)
        def _():
            hf_s[...] = jnp.zeros_like(hf_s)

        hb_in = hb_all[kpos]
        for p in range(n_pairs):
            h0, h1 = 2 * p, 2 * p + 1
            qp = rope(q_ref[:, p * LANES:(p + 1) * LANES].astype(F32) * (RET_K_HEAD ** -0.5))
            kp = pair_k(p)
            rs = slice(p * LANES, (p + 1) * LANES)
            qcat = jnp.concatenate([jnp.where(lane_lo, qp, 0.0), jnp.where(lane_lo, 0.0, qp)],
                                   axis=0).astype(BF16)
            sc = lax.dot_general(qcat, kp, (((1,), (1,)), ((), ())), preferred_element_type=F32)
            hcat = jnp.concatenate([hf_s[rs, :].astype(BF16), hb_in[rs, :]], axis=1)
            gfb = jnp.dot(qcat, hcat, preferred_element_type=F32)
            m = jnp.concatenate([(sc[0:CHUNK] * dec_s[h0]).astype(BF16),
                                 (sc[CHUNK:2 * CHUNK] * dec_s[h1]).astype(BF16)], axis=1)
            zero = jnp.zeros((CHUNK, RET_V_HEAD), BF16)
            v_blk = jnp.concatenate([jnp.concatenate([v_of(h0), zero], axis=1),
                                     jnp.concatenate([zero, v_of(h1)], axis=1)], axis=0)
            y_intra = jnp.dot(m, v_blk, preferred_element_type=F32)
            for i, h in enumerate((h0, h1)):
                vs = slice(h * RET_V_HEAD, (h + 1) * RET_V_HEAD)
                g_h = gfb[i * CHUNK:(i + 1) * CHUNK]
                y = (y_intra[:, i * RET_V_HEAD:(i + 1) * RET_V_HEAD]
                     + g_h[:, 0:RET_V_HEAD] * tab_s[0, h] + g_h[:, RET_V_HEAD:2 * RET_V_HEAD] * tab_s[1, h])
                mu = jnp.mean(y, axis=-1, keepdims=True)
                yc = y - mu
                var = jnp.mean(jnp.square(yc), axis=-1, keepdims=True)
                yn = yc * lax.rsqrt(var + EPS)
                o_ref[:, vs] = (_silu(g_ref[:, vs].astype(F32)) * yn).astype(BF16)
            state_update(hf_s, 0, p, kp, 2)


def _ret_scan_call(u, cos_t, sin_t, decay_logit, l, nb, ctx, seq):
    t = u.shape[0]
    ncc, ncl = ctx // CHUNK, seq // CHUNK
    nk = ncc + ncl
    kpos = _scan_positions(ncc, nk)

    def in_row(b, sw, s):
        return _chunk_row(b, kpos(sw, s), ncc, ncl, nb)

    def out_row(b, sw, s):
        return _chunk_row(b, jnp.where(sw == 0, 0, s), ncc, ncl, nb)

    bw = BRANCH_WIDTH
    return pl.pallas_call(
        functools.partial(_ret_scan_kernel, ncc=ncc, nk=nk),
        grid=(nb, 2, nk),
        in_specs=[pl.BlockSpec((CHUNK, RET_QK), lambda b, sw, s: (in_row(b, sw, s), U_RQ // RET_QK)),
                  pl.BlockSpec((CHUNK, RET_QK), lambda b, sw, s: (in_row(b, sw, s), U_RK // RET_QK)),
                  pl.BlockSpec((CHUNK, bw), lambda b, sw, s: (in_row(b, sw, s), U_RV // bw)),
                  pl.BlockSpec((CHUNK, bw), lambda b, sw, s: (in_row(b, sw, s), U_RG // bw)),
                  pl.BlockSpec((CHUNK, LANES), lambda b, sw, s: (kpos(sw, s), 0)),
                  pl.BlockSpec((CHUNK, LANES), lambda b, sw, s: (kpos(sw, s), 0)),
                  pl.BlockSpec((None, 2, LANES), lambda b, sw, s: (l, 0, 0))],
        out_specs=pl.BlockSpec((CHUNK, bw), lambda b, sw, s: (out_row(b, sw, s), 0)),
        out_shape=jax.ShapeDtypeStruct((t, bw), BF16),
        scratch_shapes=[pltpu.VMEM((nk, RET_QK, RET_V_HEAD), BF16),
                        pltpu.VMEM((RET_QK, RET_V_HEAD), F32),
                        pltpu.VMEM((RET_QK, RET_V_HEAD), F32),
                        pltpu.VMEM((RET_HEADS, CHUNK, CHUNK), F32),
                        pltpu.VMEM((4, RET_HEADS, CHUNK, LANES), F32)],
        compiler_params=_params(("parallel", "arbitrary", "arbitrary")),
        name="retention_scan",
    )(u, u, u, u, cos_t, sin_t, decay_logit)


def _rope_tables(ctx, seq):
    quarter = RET_K_HEAD // 4
    t = jnp.arange(seq)
    rowp = (t // GRID_W).astype(F32)
    colp = (t % GRID_W).astype(F32)
    inv = ROPE_BASE ** (-jnp.arange(quarter, dtype=F32) / quarter)
    ang_r = rowp[:, None] * inv[None, :]
    ang_c = colp[:, None] * inv[None, :]

    def blocks(ang):
        c, s = jnp.cos(ang), jnp.sin(ang)
        return jnp.concatenate([c, c], axis=-1), jnp.concatenate([-s, s], axis=-1)

    cr, sr = blocks(ang_r)
    cc, sc = blocks(ang_c)
    cos_h = jnp.concatenate([cr, cc], axis=-1)
    sin_h = jnp.concatenate([sr, sc], axis=-1)
    cos_l = jnp.concatenate([cos_h, cos_h], axis=-1)
    sin_l = jnp.concatenate([sin_h, sin_h], axis=-1)
    cos_t = jnp.concatenate([jnp.ones((ctx, LANES), F32), cos_l], axis=0)
    sin_t = jnp.concatenate([jnp.zeros((ctx, LANES), F32), sin_l], axis=0)
    return cos_t, sin_t


def _pad_lanes(a):
    return jnp.pad(a, [(0, 0)] * (a.ndim - 1) + [(0, LANES - a.shape[-1])])


def kernel(x, c, ctx, c_ctx, w_mod, b_mod, norm1_w, w_in, ssd_conv_w, ssd_conv_b, ssd_a_log,
           ssd_dt_bias, ssd_d, ssd_norm_w, pool_w, pool_scale, sconv_w, ret_decay_logit,
           w_branch, w_gate, b_gate, w_o, norm2_w, ffn_up, ffn_conv_w, ffn_conv_b, ffn_down,
           final_norm_w):
    nb, seq, d = x.shape
    nctx = ctx.shape[1]
    depth = w_mod.shape[0]
    assert d == D_MODEL and seq % GRID_W == 0
    assert nctx % POOL_ROWS == 0 and seq % POOL_ROWS == 0
    bc = nb * nctx
    t = bc + nb * seq
    tm = _pick((1024, 512, 256), bc, seq)
    rt = _pick((512, 256), bc, seq)

    def row_of(i):
        r = i * tm
        return jnp.where(r < bc, 0, 1 + (r - bc) // seq)

    sizes = (SSD_INNER, SSD_CONV_CH, SSD_HEADS, BRANCH_WIDTH, BRANCH_WIDTH, BRANCH_WIDTH,
             BRANCH_WIDTH, RET_QK, RET_QK, BRANCH_WIDTH, BRANCH_WIDTH)
    offs = [0]
    for sz in sizes:
        offs.append(offs[-1] + sz)
    parts = [w_in[:, :, offs[i]:offs[i + 1]] for i in range(len(sizes))]
    w_main = jnp.concatenate(parts[:2] + parts[3:], axis=-1).astype(BF16)
    w_dt = _pad_lanes(parts[2]).astype(BF16)
    assert w_main.shape[-1] == U_COLS
    wg_b = w_gate.astype(BF16)
    wb_b = w_branch.astype(BF16)
    wo_b = w_o.astype(BF16)
    up_b = ffn_up.astype(BF16)
    down_b = ffn_down.astype(BF16)
    poolw_b = pool_w.astype(BF16)

    nrows = 8 * ((1 + nb + 7) // 8)
    cvec = jnp.zeros((nrows, d), F32).at[0].set(c_ctx).at[1:1 + nb].set(c)
    mod = _mod_call(cvec, w_mod, b_mod)
    mod5 = mod.reshape(depth, nrows, 6, 1, d)

    cos_t, sin_t = _rope_tables(nctx, seq)
    rep = lambda a: jnp.broadcast_to(a.reshape(depth, 2 * SSD_HEADS, 1), (depth, 2 * SSD_HEADS, CHUNK))
    dt_bias_p = rep(ssd_dt_bias)
    a_log_p = rep(ssd_a_log)
    decay_p = _pad_lanes(ret_decay_logit)
    d_exp = jnp.repeat(ssd_d, SSD_HEAD_DIM, axis=-1)[:, None, :]

    r3 = lambda a: a[:, None, :]
    xs = jnp.concatenate([ctx.reshape(bc, d), x.reshape(nb * seq, d)], axis=0)
    for l in range(depth):
        u, h, dt = _in_proj_call(xs, r3(norm1_w), mod5, w_main, w_dt, l, 0, 1, tm, row_of)
        xbc = _ssd_conv_call(u, ssd_conv_w, r3(ssd_conv_b), l, rt, bc, nctx, seq)
        y_ssd = _ssd_scan_call(xbc, dt, u, dt_bias_p, a_log_p, d_exp, r3(ssd_norm_w), l, nb, nctx, seq)
        y_pool = _pool_call(u, poolw_b, r3(pool_scale), l, bc, nctx, seq)
        y_sc = _sconv_call(u, sconv_w, l, rt, bc, nctx, seq)
        y_ret = _ret_scan_call(u, cos_t, sin_t, decay_p, l, nb, nctx, seq)
        merged = _merge_call(h, (y_ssd, y_pool, y_sc, y_ret), wg_b, b_gate[:, :, None, :], wb_b, l, tm)
        xs, h2 = _outproj_norm_call(merged, wo_b, xs, mod5, r3(norm2_w), l, 2, 3, 4, rt, bc, seq)
        gact = _ffn_gate_call(h2, up_b, ffn_conv_w, r3(ffn_conv_b), l, tm, bc, nctx, seq)
        xs = _res_call(gact, down_b, xs, mod5, l, 5, tm, row_of, "ffn_down_residual")
    out = _final_norm_call(xs, final_norm_w[None, :], bc, nb * seq, tm)
    return out.reshape(nb, seq, d)
```

```python
import functools
import math

import jax
import jax.numpy as jnp
from jax import lax
from jax.experimental import pallas as pl
from jax.experimental.pallas import tpu as pltpu

F32 = jnp.float32
BF16 = jnp.bfloat16

D_MODEL = 2048
GRID_W = 64
EPS = 1e-6
CHUNK = 128
BRANCH_WIDTH = D_MODEL // 2
SSD_INNER = BRANCH_WIDTH
SSD_HEAD_DIM = 64
SSD_HEADS = SSD_INNER // SSD_HEAD_DIM
SSD_GROUPS = 4
SSD_STATE = 128
SSD_CONV_CH = SSD_INNER + 2 * SSD_GROUPS * SSD_STATE
POOL_WINDOWS = (2, 4, 8, 16)
POOL_GROUP = BRANCH_WIDTH // len(POOL_WINDOWS)
RET_HEADS = 8
RET_V_HEAD = BRANCH_WIDTH // RET_HEADS
RET_K_HEAD = RET_V_HEAD // 2
RET_QK = RET_HEADS * RET_K_HEAD
ROPE_BASE = 10000.0
D_FF = 256 * ((8 * D_MODEL // 3 + 255) // 256)

LANES = 128
HALO = 16
VMEM_LIMIT = 56 * 1024 * 1024

U_Z = 0
U_XBC = 1024
U_POOL = 3072
U_SCB = 4096
U_SCC = 5120
U_SCX = 6144
U_RQ = 7168
U_RK = 7680
U_RV = 8192
U_RG = 9216
U_COLS = 10240


def _params(sem):
    return pltpu.CompilerParams(dimension_semantics=sem, vmem_limit_bytes=VMEM_LIMIT)


def _silu(v):
    return v * jax.nn.sigmoid(v)


def _pick(cands, *ns):
    for c in cands:
        if all(n % c == 0 for n in ns):
            return c
    raise ValueError(f"no tile in {cands} divides {ns}")


def _mod_const(v, n):
    if n & (n - 1) == 0:
        return v & (n - 1)
    return lax.rem(v, n)


def _seq_pos(r0, rows, bc, ctx, seq):
    in_ctx = r0 < bc
    g = r0 + lax.broadcasted_iota(jnp.int32, (rows, 1), 0)
    pos = jnp.where(in_ctx, _mod_const(g, ctx), _mod_const(g - bc, seq))
    n = jnp.where(in_ctx, ctx, seq)
    return pos, n


def _mod_kernel(c_ref, w_ref, b_ref, o_ref):
    a = _silu(c_ref[...]).astype(BF16)
    o_ref[...] = jnp.dot(a, w_ref[...].astype(BF16), preferred_element_type=F32) + b_ref[...]


def _mod_call(cvec, w_mod, b_mod):
    nl, d, n6 = w_mod.shape
    r = cvec.shape[0]
    tn = 1024
    return pl.pallas_call(
        _mod_kernel,
        grid=(nl, n6 // tn),
        in_specs=[pl.BlockSpec((r, d), lambda l, j: (0, 0)),
                  pl.BlockSpec((None, d, tn), lambda l, j: (l, 0, j)),
                  pl.BlockSpec((None, 1, tn), lambda l, j: (l, 0, j))],
        out_specs=pl.BlockSpec((None, r, tn), lambda l, j: (l, 0, j)),
        out_shape=jax.ShapeDtypeStruct((nl, r, n6), F32),
        compiler_params=_params(("parallel", "parallel")),
        name="mod_vectors",
    )(cvec, w_mod, b_mod.reshape(nl, 1, n6))


def _normed(x, nw, sh, sc):
    y = x * lax.rsqrt(jnp.mean(jnp.square(x), axis=-1, keepdims=True) + EPS)
    return (y * nw) * (1.0 + sc) + sh


def _in_proj_kernel(x_ref, nw_ref, sh_ref, sc_ref, w_ref, wdt_ref, u_ref, h_ref, dt_ref):
    @pl.when(pl.program_id(1) == 0)
    def _():
        h = _normed(x_ref[...], nw_ref[...], sh_ref[...], sc_ref[...]).astype(BF16)
        h_ref[...] = h
        dt_ref[...] = jnp.dot(h, wdt_ref[...], preferred_element_type=F32)

    u_ref[...] = jnp.dot(h_ref[...], w_ref[...], preferred_element_type=F32).astype(BF16)


def _mod_spec(l, k, tn, row_fn, col_fn):
    return pl.BlockSpec((None, None, None, 1, tn),
                        lambda *g: (l, row_fn(*g), k, 0, col_fn(*g)))


def _mod_row(tile, bc, seq):
    def row(i):
        r = i * tile
        return jnp.where(r < bc, 0, 1 + (r - bc) // seq)
    return row


def _in_proj_call(x, norm_w, mod5, w, wdt, l, k_sh, k_sc, tm, row_of):
    t, d = x.shape
    n = w.shape[-1]
    tn = 1024
    zero = lambda i, j: 0
    row_of = (lambda f: lambda i, j: f(i))(row_of)
    return pl.pallas_call(
        _in_proj_kernel,
        grid=(t // tm, n // tn),
        in_specs=[pl.BlockSpec((tm, d), lambda i, j: (i, 0)),
                  pl.BlockSpec((None, 1, d), lambda i, j: (l, 0, 0)),
                  _mod_spec(l, k_sh, d, row_of, zero),
                  _mod_spec(l, k_sc, d, row_of, zero),
                  pl.BlockSpec((None, d, tn), lambda i, j: (l, 0, j)),
                  pl.BlockSpec((None, d, LANES), lambda i, j: (l, 0, 0))],
        out_specs=[pl.BlockSpec((tm, tn), lambda i, j: (i, j)),
                   pl.BlockSpec((tm, d), lambda i, j: (i, 0)),
                   pl.BlockSpec((tm, LANES), lambda i, j: (i, 0))],
        out_shape=[jax.ShapeDtypeStruct((t, n), BF16),
                   jax.ShapeDtypeStruct((t, d), BF16),
                   jax.ShapeDtypeStruct((t, LANES), F32)],
        compiler_params=_params(("parallel", "arbitrary")),
        name="in_proj",
    )(x, norm_w, mod5, mod5, w, wdt)


def _merge_kernel(h_ref, y0_ref, y1_ref, y2_ref, y3_ref, wg_ref, bg_ref, wb_ref, o_ref):
    h = h_ref[...]
    acc = None
    for i, y_ref in enumerate((y0_ref, y1_ref, y2_ref, y3_ref)):
        gate = jax.nn.sigmoid(jnp.dot(h, wg_ref[i], preferred_element_type=F32) + bg_ref[i])
        term = gate * jnp.dot(y_ref[...], wb_ref[i], preferred_element_type=F32)
        acc = term if acc is None else acc + term
    o_ref[...] = acc.astype(BF16)


def _merge_call(h, ys, wg, bg, wb, l, tm, row0=0):
    t, d = h.shape
    bw = ys[0].shape[-1]
    tn = 256
    nb = len(ys)
    off = row0 // tm
    t_out = t - row0
    return pl.pallas_call(
        _merge_kernel,
        grid=(t_out // tm, d // tn),
        in_specs=[pl.BlockSpec((tm, d), lambda i, j: (i + off, 0))]
                 + [pl.BlockSpec((tm, bw), lambda i, j: (i + off, 0)) for _ in ys]
                 + [pl.BlockSpec((None, nb, d, tn), lambda i, j: (l, 0, 0, j)),
                    pl.BlockSpec((None, nb, 1, tn), lambda i, j: (l, 0, 0, j)),
                    pl.BlockSpec((None, nb, bw, tn), lambda i, j: (l, 0, 0, j))],
        out_specs=pl.BlockSpec((tm, tn), lambda i, j: (i, j)),
        out_shape=jax.ShapeDtypeStruct((t_out, d), BF16),
        compiler_params=_params(("parallel", "arbitrary")),
        name="branch_merge",
    )(h, *ys, wg, bg, wb)


def _res_kernel(a_ref, w_ref, x_ref, g_ref, o_ref):
    o_ref[...] = x_ref[...] + g_ref[...] * jnp.dot(a_ref[...], w_ref[...],
                                                   preferred_element_type=F32)


def _res_call(a, w, x, mod5, l, k_gate, tm, row_of, name):
    t, kdim = a.shape
    d = x.shape[-1]
    tn = 512
    return pl.pallas_call(
        _res_kernel,
        grid=(t // tm, d // tn),
        in_specs=[pl.BlockSpec((tm, kdim), lambda i, j: (i, 0)),
                  pl.BlockSpec((None, kdim, tn), lambda i, j: (l, 0, j)),
                  pl.BlockSpec((tm, tn), lambda i, j: (i, j)),
                  _mod_spec(l, k_gate, tn, lambda i, j: row_of(i), lambda i, j: j)],
        out_specs=pl.BlockSpec((tm, tn), lambda i, j: (i, j)),
        out_shape=jax.ShapeDtypeStruct((t, d), F32),
        compiler_params=_params(("parallel", "arbitrary")),
        name=name,
    )(a, w, x, mod5)


EPI_ROWS = 128


def _outproj_norm_kernel(a_ref, w_ref, x_ref, g_ref, nw_ref, sh_ref, sc_ref, xo_ref, ho_ref, y0, y1):
    s = pl.program_id(0)
    tm = a_ref.shape[0]

    @pl.when(s == 0)
    def _():
        y1[...] = jnp.zeros_like(y1)

    def step(y_cur, y_prev):
        nslab = tm // EPI_ROWS
        cw = w_ref.shape[1] // nslab
        for k in range(nslab):
            rs = slice(k * EPI_ROWS, (k + 1) * EPI_ROWS)
            cs = slice(k * cw, (k + 1) * cw)
            y_cur[:, cs] = jnp.dot(a_ref[...], w_ref[:, cs], preferred_element_type=F32)
            xn = x_ref[rs, :] + g_ref[...] * y_prev[rs, :]
            xo_ref[rs, :] = xn
            ho_ref[rs, :] = _normed(xn, nw_ref[...], sh_ref[...], sc_ref[...]).astype(BF16)

    @pl.when(s % 2 == 0)
    def _():
        step(y0, y1)

    @pl.when(s % 2 == 1)
    def _():
        step(y1, y0)


def _outproj_norm_call(a, w, x, mod5, norm_w, l, k_gate, k_sh, k_sc, tm, bc, seq):
    t, kdim = a.shape
    d = x.shape[-1]
    ni = t // tm
    row0 = x.shape[0] - t
    assert row0 in (0, bc)
    off = row0 // tm
    row = _mod_row(tm, bc - row0, seq)
    cur = lambda s: jnp.minimum(s, ni - 1)
    prev = lambda s: jnp.maximum(s - 1, 0)
    zero = lambda s: 0
    return pl.pallas_call(
        _outproj_norm_kernel,
        grid=(ni + 1,),
        in_specs=[pl.BlockSpec((tm, kdim), lambda s: (cur(s), 0)),
                  pl.BlockSpec((None, kdim, d), lambda s: (l, 0, 0), pipeline_mode=pl.Buffered(1)),
                  pl.BlockSpec((tm, d), lambda s: (prev(s) + off, 0)),
                  _mod_spec(l, k_gate, d, lambda s: row(prev(s)), zero),
                  pl.BlockSpec((None, 1, d), lambda s: (l, 0, 0)),
                  _mod_spec(l, k_sh, d, lambda s: row(prev(s)), zero),
                  _mod_spec(l, k_sc, d, lambda s: row(prev(s)), zero)],
        out_specs=[pl.BlockSpec((tm, d), lambda s: (prev(s), 0)),
                   pl.BlockSpec((tm, d), lambda s: (prev(s), 0))],
        out_shape=[jax.ShapeDtypeStruct((t, d), F32),
                   jax.ShapeDtypeStruct((t, d), BF16)],
        scratch_shapes=[pltpu.VMEM((tm, d), F32), pltpu.VMEM((tm, d), F32)],
        compiler_params=_params(("arbitrary",)),
        name="out_proj_residual_norm",
    )(a, w, x, mod5, norm_w, mod5, mod5)


FFN_PIECE_ROWS = 64
FFN_SLABS = 4
SEQ_ALIGN = 256


def _ffn_gate_kernel(h_ref, hp_ref, hn_ref, wa_ref, wb_ref, cwa_ref, cwb_ref, ba_ref, bb_ref, o_ref,
                     h_s, a_s, b_s, *, nj, bc, ctx, seq):
    s = pl.program_id(0)
    tm = h_ref.shape[0]

    @pl.when(s == 0)
    def _():
        a_s[1] = jnp.zeros(a_s.shape[1:], F32)
        b_s[1] = jnp.zeros(b_s.shape[1:], F32)

    @pl.when(s % nj == 0)
    def _():
        h_s[0:HALO, :] = hp_ref[...]
        h_s[HALO:HALO + tm, :] = h_ref[...]
        h_s[HALO + tm:, :] = hn_ref[...]

    r0 = (jnp.maximum(s - 1, 0) // nj) * tm
    in_ctx = r0 < bc
    tn = o_ref.shape[1]
    rows_ext = tm + 2 * HALO
    pr = FFN_PIECE_ROWS
    row_first = lax.broadcasted_iota(jnp.int32, (pr, 1), 0) == 0
    row_last = lax.broadcasted_iota(jnp.int32, (pr, 1), 0) == pr - 1

    def seq_start(g):
        return jnp.where(in_ctx, _mod_const(g, ctx), _mod_const(g - bc, seq)) == 0

    def taps(w_ref, bias_ref, cs):
        return [jnp.broadcast_to(w_ref[j:j + 1, cs], (pr, LANES)) for j in range(3)] + \
               [jnp.broadcast_to(bias_ref[:, cs], (pr, LANES))]

    col_slabs = [slice(c, c + LANES) for c in range(0, tn, LANES)]
    taps_a = [taps(cwa_ref, ba_ref, cs) for cs in col_slabs]
    taps_b = [taps(cwb_ref, bb_ref, cs) for cs in col_slabs]

    def conv_piece(src, r, cs, tp, kill_prev, kill_next):
        base = HALO + r
        ext = src[base - 8:base + pr + 8, cs]
        xm1 = pltpu.roll(ext, 1, 0)[8:8 + pr]
        x00 = ext[8:8 + pr]
        xp1 = pltpu.roll(ext, pr + 15, 0)[8:8 + pr]
        if kill_prev is not None:
            xm1 = jnp.where(kill_prev, 0.0, xm1)
        if kill_next is not None:
            xp1 = jnp.where(kill_next, 0.0, xp1)
        return xm1 * tp[0] + x00 * tp[1] + xp1 * tp[2] + tp[3]

    def epilogue(a_prev, b_prev, lo, hi):
        for r in range(lo, hi, pr):
            kill_prev = kill_next = None
            if r % SEQ_ALIGN == 0:
                kill_prev = jnp.logical_and(row_first, seq_start(r0 + r))
            if (r + pr) % SEQ_ALIGN == 0:
                kill_next = jnp.logical_and(row_last, seq_start(r0 + r + pr))
            for ci, cs in enumerate(col_slabs):
                a = conv_piece(a_prev, r, cs, taps_a[ci], kill_prev, kill_next)
                b = conv_piece(b_prev, r, cs, taps_b[ci], kill_prev, kill_next)
                o_ref[r:r + pr, cs] = (_silu(a) * b).astype(BF16)

    def step(a_cur, b_cur, a_prev, b_prev):
        nslab = FFN_SLABS
        for k in range(nslab):
            m_lo = (k * rows_ext // nslab) // HALO * HALO
            m_hi = rows_ext if k == nslab - 1 else ((k + 1) * rows_ext // nslab) // HALO * HALO
            h = h_s[m_lo:m_hi, :]
            a_cur[m_lo:m_hi, :] = jnp.dot(h, wa_ref[...], preferred_element_type=F32)
            b_cur[m_lo:m_hi, :] = jnp.dot(h, wb_ref[...], preferred_element_type=F32)
            epilogue(a_prev, b_prev, k * tm // nslab, (k + 1) * tm // nslab)

    cur = s % 2
    step(a_s.at[cur], b_s.at[cur], a_s.at[1 - cur], b_s.at[1 - cur])


def _ffn_gate_call(h, w, cw, cb, l, tm, bc, ctx, seq):
    t, d = h.shape
    tn = 512
    nj = D_FF // tn
    ni = t // tm
    per = tm // HALO
    last = t // HALO - 1
    ci = lambda s: jnp.minimum(s // nj, ni - 1)
    cj = lambda s: s % nj
    pi = lambda s: jnp.maximum(s - 1, 0) // nj
    pj = lambda s: jnp.maximum(s - 1, 0) % nj
    return pl.pallas_call(
        functools.partial(_ffn_gate_kernel, nj=nj, bc=bc, ctx=ctx, seq=seq),
        grid=(ni * nj + 1,),
        in_specs=[pl.BlockSpec((tm, d), lambda s: (ci(s), 0)),
                  pl.BlockSpec((HALO, d), lambda s: (jnp.maximum(ci(s) * per - 1, 0), 0)),
                  pl.BlockSpec((HALO, d), lambda s: (jnp.minimum((ci(s) + 1) * per, last), 0)),
                  pl.BlockSpec((None, d, tn), lambda s: (l, 0, cj(s))),
                  pl.BlockSpec((None, d, tn), lambda s: (l, 0, nj + cj(s))),
                  pl.BlockSpec((None, 3, tn), lambda s: (l, 0, pj(s))),
                  pl.BlockSpec((None, 3, tn), lambda s: (l, 0, nj + pj(s))),
                  pl.BlockSpec((None, 1, tn), lambda s: (l, 0, pj(s))),
                  pl.BlockSpec((None, 1, tn), lambda s: (l, 0, nj + pj(s)))],
        out_specs=pl.BlockSpec((tm, tn), lambda s: (pi(s), pj(s))),
        out_shape=jax.ShapeDtypeStruct((t, D_FF), BF16),
        scratch_shapes=[pltpu.VMEM((tm + 2 * HALO, d), BF16)]
                       + [pltpu.VMEM((2, tm + 2 * HALO, tn), F32) for _ in range(2)],
        compiler_params=_params(("arbitrary",)),
        name="ffn_up_conv_gate",
    )(h, h, h, w, w, cw, cw, cb, cb)


def _final_norm_kernel(x_ref, w_ref, o_ref):
    x = x_ref[...]
    o_ref[...] = (x * lax.rsqrt(jnp.mean(jnp.square(x), axis=-1, keepdims=True) + EPS)) * w_ref[...]


def _final_norm_call(x, w, row0, rows, tm):
    d = x.shape[-1]
    off = row0 // tm
    return pl.pallas_call(
        _final_norm_kernel,
        grid=(rows // tm,),
        in_specs=[pl.BlockSpec((tm, d), lambda i: (i + off, 0)),
                  pl.BlockSpec((1, d), lambda i: (0, 0))],
        out_specs=pl.BlockSpec((tm, d), lambda i: (i, 0)),
        out_shape=jax.ShapeDtypeStruct((rows, d), F32),
        compiler_params=_params(("parallel",)),
        name="final_norm",
    )(x, w)


def _halo_specs(rt, cols, col_idx, t):
    per = rt // HALO
    last = t // HALO - 1
    return [pl.BlockSpec((rt, cols), lambda i, j: (i, col_idx(j))),
            pl.BlockSpec((HALO, cols), lambda i, j: (jnp.maximum(i * per - 1, 0), col_idx(j))),
            pl.BlockSpec((HALO, cols), lambda i, j: (jnp.minimum((i + 1) * per, last), col_idx(j)))]


def _shift_pm1(x, prev_row, next_row, pos, n):
    rt = x.shape[0]
    row = lax.broadcasted_iota(jnp.int32, (rt, 1), 0)
    xm1 = jnp.where(row == 0, prev_row, pltpu.roll(x, 1, 0))
    xm1 = jnp.where(pos == 0, 0.0, xm1)
    xp1 = jnp.where(row == rt - 1, next_row, pltpu.roll(x, rt - 1, 0))
    xp1 = jnp.where(pos == n - 1, 0.0, xp1)
    return xm1, xp1


def _conv3(x, prev_row, next_row, w, pos, n):
    xm1, xp1 = _shift_pm1(x, prev_row, next_row, pos, n)
    return xm1 * w[0:1] + x * w[1:2] + xp1 * w[2:3]


def _ssd_conv_kernel(x_ref, p_ref, n_ref, w_ref, b_ref, o_ref, *, bc, ctx, seq):
    rt = x_ref.shape[0]
    pos, n = _seq_pos(pl.program_id(0) * rt, rt, bc, ctx, seq)
    x = x_ref[...].astype(F32)
    prev_row = p_ref[...].astype(F32)[HALO - 1:HALO]
    next_row = n_ref[...].astype(F32)[0:1]
    y = _conv3(x, prev_row, next_row, w_ref[...], pos, n) + b_ref[...]
    o_ref[...] = _silu(y).astype(BF16)


def _ssd_conv_call(u, w, b, l, rt, bc, ctx, seq):
    t = u.shape[0]
    cols = 1024
    nj = SSD_CONV_CH // cols
    base = U_XBC // cols
    return pl.pallas_call(
        functools.partial(_ssd_conv_kernel, bc=bc, ctx=ctx, seq=seq),
        grid=(t // rt, nj),
        in_specs=_halo_specs(rt, cols, lambda j: base + j, t)
                 + [pl.BlockSpec((None, 3, cols), lambda i, j: (l, 0, j)),
                    pl.BlockSpec((None, 1, cols), lambda i, j: (l, 0, j))],
        out_specs=pl.BlockSpec((rt, cols), lambda i, j: (i, j)),
        out_shape=jax.ShapeDtypeStruct((t, SSD_CONV_CH), BF16),
        compiler_params=_params(("parallel", "parallel")),
        name="ssd_conv",
    )(u, u, u, w, b)


def _sconv_kernel(b_ref, c_ref, cp_ref, cn_ref, x_ref, xp_ref, xn_ref, w_ref, o_ref, *, bc, ctx, seq):
    rt = b_ref.shape[0]
    pos, n = _seq_pos(pl.program_id(0) * rt, rt, bc, ctx, seq)
    cx = c_ref[...].astype(F32) * x_ref[...].astype(F32)
    prev_row = (cp_ref[...].astype(F32) * xp_ref[...].astype(F32))[HALO - 1:HALO]
    next_row = (cn_ref[...].astype(F32) * xn_ref[...].astype(F32))[0:1]
    y = b_ref[...].astype(F32) * _conv3(cx, prev_row, next_row, w_ref[...], pos, n)
    o_ref[...] = y.astype(BF16)


def _sconv_call(u, w, l, rt, bc, ctx, seq):
    t = u.shape[0]
    cols = BRANCH_WIDTH
    return pl.pallas_call(
        functools.partial(_sconv_kernel, bc=bc, ctx=ctx, seq=seq),
        grid=(t // rt, 1),
        in_specs=[pl.BlockSpec((rt, cols), lambda i, j: (i, U_SCB // cols))]
                 + _halo_specs(rt, cols, lambda j: U_SCC // cols, t)
                 + _halo_specs(rt, cols, lambda j: U_SCX // cols, t)
                 + [pl.BlockSpec((None, 3, cols), lambda i, j: (l, 0, 0))],
        out_specs=pl.BlockSpec((rt, cols), lambda i, j: (i, 0)),
        out_shape=jax.ShapeDtypeStruct((t, cols), BF16),
        compiler_params=_params(("parallel", "parallel")),
        name="short_conv",
    )(u, u, u, u, u, u, u, w)


POOL_ROWS = 256


def _pool_kernel(x_ref, p_ref, n_ref, w_ref, s_ref, o_ref, *, bc, ctx, seq):
    rt = POOL_ROWS
    r0 = pl.program_id(0) * rt
    pos, n = _seq_pos(r0, rt, bc, ctx, seq)
    in_ctx = r0 < bc
    p0 = jnp.where(in_ctx, _mod_const(r0, ctx), _mod_const(r0 - bc, seq))
    has_prev = p0 != 0
    has_next = p0 + rt != n
    ext_rows = rt + 2 * HALO
    for gi, win in enumerate(POOL_WINDOWS):
        cs = slice(gi * POOL_GROUP, (gi + 1) * POOL_GROUP)
        g = x_ref[:, cs].astype(F32)
        prev = jnp.where(has_prev, p_ref[:, cs].astype(F32), 0.0)
        nxt = jnp.where(has_next, n_ref[:, cs].astype(F32), 0.0)
        ext = jnp.concatenate([prev, g, nxt], axis=0)
        s = ext + pltpu.roll(ext, 1, 0)
        half = 1
        while 2 * half < win:
            s = pltpu.roll(s, half, 0) + pltpu.roll(s, ext_rows - half, 0)
            half *= 2
        s = s[HALO:HALO + rt]
        lo = jnp.clip(pos - win // 2, 0, n)
        hi = jnp.clip(pos - win // 2 + win, 0, n)
        mean = s / (hi - lo).astype(F32)
        pooled = (mean - g).astype(BF16)
        y = jnp.dot(pooled, w_ref[gi], preferred_element_type=F32) * s_ref[:, cs]
        o_ref[:, cs] = y.astype(BF16)


def _pool_call(u, w, scale, l, bc, ctx, seq):
    t = u.shape[0]
    cols = BRANCH_WIDTH
    rt = POOL_ROWS
    ng = len(POOL_WINDOWS)
    return pl.pallas_call(
        functools.partial(_pool_kernel, bc=bc, ctx=ctx, seq=seq),
        grid=(t // rt, 1),
        in_specs=_halo_specs(rt, cols, lambda j: U_POOL // cols, t)
                 + [pl.BlockSpec((None, ng, POOL_GROUP, POOL_GROUP), lambda i, j: (l, 0, 0, 0)),
                    pl.BlockSpec((None, 1, cols), lambda i, j: (l, 0, 0))],
        out_specs=pl.BlockSpec((rt, cols), lambda i, j: (i, 0)),
        out_shape=jax.ShapeDtypeStruct((t, cols), BF16),
        compiler_params=_params(("parallel", "parallel")),
        name="multiscale_pool",
    )(u, u, u, w, scale)


def _scan_positions(ncc, nk):
    def kpos(sw, s):
        back = jnp.where(s < ncc, ncc - 1 - s, nk - 1 - (s - ncc))
        return jnp.where(sw == 0, back, s)
    return kpos


def _chunk_row(b, k, ncc, ncl, nb):
    return jnp.where(k < ncc, b * ncc + k, nb * ncc + b * ncl + (k - ncc))


def _tri_masks():
    row = lax.broadcasted_iota(jnp.int32, (CHUNK, CHUNK), 0)
    col = lax.broadcasted_iota(jnp.int32, (CHUNK, CHUNK), 1)
    return row, col, col <= row, col >= row


def _colb(x, h):
    return jnp.broadcast_to(x[:, h:h + 1], (x.shape[0], LANES))


MASKED = -1e30


SCAN_CHUNKS = 2


def _ssd_scan_kernel(xbc_ref, dt_ref, z_ref, bias_ref, alog_ref, dexp_ref, nw_ref, sp_ref, sg_ref, o_ref,
                     hb_all, hf_s, hb_s, y_s, vw_s, pre_s, *, ncc, nk):
    sw = pl.program_id(1)
    s = pl.program_id(2)
    kpos = _scan_positions(ncc, nk)(sw, s)
    row, col, lower, upper = _tri_masks()
    lane_lo = col < SSD_HEAD_DIM
    row_lo1 = lane_lo[0:1]
    nh = SSD_HEADS
    n_pairs = nh // 2
    pairs_per_group = n_pairs // SSD_GROUPS
    gw = SSD_INNER // SSD_GROUPS

    def chunk_rows(c):
        return slice(c * CHUNK, (c + 1) * CHUNK)

    def split2(x):
        hi = x.astype(BF16).astype(F32)
        return hi, x - hi

    def split3(x):
        hi, r = split2(x)
        mid, lo = split2(r)
        return hi, mid, lo

    def to_tokens(parts):
        used = sum(p.shape[0] for p in parts)
        return jnp.concatenate(list(parts) + [jnp.zeros((CHUNK - used, CHUNK), F32)], axis=0).T.astype(BF16)

    def store_state_operand(wt, rc):
        lhs = to_tokens((jnp.zeros((4 * nh, CHUNK), F32),) + split2(wt))
        w_full = jnp.dot(lhs, sp_ref[:, 2 * SSD_INNER:3 * SSD_INNER], preferred_element_type=F32)
        vw_s[...] = (xbc_ref[rc, 0:SSD_INNER].astype(F32) * w_full).astype(BF16)

    def seg_all(cum):
        parts = split3(cum)
        ones = jnp.ones((nh, CHUNK), F32)
        a = to_tokens(parts + (ones, ones, ones))
        neg = jnp.concatenate([-p for p in parts], axis=0)
        tiled = jnp.concatenate([neg] * nh, axis=1)
        rows_b = jnp.where(sg_ref[3 * nh:6 * nh, :] != 0, tiled, 0.0).astype(BF16)
        b = jnp.concatenate([sg_ref[0:3 * nh, :], rows_b,
                             jnp.zeros((CHUNK - 6 * nh, nh * CHUNK), BF16)], axis=0)
        return jnp.dot(a, b, preferred_element_type=F32)

    def pair_scalar(x, p, j):
        return jnp.where(row_lo1, jnp.broadcast_to(x[2 * p:2 * p + 1, j:j + 1], (1, LANES)),
                         jnp.broadcast_to(x[2 * p + 1:2 * p + 2, j:j + 1], (1, LANES)))

    def state_update(h_s, g, dec_rows, rc):
        gs = slice(g * gw, (g + 1) * gw)
        bm = xbc_ref[rc, SSD_INNER + g * SSD_STATE:SSD_INNER + (g + 1) * SSD_STATE]
        st = lax.dot_general(bm, vw_s[:, gs], (((0,), (0,)), ((), ())), preferred_element_type=F32)
        h_s[:, gs] = h_s[:, gs] * jnp.concatenate(dec_rows, axis=1) + st

    @pl.when(sw == 0)
    def _():
        @pl.when(s == 0)
        def _():
            hb_s[...] = jnp.zeros_like(hb_s)

        for c in reversed(range(SCAN_CHUNKS)):
            rc = chunk_rows(c)
            ck = kpos * SCAN_CHUNKS + c
            dtt = dt_ref[rc, :].T[0:nh]
            dt2 = jax.nn.softplus(jnp.concatenate([dtt, dtt], axis=0) + bias_ref[...])
            la2 = dt2 * (-jnp.exp(alog_ref[...]))
            tri2 = jnp.concatenate([upper.astype(F32), lower.astype(F32)], axis=1)
            cum2 = jnp.dot(la2, tri2, precision=lax.Precision.HIGHEST, preferred_element_type=F32)
            cum_f = cum2[0:nh, 0:CHUNK]
            cum_b = cum2[nh:2 * nh, CHUNK:2 * CHUNK]
            pre_s[ck, 0:2 * nh] = dt2
            pre_s[ck, 2 * nh:3 * nh] = cum_f
            pre_s[ck, 3 * nh:4 * nh] = cum_b

            hb_all[ck] = hb_s[...].astype(BF16)
            store_state_operand(jnp.exp(cum_b[:, 0:1] - cum_b) * dt2[nh:2 * nh], rc)
            for g in range(SSD_GROUPS):
                dec_rows = [jnp.exp(pair_scalar(cum_b, g * pairs_per_group + pp, 0))
                            for pp in range(pairs_per_group)]
                state_update(hb_s, g, dec_rows, rc)

    @pl.when(sw == 1)
    def _():
        @pl.when(s == 0)
        def _():
            hf_s[...] = jnp.zeros_like(hf_s)

        for c in range(SCAN_CHUNKS):
            rc = chunk_rows(c)
            ck = kpos * SCAN_CHUNKS + c
            dt2 = pre_s[ck, 0:2 * nh]
            cum_f = pre_s[ck, 2 * nh:3 * nh]
            cum_b = pre_s[ck, 3 * nh:4 * nh]
            hb_in = hb_all[ck]
            wt = jnp.exp(cum_f[:, CHUNK - 1:CHUNK] - cum_f) * dt2[0:nh]
            ef, eb = split2(jnp.exp(cum_f)), split2(jnp.exp(cum_b))
            lhs = to_tokens((ef[0], eb[0], ef[1], eb[1]) + split2(wt))
            spread = jnp.dot(lhs, sp_ref[...], preferred_element_type=F32)
            vw_s[...] = (xbc_ref[rc, 0:SSD_INNER].astype(F32)
                         * spread[:, 2 * SSD_INNER:3 * SSD_INNER]).astype(BF16)
            seg_f = seg_all(cum_f)
            seg_b = seg_all(cum_b)
            for g in range(SSD_GROUPS):
                gs = slice(g * gw, (g + 1) * gw)
                bm = xbc_ref[rc, SSD_INNER + g * SSD_STATE:SSD_INNER + (g + 1) * SSD_STATE]
                cm = xbc_ref[rc, SSD_INNER + SSD_GROUPS * SSD_STATE + g * SSD_STATE:
                             SSD_INNER + SSD_GROUPS * SSD_STATE + (g + 1) * SSD_STATE]
                sc = lax.dot_general(cm, bm, (((1,), (1,)), ((), ())), preferred_element_type=F32)
                hcat = jnp.concatenate([hf_s[:, gs].astype(BF16), hb_in[:, gs]], axis=1)
                gfb = jnp.dot(cm, hcat, preferred_element_type=F32)
                dec_rows = []
                for pp in range(pairs_per_group):
                    p = g * pairs_per_group + pp
                    ps = slice(p * LANES, (p + 1) * LANES)
                    xs_pair = xbc_ref[rc, ps]
                    ms = []
                    for h in (2 * p, 2 * p + 1):
                        hs = slice(h * CHUNK, (h + 1) * CHUNK)
                        dec_f = jnp.exp(jnp.where(lower, seg_f[:, hs], MASKED))
                        dec_b = jnp.exp(jnp.where(upper, seg_b[:, hs], MASKED))
                        m = sc * (dec_f * dt2[h:h + 1] + dec_b * dt2[nh + h:nh + h + 1])
                        ms.append(m.astype(BF16))
                    zero = jnp.zeros_like(xs_pair)
                    x_split = jnp.concatenate([jnp.where(lane_lo, xs_pair, zero),
                                               jnp.where(lane_lo, zero, xs_pair)], axis=0)
                    y_intra = jnp.dot(jnp.concatenate(ms, axis=1), x_split, preferred_element_type=F32)
                    lo = pp * LANES
                    y_inter = (gfb[:, lo:lo + LANES] * spread[:, ps]
                               + gfb[:, gw + lo:gw + lo + LANES]
                               * spread[:, SSD_INNER + p * LANES:SSD_INNER + (p + 1) * LANES])
                    y_s[:, ps] = y_intra + y_inter
                    dec_rows.append(jnp.exp(pair_scalar(cum_f, p, CHUNK - 1)))
                state_update(hf_s, g, dec_rows, rc)

            xs32 = xbc_ref[rc, 0:SSD_INNER].astype(F32)
            y = (y_s[...] + xs32 * dexp_ref[...]) * _silu(z_ref[rc, :].astype(F32))
            y = y * lax.rsqrt(jnp.mean(jnp.square(y), axis=-1, keepdims=True) + EPS)
            o_ref[rc, :] = (y * nw_ref[...]).astype(BF16)


def _ssd_spread_constants():
    nh = SSD_HEADS
    r = jnp.arange(CHUNK)[:, None]
    c = jnp.arange(3 * SSD_INNER)[None, :]
    block_of_row = jnp.array([0, 1, 0, 1, 2, 2, -1, -1])[r // nh]
    sp = (block_of_row == c // SSD_INNER) & (r % nh == (c % SSD_INNER) // SSD_HEAD_DIM)
    c2 = jnp.arange(nh * CHUNK)[None, :]
    sg = (r < 6 * nh) & (r % nh == c2 // CHUNK)
    return sp.astype(BF16), sg.astype(BF16)


def _ssd_scan_call(xbc, dt, u, dt_bias, a_log, d_exp, norm_w, l, nb, ctx, seq):
    t = xbc.shape[0]
    rows = SCAN_CHUNKS * CHUNK
    ncc, ncl = ctx // rows, seq // rows
    nk = ncc + ncl
    kpos = _scan_positions(ncc, nk)

    def in_row(b, sw, s):
        return _chunk_row(b, kpos(sw, s), ncc, ncl, nb)

    def out_row(b, sw, s):
        return _chunk_row(b, jnp.where(sw == 0, 0, s), ncc, ncl, nb)

    return pl.pallas_call(
        functools.partial(_ssd_scan_kernel, ncc=ncc, nk=nk),
        grid=(nb, 2, nk),
        in_specs=[pl.BlockSpec((rows, SSD_CONV_CH), lambda b, sw, s: (in_row(b, sw, s), 0)),
                  pl.BlockSpec((rows, LANES), lambda b, sw, s: (in_row(b, sw, s), 0)),
                  pl.BlockSpec((rows, SSD_INNER), lambda b, sw, s: (in_row(b, sw, s), U_Z // SSD_INNER)),
                  pl.BlockSpec((None, 2 * SSD_HEADS, CHUNK), lambda b, sw, s: (l, 0, 0)),
                  pl.BlockSpec((None, 2 * SSD_HEADS, CHUNK), lambda b, sw, s: (l, 0, 0)),
                  pl.BlockSpec((None, 1, SSD_INNER), lambda b, sw, s: (l, 0, 0)),
                  pl.BlockSpec((None, 1, SSD_INNER), lambda b, sw, s: (l, 0, 0)),
                  pl.BlockSpec((CHUNK, 3 * SSD_INNER), lambda b, sw, s: (0, 0)),
                  pl.BlockSpec((CHUNK, SSD_HEADS * CHUNK), lambda b, sw, s: (0, 0))],
        out_specs=pl.BlockSpec((rows, SSD_INNER), lambda b, sw, s: (out_row(b, sw, s), 0)),
        out_shape=jax.ShapeDtypeStruct((t, SSD_INNER), BF16),
        scratch_shapes=[pltpu.VMEM((nk * SCAN_CHUNKS, SSD_STATE, SSD_INNER), BF16),
                        pltpu.VMEM((SSD_STATE, SSD_INNER), F32),
                        pltpu.VMEM((SSD_STATE, SSD_INNER), F32),
                        pltpu.VMEM((CHUNK, SSD_INNER), F32),
                        pltpu.VMEM((CHUNK, SSD_INNER), BF16),
                        pltpu.VMEM((nk * SCAN_CHUNKS, 4 * SSD_HEADS, CHUNK), F32)],
        compiler_params=_params(("parallel", "arbitrary", "arbitrary")),
        name="ssd_scan",
    )(xbc, dt, u, dt_bias, a_log, d_exp, norm_w, *_ssd_spread_constants())


def _ret_scan_kernel(q_ref, k_ref, v_ref, g_ref, cos_ref, sin_ref, dl_ref, o_ref,
                     hb_all, hf_s, hb_s, dec_s, tab_s, *, ncc, nk):
    sw = pl.program_id(1)
    s = pl.program_id(2)
    kpos = _scan_positions(ncc, nk)(sw, s)
    row, col, lower, upper = _tri_masks()
    lane_lo = col < RET_K_HEAD
    la = jax.nn.log_sigmoid(dl_ref[...])
    rowf = row.astype(F32)
    diff = (row - col).astype(F32)
    n_pairs = RET_HEADS // 2

    first16 = (col & 31) < 16

    def chunk_rows(c):
        return slice(c * CHUNK, (c + 1) * CHUNK)

    def rope(x, rc):
        swapped = jnp.where(first16, pltpu.roll(x, LANES - 16, 1), pltpu.roll(x, 16, 1))
        return x * cos_ref[rc, :] + swapped * sin_ref[rc, :]

    def head_scalar(d, h):
        return la[d:d + 1, h:h + 1]

    def pair_k(p, rc):
        return rope(k_ref[rc, p * LANES:(p + 1) * LANES].astype(F32), rc).astype(BF16)

    def state_update(h_s, d, p, kp, w_tab, rc):
        h0, h1 = 2 * p, 2 * p + 1
        v_of = lambda h: v_ref[rc, h * RET_V_HEAD:(h + 1) * RET_V_HEAD]
        vw = jnp.concatenate([(v_of(h0).astype(F32) * tab_s[w_tab, h0]).astype(BF16),
                              (v_of(h1).astype(F32) * tab_s[w_tab, h1]).astype(BF16)], axis=1)
        st = lax.dot_general(kp, vw, (((0,), (0,)), ((), ())), preferred_element_type=F32)
        top = row < RET_K_HEAD
        acc = jnp.where(top, st[:, 0:RET_V_HEAD], st[:, RET_V_HEAD:2 * RET_V_HEAD])
        dec = jnp.where(top, jnp.exp(float(CHUNK) * head_scalar(d, h0)),
                        jnp.exp(float(CHUNK) * head_scalar(d, h1)))
        rs = slice(p * LANES, (p + 1) * LANES)
        h_s[rs, :] = h_s[rs, :] * dec + acc

    @pl.when(jnp.logical_and(sw == 0, s == 0))
    def _():
        hb_s[...] = jnp.zeros_like(hb_s)
        for h in range(RET_HEADS):
            laf = head_scalar(0, h)
            lab = head_scalar(1, h)
            dec_s[h] = (jnp.exp(jnp.where(lower, diff * laf, MASKED))
                        + jnp.exp(jnp.where(upper, -diff * lab, MASKED)))
            tab_s[0, h] = jnp.exp((rowf + 1.0) * laf)
            tab_s[1, h] = jnp.exp((float(CHUNK) - rowf) * lab)
            tab_s[2, h] = jnp.exp((float(CHUNK - 1) - rowf) * laf)
            tab_s[3, h] = jnp.exp(rowf * lab)

    @pl.when(sw == 0)
    def _():
        for c in reversed(range(SCAN_CHUNKS)):
            rc = chunk_rows(c)
            hb_all[kpos * SCAN_CHUNKS + c] = hb_s[...].astype(BF16)
            for p in range(n_pairs):
                state_update(hb_s, 1, p, pair_k(p, rc), 3, rc)

    @pl.when(sw == 1)
    def _():
        @pl.when(s == 0)
        def _():
            hf_s[...] = jnp.zeros_like(hf_s)

        for c in range(SCAN_CHUNKS):
            rc = chunk_rows(c)
            hb_in = hb_all[kpos * SCAN_CHUNKS + c]
            for p in range(n_pairs):
                h0, h1 = 2 * p, 2 * p + 1
                v_of = lambda h: v_ref[rc, h * RET_V_HEAD:(h + 1) * RET_V_HEAD]
                qp = rope(q_ref[rc, p * LANES:(p + 1) * LANES].astype(F32) * (RET_K_HEAD ** -0.5), rc)
                kp = pair_k(p, rc)
                rs = slice(p * LANES, (p + 1) * LANES)
                qcat = jnp.concatenate([jnp.where(lane_lo, qp, 0.0), jnp.where(lane_lo, 0.0, qp)],
                                       axis=0).astype(BF16)
                sc = lax.dot_general(qcat, kp, (((1,), (1,)), ((), ())), preferred_element_type=F32)
                hcat = jnp.concatenate([hf_s[rs, :].astype(BF16), hb_in[rs, :]], axis=1)
                gfb = jnp.dot(qcat, hcat, preferred_element_type=F32)
                m = jnp.concatenate([(sc[0:CHUNK] * dec_s[h0]).astype(BF16),
                                     (sc[CHUNK:2 * CHUNK] * dec_s[h1]).astype(BF16)], axis=1)
                zero = jnp.zeros((CHUNK, RET_V_HEAD), BF16)
                v_blk = jnp.concatenate([jnp.concatenate([v_of(h0), zero], axis=1),
                                         jnp.concatenate([zero, v_of(h1)], axis=1)], axis=0)
                y_intra = jnp.dot(m, v_blk, preferred_element_type=F32)
                for i, h in enumerate((h0, h1)):
                    vs = slice(h * RET_V_HEAD, (h + 1) * RET_V_HEAD)
                    g_h = gfb[i * CHUNK:(i + 1) * CHUNK]
                    y = (y_intra[:, i * RET_V_HEAD:(i + 1) * RET_V_HEAD]
                         + g_h[:, 0:RET_V_HEAD] * tab_s[0, h]
                         + g_h[:, RET_V_HEAD:2 * RET_V_HEAD] * tab_s[1, h])
                    mu = jnp.mean(y, axis=-1, keepdims=True)
                    yc = y - mu
                    var = jnp.mean(jnp.square(yc), axis=-1, keepdims=True)
                    yn = yc * lax.rsqrt(var + EPS)
                    o_ref[rc, vs] = (_silu(g_ref[rc, vs].astype(F32)) * yn).astype(BF16)
                state_update(hf_s, 0, p, kp, 2, rc)


def _ret_scan_call(u, cos_t, sin_t, decay_logit, l, nb, ctx, seq):
    t = u.shape[0]
    rows = SCAN_CHUNKS * CHUNK
    ncc, ncl = ctx // rows, seq // rows
    nk = ncc + ncl
    kpos = _scan_positions(ncc, nk)

    def in_row(b, sw, s):
        return _chunk_row(b, kpos(sw, s), ncc, ncl, nb)

    def out_row(b, sw, s):
        return _chunk_row(b, jnp.where(sw == 0, 0, s), ncc, ncl, nb)

    bw = BRANCH_WIDTH
    return pl.pallas_call(
        functools.partial(_ret_scan_kernel, ncc=ncc, nk=nk),
        grid=(nb, 2, nk),
        in_specs=[pl.BlockSpec((rows, RET_QK), lambda b, sw, s: (in_row(b, sw, s), U_RQ // RET_QK)),
                  pl.BlockSpec((rows, RET_QK), lambda b, sw, s: (in_row(b, sw, s), U_RK // RET_QK)),
                  pl.BlockSpec((rows, bw), lambda b, sw, s: (in_row(b, sw, s), U_RV // bw)),
                  pl.BlockSpec((rows, bw), lambda b, sw, s: (in_row(b, sw, s), U_RG // bw)),
                  pl.BlockSpec((rows, LANES), lambda b, sw, s: (kpos(sw, s), 0)),
                  pl.BlockSpec((rows, LANES), lambda b, sw, s: (kpos(sw, s), 0)),
                  pl.BlockSpec((None, 2, LANES), lambda b, sw, s: (l, 0, 0))],
        out_specs=pl.BlockSpec((rows, bw), lambda b, sw, s: (out_row(b, sw, s), 0)),
        out_shape=jax.ShapeDtypeStruct((t, bw), BF16),
        scratch_shapes=[pltpu.VMEM((nk * SCAN_CHUNKS, RET_QK, RET_V_HEAD), BF16),
                        pltpu.VMEM((RET_QK, RET_V_HEAD), F32),
                        pltpu.VMEM((RET_QK, RET_V_HEAD), F32),
                        pltpu.VMEM((RET_HEADS, CHUNK, CHUNK), F32),
                        pltpu.VMEM((4, RET_HEADS, CHUNK, LANES), F32)],
        compiler_params=_params(("parallel", "arbitrary", "arbitrary")),
        name="retention_scan",
    )(u, u, u, u, cos_t, sin_t, decay_logit)


def _rope_tables(ctx, seq):
    quarter = RET_K_HEAD // 4
    t = jnp.arange(seq)
    rowp = (t // GRID_W).astype(F32)
    colp = (t % GRID_W).astype(F32)
    inv = ROPE_BASE ** (-jnp.arange(quarter, dtype=F32) / quarter)
    ang_r = rowp[:, None] * inv[None, :]
    ang_c = colp[:, None] * inv[None, :]

    def blocks(ang):
        c, s = jnp.cos(ang), jnp.sin(ang)
        return jnp.concatenate([c, c], axis=-1), jnp.concatenate([-s, s], axis=-1)

    cr, sr = blocks(ang_r)
    cc, sc = blocks(ang_c)
    cos_h = jnp.concatenate([cr, cc], axis=-1)
    sin_h = jnp.concatenate([sr, sc], axis=-1)
    cos_l = jnp.concatenate([cos_h, cos_h], axis=-1)
    sin_l = jnp.concatenate([sin_h, sin_h], axis=-1)
    cos_t = jnp.concatenate([jnp.ones((ctx, LANES), F32), cos_l], axis=0)
    sin_t = jnp.concatenate([jnp.zeros((ctx, LANES), F32), sin_l], axis=0)
    return cos_t, sin_t


def _pad_lanes(a):
    return jnp.pad(a, [(0, 0)] * (a.ndim - 1) + [(0, LANES - a.shape[-1])])


def kernel(x, c, ctx, c_ctx, w_mod, b_mod, norm1_w, w_in, ssd_conv_w, ssd_conv_b, ssd_a_log,
           ssd_dt_bias, ssd_d, ssd_norm_w, pool_w, pool_scale, sconv_w, ret_decay_logit,
           w_branch, w_gate, b_gate, w_o, norm2_w, ffn_up, ffn_conv_w, ffn_conv_b, ffn_down,
           final_norm_w):
    nb, seq, d = x.shape
    nctx = ctx.shape[1]
    depth = w_mod.shape[0]
    assert d == D_MODEL and seq % GRID_W == 0
    assert nctx % POOL_ROWS == 0 and seq % POOL_ROWS == 0
    bc = nb * nctx
    t = bc + nb * seq
    tm = _pick((1024, 512, 256), bc, seq)
    rt = _pick((512, 256), bc, seq)

    def row_of(i):
        r = i * tm
        return jnp.where(r < bc, 0, 1 + (r - bc) // seq)

    sizes = (SSD_INNER, SSD_CONV_CH, SSD_HEADS, BRANCH_WIDTH, BRANCH_WIDTH, BRANCH_WIDTH,
             BRANCH_WIDTH, RET_QK, RET_QK, BRANCH_WIDTH, BRANCH_WIDTH)
    offs = [0]
    for sz in sizes:
        offs.append(offs[-1] + sz)
    parts = [w_in[:, :, offs[i]:offs[i + 1]] for i in range(len(sizes))]
    w_main = jnp.concatenate(parts[:2] + parts[3:], axis=-1).astype(BF16)
    w_dt = _pad_lanes(parts[2]).astype(BF16)
    assert w_main.shape[-1] == U_COLS
    wg_b = w_gate.astype(BF16)
    wb_b = w_branch.astype(BF16)
    wo_b = w_o.astype(BF16)
    up_b = ffn_up.astype(BF16)
    down_b = ffn_down.astype(BF16)
    poolw_b = pool_w.astype(BF16)

    nrows = 8 * ((1 + nb + 7) // 8)
    cvec = jnp.zeros((nrows, d), F32).at[0].set(c_ctx).at[1:1 + nb].set(c)
    mod = _mod_call(cvec, w_mod, b_mod)
    mod5 = mod.reshape(depth, nrows, 6, 1, d)

    cos_t, sin_t = _rope_tables(nctx, seq)
    rep = lambda a: jnp.broadcast_to(a.reshape(depth, 2 * SSD_HEADS, 1), (depth, 2 * SSD_HEADS, CHUNK))
    dt_bias_p = rep(ssd_dt_bias)
    a_log_p = rep(ssd_a_log)
    decay_p = _pad_lanes(ret_decay_logit)
    d_exp = jnp.repeat(ssd_d, SSD_HEAD_DIM, axis=-1)[:, None, :]

    r3 = lambda a: a[:, None, :]
    xs = jnp.concatenate([ctx.reshape(bc, d), x.reshape(nb * seq, d)], axis=0)
    for l in range(depth):
        u, h, dt = _in_proj_call(xs, r3(norm1_w), mod5, w_main, w_dt, l, 0, 1, tm, row_of)
        xbc = _ssd_conv_call(u, ssd_conv_w, r3(ssd_conv_b), l, rt, bc, nctx, seq)
        y_ssd = _ssd_scan_call(xbc, dt, u, dt_bias_p, a_log_p, d_exp, r3(ssd_norm_w), l, nb, nctx, seq)
        y_pool = _pool_call(u, poolw_b, r3(pool_scale), l, bc, nctx, seq)
        y_sc = _sconv_call(u, sconv_w, l, rt, bc, nctx, seq)
        y_ret = _ret_scan_call(u, cos_t, sin_t, decay_p, l, nb, nctx, seq)
        lat = l == depth - 1
        bc_l = 0 if lat else bc
        row_l = _mod_row(tm, bc_l, seq)
        merged = _merge_call(h, (y_ssd, y_pool, y_sc, y_ret), wg_b, b_gate[:, :, None, :], wb_b, l, tm,
                             row0=bc if lat else 0)
        xs, h2 = _outproj_norm_call(merged, wo_b, xs, mod5, r3(norm2_w), l, 2, 3, 4, rt, bc, seq)
        gact = _ffn_gate_call(h2, up_b, ffn_conv_w, r3(ffn_conv_b), l, tm, bc_l, nctx, seq)
        xs = _res_call(gact, down_b, xs, mod5, l, 5, tm, row_l, "ffn_down_residual")
    out = _final_norm_call(xs, final_norm_w[None, :], 0, nb * seq, tm)
    return out.reshape(nb, seq, d)
```

```python
import functools
import math

import jax
import jax.numpy as jnp
from jax import lax
from jax.experimental import pallas as pl
from jax.experimental.pallas import tpu as pltpu

F32 = jnp.float32
BF16 = jnp.bfloat16

D_MODEL = 2048
GRID_W = 64
EPS = 1e-6
CHUNK = 128
BRANCH_WIDTH = D_MODEL // 2
SSD_INNER = BRANCH_WIDTH
SSD_HEAD_DIM = 64
SSD_HEADS = SSD_INNER // SSD_HEAD_DIM
SSD_GROUPS = 4
SSD_STATE = 128
SSD_CONV_CH = SSD_INNER + 2 * SSD_GROUPS * SSD_STATE
POOL_WINDOWS = (2, 4, 8, 16)
POOL_GROUP = BRANCH_WIDTH // len(POOL_WINDOWS)
RET_HEADS = 8
RET_V_HEAD = BRANCH_WIDTH // RET_HEADS
RET_K_HEAD = RET_V_HEAD // 2
RET_QK = RET_HEADS * RET_K_HEAD
ROPE_BASE = 10000.0
D_FF = 256 * ((8 * D_MODEL // 3 + 255) // 256)

LANES = 128
HALO = 16
VMEM_LIMIT = 56 * 1024 * 1024

U_Z = 0
U_XBC = 1024
U_POOL = 3072
U_SCB = 4096
U_SCC = 5120
U_SCX = 6144
U_RQ = 7168
U_RK = 7680
U_RV = 8192
U_RG = 9216
U_COLS = 10240


def _params(sem):
    return pltpu.CompilerParams(dimension_semantics=sem, vmem_limit_bytes=VMEM_LIMIT)


def _silu(v):
    return v * jax.nn.sigmoid(v)


def _pick(cands, *ns):
    for c in cands:
        if all(n % c == 0 for n in ns):
            return c
    raise ValueError(f"no tile in {cands} divides {ns}")


def _mod_const(v, n):
    if n & (n - 1) == 0:
        return v & (n - 1)
    return lax.rem(v, n)


def _seq_pos(r0, rows, bc, ctx, seq):
    in_ctx = r0 < bc
    g = r0 + lax.broadcasted_iota(jnp.int32, (rows, 1), 0)
    pos = jnp.where(in_ctx, _mod_const(g, ctx), _mod_const(g - bc, seq))
    n = jnp.where(in_ctx, ctx, seq)
    return pos, n


def _mod_kernel(c_ref, w_ref, b_ref, o_ref):
    a = _silu(c_ref[...]).astype(BF16)
    o_ref[...] = jnp.dot(a, w_ref[...].astype(BF16), preferred_element_type=F32) + b_ref[...]


def _mod_call(cvec, w_mod, b_mod):
    nl, d, n6 = w_mod.shape
    r = cvec.shape[0]
    tn = 1024
    return pl.pallas_call(
        _mod_kernel,
        grid=(nl, n6 // tn),
        in_specs=[pl.BlockSpec((r, d), lambda l, j: (0, 0)),
                  pl.BlockSpec((None, d, tn), lambda l, j: (l, 0, j)),
                  pl.BlockSpec((None, 1, tn), lambda l, j: (l, 0, j))],
        out_specs=pl.BlockSpec((None, r, tn), lambda l, j: (l, 0, j)),
        out_shape=jax.ShapeDtypeStruct((nl, r, n6), F32),
        compiler_params=_params(("parallel", "parallel")),
        name="mod_vectors",
    )(cvec, w_mod, b_mod.reshape(nl, 1, n6))


def _normed(x, nw, sh, sc):
    y = x * lax.rsqrt(jnp.mean(jnp.square(x), axis=-1, keepdims=True) + EPS)
    return (y * nw) * (1.0 + sc) + sh


def _in_proj_kernel(x_ref, nw_ref, sh_ref, sc_ref, w_ref, wdt_ref, u_ref, h_ref, dt_ref):
    @pl.when(pl.program_id(1) == 0)
    def _():
        h = _normed(x_ref[...], nw_ref[...], sh_ref[...], sc_ref[...]).astype(BF16)
        h_ref[...] = h
        dt_ref[...] = jnp.dot(h, wdt_ref[...], preferred_element_type=F32)

    u_ref[...] = jnp.dot(h_ref[...], w_ref[...], preferred_element_type=F32).astype(BF16)


def _mod_spec(l, k, tn, row_fn, col_fn):
    return pl.BlockSpec((None, None, None, 1, tn),
                        lambda *g: (l, row_fn(*g), k, 0, col_fn(*g)))


def _mod_row(tile, bc, seq):
    def row(i):
        r = i * tile
        return jnp.where(r < bc, 0, 1 + (r - bc) // seq)
    return row


def _in_proj_call(x, norm_w, mod5, w, wdt, l, k_sh, k_sc, tm, row_of):
    t, d = x.shape
    n = w.shape[-1]
    tn = 1024
    zero = lambda i, j: 0
    row_of = (lambda f: lambda i, j: f(i))(row_of)
    return pl.pallas_call(
        _in_proj_kernel,
        grid=(t // tm, n // tn),
        in_specs=[pl.BlockSpec((tm, d), lambda i, j: (i, 0)),
                  pl.BlockSpec((None, 1, d), lambda i, j: (l, 0, 0)),
                  _mod_spec(l, k_sh, d, row_of, zero),
                  _mod_spec(l, k_sc, d, row_of, zero),
                  pl.BlockSpec((None, d, tn), lambda i, j: (l, 0, j)),
                  pl.BlockSpec((None, d, LANES), lambda i, j: (l, 0, 0))],
        out_specs=[pl.BlockSpec((tm, tn), lambda i, j: (i, j)),
                   pl.BlockSpec((tm, d), lambda i, j: (i, 0)),
                   pl.BlockSpec((tm, LANES), lambda i, j: (i, 0))],
        out_shape=[jax.ShapeDtypeStruct((t, n), BF16),
                   jax.ShapeDtypeStruct((t, d), BF16),
                   jax.ShapeDtypeStruct((t, LANES), F32)],
        compiler_params=_params(("parallel", "arbitrary")),
        name="in_proj",
    )(x, norm_w, mod5, mod5, w, wdt)


def _merge_kernel(h_ref, y0_ref, y1_ref, y2_ref, y3_ref, wg_ref, bg_ref, wb_ref, o_ref):
    h = h_ref[...]
    acc = None
    for i, y_ref in enumerate((y0_ref, y1_ref, y2_ref, y3_ref)):
        gate = jax.nn.sigmoid(jnp.dot(h, wg_ref[i], preferred_element_type=F32) + bg_ref[i])
        term = gate * jnp.dot(y_ref[...], wb_ref[i], preferred_element_type=F32)
        acc = term if acc is None else acc + term
    o_ref[...] = acc.astype(BF16)


def _merge_call(h, ys, wg, bg, wb, l, tm, row0=0):
    t, d = h.shape
    bw = ys[0].shape[-1]
    tn = 256
    nb = len(ys)
    off = row0 // tm
    t_out = t - row0
    return pl.pallas_call(
        _merge_kernel,
        grid=(t_out // tm, d // tn),
        in_specs=[pl.BlockSpec((tm, d), lambda i, j: (i + off, 0))]
                 + [pl.BlockSpec((tm, bw), lambda i, j: (i + off, 0)) for _ in ys]
                 + [pl.BlockSpec((None, nb, d, tn), lambda i, j: (l, 0, 0, j)),
                    pl.BlockSpec((None, nb, 1, tn), lambda i, j: (l, 0, 0, j)),
                    pl.BlockSpec((None, nb, bw, tn), lambda i, j: (l, 0, 0, j))],
        out_specs=pl.BlockSpec((tm, tn), lambda i, j: (i, j)),
        out_shape=jax.ShapeDtypeStruct((t_out, d), BF16),
        compiler_params=_params(("parallel", "arbitrary")),
        name="branch_merge",
    )(h, *ys, wg, bg, wb)


def _res_kernel(a_ref, w_ref, x_ref, g_ref, o_ref):
    o_ref[...] = x_ref[...] + g_ref[...] * jnp.dot(a_ref[...], w_ref[...],
                                                   preferred_element_type=F32)


def _res_call(a, w, x, mod5, l, k_gate, tm, row_of, name):
    t, kdim = a.shape
    d = x.shape[-1]
    tn = 512
    return pl.pallas_call(
        _res_kernel,
        grid=(t // tm, d // tn),
        in_specs=[pl.BlockSpec((tm, kdim), lambda i, j: (i, 0)),
                  pl.BlockSpec((None, kdim, tn), lambda i, j: (l, 0, j)),
                  pl.BlockSpec((tm, tn), lambda i, j: (i, j)),
                  _mod_spec(l, k_gate, tn, lambda i, j: row_of(i), lambda i, j: j)],
        out_specs=pl.BlockSpec((tm, tn), lambda i, j: (i, j)),
        out_shape=jax.ShapeDtypeStruct((t, d), F32),
        compiler_params=_params(("parallel", "arbitrary")),
        name=name,
    )(a, w, x, mod5)


EPI_ROWS = 128


def _outproj_norm_kernel(a_ref, w_ref, x_ref, g_ref, nw_ref, sh_ref, sc_ref, xo_ref, ho_ref, y0, y1):
    s = pl.program_id(0)
    tm = a_ref.shape[0]

    @pl.when(s == 0)
    def _():
        y1[...] = jnp.zeros_like(y1)

    def step(y_cur, y_prev):
        nslab = tm // EPI_ROWS
        cw = w_ref.shape[1] // nslab
        for k in range(nslab):
            rs = slice(k * EPI_ROWS, (k + 1) * EPI_ROWS)
            cs = slice(k * cw, (k + 1) * cw)
            y_cur[:, cs] = jnp.dot(a_ref[...], w_ref[:, cs], preferred_element_type=F32)
            xn = x_ref[rs, :] + g_ref[...] * y_prev[rs, :]
            xo_ref[rs, :] = xn
            ho_ref[rs, :] = _normed(xn, nw_ref[...], sh_ref[...], sc_ref[...]).astype(BF16)

    @pl.when(s % 2 == 0)
    def _():
        step(y0, y1)

    @pl.when(s % 2 == 1)
    def _():
        step(y1, y0)


def _outproj_norm_call(a, w, x, mod5, norm_w, l, k_gate, k_sh, k_sc, tm, bc, seq):
    t, kdim = a.shape
    d = x.shape[-1]
    ni = t // tm
    row0 = x.shape[0] - t
    assert row0 in (0, bc)
    off = row0 // tm
    row = _mod_row(tm, bc - row0, seq)
    cur = lambda s: jnp.minimum(s, ni - 1)
    prev = lambda s: jnp.maximum(s - 1, 0)
    zero = lambda s: 0
    return pl.pallas_call(
        _outproj_norm_kernel,
        grid=(ni + 1,),
        in_specs=[pl.BlockSpec((tm, kdim), lambda s: (cur(s), 0)),
                  pl.BlockSpec((None, kdim, d), lambda s: (l, 0, 0), pipeline_mode=pl.Buffered(1)),
                  pl.BlockSpec((tm, d), lambda s: (prev(s) + off, 0)),
                  _mod_spec(l, k_gate, d, lambda s: row(prev(s)), zero),
                  pl.BlockSpec((None, 1, d), lambda s: (l, 0, 0)),
                  _mod_spec(l, k_sh, d, lambda s: row(prev(s)), zero),
                  _mod_spec(l, k_sc, d, lambda s: row(prev(s)), zero)],
        out_specs=[pl.BlockSpec((tm, d), lambda s: (prev(s), 0)),
                   pl.BlockSpec((tm, d), lambda s: (prev(s), 0))],
        out_shape=[jax.ShapeDtypeStruct((t, d), F32),
                   jax.ShapeDtypeStruct((t, d), BF16)],
        scratch_shapes=[pltpu.VMEM((tm, d), F32), pltpu.VMEM((tm, d), F32)],
        compiler_params=_params(("arbitrary",)),
        name="out_proj_residual_norm",
    )(a, w, x, mod5, norm_w, mod5, mod5)


FFN_PIECE_ROWS = 64
FFN_SLABS = 4
SEQ_ALIGN = 256


def _ffn_gate_kernel(h_ref, hp_ref, hn_ref, wa_ref, wb_ref, cwa_ref, cwb_ref, ba_ref, bb_ref, o_ref,
                     h_s, a_s, b_s, *, nj, bc, ctx, seq):
    s = pl.program_id(0)
    tm = h_ref.shape[0]

    @pl.when(s == 0)
    def _():
        a_s[1] = jnp.zeros(a_s.shape[1:], F32)
        b_s[1] = jnp.zeros(b_s.shape[1:], F32)

    @pl.when(s % nj == 0)
    def _():
        h_s[0:HALO, :] = hp_ref[...]
        h_s[HALO:HALO + tm, :] = h_ref[...]
        h_s[HALO + tm:, :] = hn_ref[...]

    r0 = (jnp.maximum(s - 1, 0) // nj) * tm
    in_ctx = r0 < bc
    tn = o_ref.shape[1]
    rows_ext = tm + 2 * HALO
    pr = FFN_PIECE_ROWS
    row_first = lax.broadcasted_iota(jnp.int32, (pr, 1), 0) == 0
    row_last = lax.broadcasted_iota(jnp.int32, (pr, 1), 0) == pr - 1

    def seq_start(g):
        return jnp.where(in_ctx, _mod_const(g, ctx), _mod_const(g - bc, seq)) == 0

    def taps(w_ref, bias_ref, cs):
        return [jnp.broadcast_to(w_ref[j:j + 1, cs], (pr, LANES)) for j in range(3)] + \
               [jnp.broadcast_to(bias_ref[:, cs], (pr, LANES))]

    col_slabs = [slice(c, c + LANES) for c in range(0, tn, LANES)]
    taps_a = [taps(cwa_ref, ba_ref, cs) for cs in col_slabs]
    taps_b = [taps(cwb_ref, bb_ref, cs) for cs in col_slabs]

    def conv_piece(src, r, cs, tp, kill_prev, kill_next):
        base = HALO + r
        ext = src[base - 8:base + pr + 8, cs]
        xm1 = pltpu.roll(ext, 1, 0)[8:8 + pr]
        x00 = ext[8:8 + pr]
        xp1 = pltpu.roll(ext, pr + 15, 0)[8:8 + pr]
        if kill_prev is not None:
            xm1 = jnp.where(kill_prev, 0.0, xm1)
        if kill_next is not None:
            xp1 = jnp.where(kill_next, 0.0, xp1)
        return xm1 * tp[0] + x00 * tp[1] + xp1 * tp[2] + tp[3]

    def epilogue(a_prev, b_prev, lo, hi):
        for r in range(lo, hi, pr):
            kill_prev = kill_next = None
            if r % SEQ_ALIGN == 0:
                kill_prev = jnp.logical_and(row_first, seq_start(r0 + r))
            if (r + pr) % SEQ_ALIGN == 0:
                kill_next = jnp.logical_and(row_last, seq_start(r0 + r + pr))
            for ci, cs in enumerate(col_slabs):
                a = conv_piece(a_prev, r, cs, taps_a[ci], kill_prev, kill_next)
                b = conv_piece(b_prev, r, cs, taps_b[ci], kill_prev, kill_next)
                o_ref[r:r + pr, cs] = (_silu(a) * b).astype(BF16)

    def step(a_cur, b_cur, a_prev, b_prev):
        nslab = FFN_SLABS
        for k in range(nslab):
            m_lo = (k * rows_ext // nslab) // HALO * HALO
            m_hi = rows_ext if k == nslab - 1 else ((k + 1) * rows_ext // nslab) // HALO * HALO
            h = h_s[m_lo:m_hi, :]
            a_cur[m_lo:m_hi, :] = jnp.dot(h, wa_ref[...], preferred_element_type=F32)
            b_cur[m_lo:m_hi, :] = jnp.dot(h, wb_ref[...], preferred_element_type=F32)
            epilogue(a_prev, b_prev, k * tm // nslab, (k + 1) * tm // nslab)

    cur = s % 2
    step(a_s.at[cur], b_s.at[cur], a_s.at[1 - cur], b_s.at[1 - cur])


def _ffn_gate_call(h, w, cw, cb, l, tm, bc, ctx, seq):
    t, d = h.shape
    tn = 512
    nj = D_FF // tn
    ni = t // tm
    per = tm // HALO
    last = t // HALO - 1
    ci = lambda s: jnp.minimum(s // nj, ni - 1)
    cj = lambda s: s % nj
    pi = lambda s: jnp.maximum(s - 1, 0) // nj
    pj = lambda s: jnp.maximum(s - 1, 0) % nj
    return pl.pallas_call(
        functools.partial(_ffn_gate_kernel, nj=nj, bc=bc, ctx=ctx, seq=seq),
        grid=(ni * nj + 1,),
        in_specs=[pl.BlockSpec((tm, d), lambda s: (ci(s), 0)),
                  pl.BlockSpec((HALO, d), lambda s: (jnp.maximum(ci(s) * per - 1, 0), 0)),
                  pl.BlockSpec((HALO, d), lambda s: (jnp.minimum((ci(s) + 1) * per, last), 0)),
                  pl.BlockSpec((None, d, tn), lambda s: (l, 0, cj(s))),
                  pl.BlockSpec((None, d, tn), lambda s: (l, 0, nj + cj(s))),
                  pl.BlockSpec((None, 3, tn), lambda s: (l, 0, pj(s))),
                  pl.BlockSpec((None, 3, tn), lambda s: (l, 0, nj + pj(s))),
                  pl.BlockSpec((None, 1, tn), lambda s: (l, 0, pj(s))),
                  pl.BlockSpec((None, 1, tn), lambda s: (l, 0, nj + pj(s)))],
        out_specs=pl.BlockSpec((tm, tn), lambda s: (pi(s), pj(s))),
        out_shape=jax.ShapeDtypeStruct((t, D_FF), BF16),
        scratch_shapes=[pltpu.VMEM((tm + 2 * HALO, d), BF16)]
                       + [pltpu.VMEM((2, tm + 2 * HALO, tn), F32) for _ in range(2)],
        compiler_params=_params(("arbitrary",)),
        name="ffn_up_conv_gate",
    )(h, h, h, w, w, cw, cw, cb, cb)


def _final_norm_kernel(x_ref, w_ref, o_ref):
    x = x_ref[...]
    o_ref[...] = (x * lax.rsqrt(jnp.mean(jnp.square(x), axis=-1, keepdims=True) + EPS)) * w_ref[...]


def _final_norm_call(x, w, row0, rows, tm):
    d = x.shape[-1]
    off = row0 // tm
    return pl.pallas_call(
        _final_norm_kernel,
        grid=(rows // tm,),
        in_specs=[pl.BlockSpec((tm, d), lambda i: (i + off, 0)),
                  pl.BlockSpec((1, d), lambda i: (0, 0))],
        out_specs=pl.BlockSpec((tm, d), lambda i: (i, 0)),
        out_shape=jax.ShapeDtypeStruct((rows, d), F32),
        compiler_params=_params(("parallel",)),
        name="final_norm",
    )(x, w)


def _halo_specs(rt, cols, col_idx, t):
    per = rt // HALO
    last = t // HALO - 1
    return [pl.BlockSpec((rt, cols), lambda i, j: (i, col_idx(j))),
            pl.BlockSpec((HALO, cols), lambda i, j: (jnp.maximum(i * per - 1, 0), col_idx(j))),
            pl.BlockSpec((HALO, cols), lambda i, j: (jnp.minimum((i + 1) * per, last), col_idx(j)))]


def _shift_pm1(x, prev_row, next_row, pos, n):
    rt = x.shape[0]
    row = lax.broadcasted_iota(jnp.int32, (rt, 1), 0)
    xm1 = jnp.where(row == 0, prev_row, pltpu.roll(x, 1, 0))
    xm1 = jnp.where(pos == 0, 0.0, xm1)
    xp1 = jnp.where(row == rt - 1, next_row, pltpu.roll(x, rt - 1, 0))
    xp1 = jnp.where(pos == n - 1, 0.0, xp1)
    return xm1, xp1


def _conv3(x, prev_row, next_row, w, pos, n):
    xm1, xp1 = _shift_pm1(x, prev_row, next_row, pos, n)
    return xm1 * w[0:1] + x * w[1:2] + xp1 * w[2:3]


def _ssd_conv_kernel(x_ref, p_ref, n_ref, w_ref, b_ref, o_ref, *, bc, ctx, seq):
    rt = x_ref.shape[0]
    pos, n = _seq_pos(pl.program_id(0) * rt, rt, bc, ctx, seq)
    x = x_ref[...].astype(F32)
    prev_row = p_ref[...].astype(F32)[HALO - 1:HALO]
    next_row = n_ref[...].astype(F32)[0:1]
    y = _conv3(x, prev_row, next_row, w_ref[...], pos, n) + b_ref[...]
    o_ref[...] = _silu(y).astype(BF16)


def _ssd_conv_call(u, w, b, l, rt, bc, ctx, seq):
    t = u.shape[0]
    cols = 1024
    nj = SSD_CONV_CH // cols
    base = U_XBC // cols
    return pl.pallas_call(
        functools.partial(_ssd_conv_kernel, bc=bc, ctx=ctx, seq=seq),
        grid=(t // rt, nj),
        in_specs=_halo_specs(rt, cols, lambda j: base + j, t)
                 + [pl.BlockSpec((None, 3, cols), lambda i, j: (l, 0, j)),
                    pl.BlockSpec((None, 1, cols), lambda i, j: (l, 0, j))],
        out_specs=pl.BlockSpec((rt, cols), lambda i, j: (i, j)),
        out_shape=jax.ShapeDtypeStruct((t, SSD_CONV_CH), BF16),
        compiler_params=_params(("parallel", "parallel")),
        name="ssd_conv",
    )(u, u, u, w, b)


def _sconv_kernel(b_ref, c_ref, cp_ref, cn_ref, x_ref, xp_ref, xn_ref, w_ref, o_ref, *, bc, ctx, seq):
    rt = b_ref.shape[0]
    pos, n = _seq_pos(pl.program_id(0) * rt, rt, bc, ctx, seq)
    cx = c_ref[...].astype(F32) * x_ref[...].astype(F32)
    prev_row = (cp_ref[...].astype(F32) * xp_ref[...].astype(F32))[HALO - 1:HALO]
    next_row = (cn_ref[...].astype(F32) * xn_ref[...].astype(F32))[0:1]
    y = b_ref[...].astype(F32) * _conv3(cx, prev_row, next_row, w_ref[...], pos, n)
    o_ref[...] = y.astype(BF16)


def _sconv_call(u, w, l, rt, bc, ctx, seq):
    t = u.shape[0]
    cols = BRANCH_WIDTH
    return pl.pallas_call(
        functools.partial(_sconv_kernel, bc=bc, ctx=ctx, seq=seq),
        grid=(t // rt, 1),
        in_specs=[pl.BlockSpec((rt, cols), lambda i, j: (i, U_SCB // cols))]
                 + _halo_specs(rt, cols, lambda j: U_SCC // cols, t)
                 + _halo_specs(rt, cols, lambda j: U_SCX // cols, t)
                 + [pl.BlockSpec((None, 3, cols), lambda i, j: (l, 0, 0))],
        out_specs=pl.BlockSpec((rt, cols), lambda i, j: (i, 0)),
        out_shape=jax.ShapeDtypeStruct((t, cols), BF16),
        compiler_params=_params(("parallel", "parallel")),
        name="short_conv",
    )(u, u, u, u, u, u, u, w)


POOL_ROWS = 256
POOL_EXT = 384


def _pool_kernel(x_ref, p_ref, n_ref, band_ref, cnt_ref, w_ref, s_ref, o_ref, *, bc, ctx, seq):
    rt = POOL_ROWS
    r0 = pl.program_id(0) * rt
    in_ctx = r0 < bc
    p0 = jnp.where(in_ctx, _mod_const(r0, ctx), _mod_const(r0 - bc, seq))
    n = jnp.where(in_ctx, ctx, seq)
    has_prev = p0 != 0
    has_next = p0 + rt != n
    zero = jnp.zeros((HALO, POOL_GROUP), BF16)
    pad = jnp.zeros((POOL_EXT - rt - 2 * HALO, POOL_GROUP), BF16)
    case = jnp.where(has_prev, 0, 2) + jnp.where(has_next, 0, 1)
    groups = [slice(gi * POOL_GROUP, (gi + 1) * POOL_GROUP) for gi in range(len(POOL_WINDOWS))]
    sums = []
    for gi, cs in enumerate(groups):
        ext = jnp.concatenate([jnp.where(has_prev, p_ref[:, cs], zero), x_ref[:, cs],
                               jnp.where(has_next, n_ref[:, cs], zero), pad], axis=0)
        sums.append(jnp.dot(band_ref[gi], ext, preferred_element_type=F32))
    for gi, cs in enumerate(groups):
        cnt = cnt_ref[gi, case]
        mean = sums[gi] / jnp.concatenate([cnt] * (POOL_GROUP // LANES), axis=1)
        pooled = (mean - x_ref[:, cs].astype(F32)).astype(BF16)
        y = jnp.dot(pooled, w_ref[gi], preferred_element_type=F32) * s_ref[:, cs]
        o_ref[:, cs] = y.astype(BF16)


def _pool_bands():
    t = jnp.arange(POOL_ROWS)[:, None]
    s = jnp.arange(POOL_EXT)[None, :] - HALO
    return jnp.stack([((s >= t - w // 2) & (s < t - w // 2 + w)) for w in POOL_WINDOWS]).astype(BF16)


def _pool_counts():
    t = jnp.arange(POOL_ROWS)
    tabs = []
    for w in POOL_WINDOWS:
        before = jnp.maximum(w // 2 - t, 0)
        after = jnp.maximum(t - w // 2 + w - POOL_ROWS, 0)
        cases = [w - (before if c >= 2 else 0) - (after if c % 2 else 0) for c in range(4)]
        tabs.append(jnp.stack([jnp.broadcast_to(c, (POOL_ROWS,)) for c in cases]))
    tab = jnp.stack(tabs).astype(F32)
    return jnp.broadcast_to(tab[..., None], tab.shape + (LANES,))


def _pool_call(u, w, scale, l, bc, ctx, seq):
    t = u.shape[0]
    cols = BRANCH_WIDTH
    rt = POOL_ROWS
    ng = len(POOL_WINDOWS)
    return pl.pallas_call(
        functools.partial(_pool_kernel, bc=bc, ctx=ctx, seq=seq),
        grid=(t // rt, 1),
        in_specs=_halo_specs(rt, cols, lambda j: U_POOL // cols, t)
                 + [pl.BlockSpec((ng, rt, POOL_EXT), lambda i, j: (0, 0, 0)),
                    pl.BlockSpec((ng, 4, rt, LANES), lambda i, j: (0, 0, 0, 0)),
                    pl.BlockSpec((None, ng, POOL_GROUP, POOL_GROUP), lambda i, j: (l, 0, 0, 0)),
                    pl.BlockSpec((None, 1, cols), lambda i, j: (l, 0, 0))],
        out_specs=pl.BlockSpec((rt, cols), lambda i, j: (i, 0)),
        out_shape=jax.ShapeDtypeStruct((t, cols), BF16),
        compiler_params=_params(("parallel", "parallel")),
        name="multiscale_pool",
    )(u, u, u, _pool_bands(), _pool_counts(), w, scale)


def _scan_positions(ncc, nk):
    def kpos(sw, s):
        back = jnp.where(s < ncc, ncc - 1 - s, nk - 1 - (s - ncc))
        return jnp.where(sw == 0, back, s)
    return kpos


def _chunk_row(b, k, ncc, ncl, nb):
    return jnp.where(k < ncc, b * ncc + k, nb * ncc + b * ncl + (k - ncc))


def _tri_masks():
    row = lax.broadcasted_iota(jnp.int32, (CHUNK, CHUNK), 0)
    col = lax.broadcasted_iota(jnp.int32, (CHUNK, CHUNK), 1)
    return row, col, col <= row, col >= row


def _colb(x, h):
    return jnp.broadcast_to(x[:, h:h + 1], (x.shape[0], LANES))


MASKED = -1e30


SCAN_CHUNKS = 2


def _ssd_scan_kernel(xbc_ref, dt_ref, z_ref, bias_ref, alog_ref, dexp_ref, nw_ref, sp_ref, sg_ref, o_ref,
                     hb_all, hf_s, hb_s, y_s, vw_s, pre_s, *, ncc, nk):
    sw = pl.program_id(1)
    s = pl.program_id(2)
    kpos = _scan_positions(ncc, nk)(sw, s)
    row, col, lower, upper = _tri_masks()
    lane_lo = col < SSD_HEAD_DIM
    row_lo1 = lane_lo[0:1]
    nh = SSD_HEADS
    n_pairs = nh // 2
    pairs_per_group = n_pairs // SSD_GROUPS
    gw = SSD_INNER // SSD_GROUPS

    def chunk_rows(c):
        return slice(c * CHUNK, (c + 1) * CHUNK)

    def split2(x):
        hi = x.astype(BF16).astype(F32)
        return hi, x - hi

    def split3(x):
        hi, r = split2(x)
        mid, lo = split2(r)
        return hi, mid, lo

    def to_tokens(parts):
        used = sum(p.shape[0] for p in parts)
        return jnp.concatenate(list(parts) + [jnp.zeros((CHUNK - used, CHUNK), F32)], axis=0).T.astype(BF16)

    def store_state_operand(wt, rc):
        lhs = to_tokens((jnp.zeros((4 * nh, CHUNK), F32),) + split2(wt))
        w_full = jnp.dot(lhs, sp_ref[:, 2 * SSD_INNER:3 * SSD_INNER], preferred_element_type=F32)
        vw_s[...] = (xbc_ref[rc, 0:SSD_INNER].astype(F32) * w_full).astype(BF16)

    def seg_all(cum):
        parts = split3(cum)
        ones = jnp.ones((nh, CHUNK), F32)
        a = to_tokens(parts + (ones, ones, ones))
        neg = jnp.concatenate([-p for p in parts], axis=0)
        tiled = jnp.concatenate([neg] * nh, axis=1)
        rows_b = jnp.where(sg_ref[3 * nh:6 * nh, :] != 0, tiled, 0.0).astype(BF16)
        b = jnp.concatenate([sg_ref[0:3 * nh, :], rows_b,
                             jnp.zeros((CHUNK - 6 * nh, nh * CHUNK), BF16)], axis=0)
        return jnp.dot(a, b, preferred_element_type=F32)

    def pair_scalar(x, p, j):
        return jnp.where(row_lo1, jnp.broadcast_to(x[2 * p:2 * p + 1, j:j + 1], (1, LANES)),
                         jnp.broadcast_to(x[2 * p + 1:2 * p + 2, j:j + 1], (1, LANES)))

    def state_update(h_s, g, dec_rows, rc):
        gs = slice(g * gw, (g + 1) * gw)
        bm = xbc_ref[rc, SSD_INNER + g * SSD_STATE:SSD_INNER + (g + 1) * SSD_STATE]
        st = lax.dot_general(bm, vw_s[:, gs], (((0,), (0,)), ((), ())), preferred_element_type=F32)
        h_s[:, gs] = h_s[:, gs] * jnp.concatenate(dec_rows, axis=1) + st

    @pl.when(sw == 0)
    def _():
        @pl.when(s == 0)
        def _():
            hb_s[...] = jnp.zeros_like(hb_s)

        for c in reversed(range(SCAN_CHUNKS)):
            rc = chunk_rows(c)
            ck = kpos * SCAN_CHUNKS + c
            dtt = dt_ref[rc, :].T[0:nh]
            dt2 = jax.nn.softplus(jnp.concatenate([dtt, dtt], axis=0) + bias_ref[...])
            la2 = dt2 * (-jnp.exp(alog_ref[...]))
            tri2 = jnp.concatenate([upper.astype(F32), lower.astype(F32)], axis=1)
            cum2 = jnp.dot(la2, tri2, precision=lax.Precision.HIGHEST, preferred_element_type=F32)
            cum_f = cum2[0:nh, 0:CHUNK]
            cum_b = cum2[nh:2 * nh, CHUNK:2 * CHUNK]
            pre_s[ck, 0:2 * nh] = dt2
            pre_s[ck, 2 * nh:3 * nh] = cum_f
            pre_s[ck, 3 * nh:4 * nh] = cum_b

            hb_all[ck] = hb_s[...].astype(BF16)
            store_state_operand(jnp.exp(cum_b[:, 0:1] - cum_b) * dt2[nh:2 * nh], rc)
            for g in range(SSD_GROUPS):
                dec_rows = [jnp.exp(pair_scalar(cum_b, g * pairs_per_group + pp, 0))
                            for pp in range(pairs_per_group)]
                state_update(hb_s, g, dec_rows, rc)

    @pl.when(sw == 1)
    def _():
        @pl.when(s == 0)
        def _():
            hf_s[...] = jnp.zeros_like(hf_s)

        for c in range(SCAN_CHUNKS):
            rc = chunk_rows(c)
            ck = kpos * SCAN_CHUNKS + c
            dt2 = pre_s[ck, 0:2 * nh]
            cum_f = pre_s[ck, 2 * nh:3 * nh]
            cum_b = pre_s[ck, 3 * nh:4 * nh]
            hb_in = hb_all[ck]
            wt = jnp.exp(cum_f[:, CHUNK - 1:CHUNK] - cum_f) * dt2[0:nh]
            ef, eb = split2(jnp.exp(cum_f)), split2(jnp.exp(cum_b))
            lhs = to_tokens((ef[0], eb[0], ef[1], eb[1]) + split2(wt))
            spread = jnp.dot(lhs, sp_ref[...], preferred_element_type=F32)
            vw_s[...] = (xbc_ref[rc, 0:SSD_INNER].astype(F32)
                         * spread[:, 2 * SSD_INNER:3 * SSD_INNER]).astype(BF16)
            seg_f = seg_all(cum_f)
            seg_b = seg_all(cum_b)
            for g in range(SSD_GROUPS):
                gs = slice(g * gw, (g + 1) * gw)
                bm = xbc_ref[rc, SSD_INNER + g * SSD_STATE:SSD_INNER + (g + 1) * SSD_STATE]
                cm = xbc_ref[rc, SSD_INNER + SSD_GROUPS * SSD_STATE + g * SSD_STATE:
                             SSD_INNER + SSD_GROUPS * SSD_STATE + (g + 1) * SSD_STATE]
                sc = lax.dot_general(cm, bm, (((1,), (1,)), ((), ())), preferred_element_type=F32)
                hcat = jnp.concatenate([hf_s[:, gs].astype(BF16), hb_in[:, gs]], axis=1)
                gfb = jnp.dot(cm, hcat, preferred_element_type=F32)
                dec_rows = []
                for pp in range(pairs_per_group):
                    p = g * pairs_per_group + pp
                    ps = slice(p * LANES, (p + 1) * LANES)
                    xs_pair = xbc_ref[rc, ps]
                    ms = []
                    for h in (2 * p, 2 * p + 1):
                        hs = slice(h * CHUNK, (h + 1) * CHUNK)
                        dec_f = jnp.exp(jnp.where(lower, seg_f[:, hs], MASKED))
                        dec_b = jnp.exp(jnp.where(upper, seg_b[:, hs], MASKED))
                        m = sc * (dec_f * dt2[h:h + 1] + dec_b * dt2[nh + h:nh + h + 1])
                        ms.append(m.astype(BF16))
                    zero = jnp.zeros_like(xs_pair)
                    x_split = jnp.concatenate([jnp.where(lane_lo, xs_pair, zero),
                                               jnp.where(lane_lo, zero, xs_pair)], axis=0)
                    y_intra = jnp.dot(jnp.concatenate(ms, axis=1), x_split, preferred_element_type=F32)
                    lo = pp * LANES
                    y_inter = (gfb[:, lo:lo + LANES] * spread[:, ps]
                               + gfb[:, gw + lo:gw + lo + LANES]
                               * spread[:, SSD_INNER + p * LANES:SSD_INNER + (p + 1) * LANES])
                    y_s[:, ps] = y_intra + y_inter
                    dec_rows.append(jnp.exp(pair_scalar(cum_f, p, CHUNK - 1)))
                state_update(hf_s, g, dec_rows, rc)

            xs32 = xbc_ref[rc, 0:SSD_INNER].astype(F32)
            y = (y_s[...] + xs32 * dexp_ref[...]) * _silu(z_ref[rc, :].astype(F32))
            y = y * lax.rsqrt(jnp.mean(jnp.square(y), axis=-1, keepdims=True) + EPS)
            o_ref[rc, :] = (y * nw_ref[...]).astype(BF16)


def _ssd_spread_constants():
    nh = SSD_HEADS
    r = jnp.arange(CHUNK)[:, None]
    c = jnp.arange(3 * SSD_INNER)[None, :]
    block_of_row = jnp.array([0, 1, 0, 1, 2, 2, -1, -1])[r // nh]
    sp = (block_of_row == c // SSD_INNER) & (r % nh == (c % SSD_INNER) // SSD_HEAD_DIM)
    c2 = jnp.arange(nh * CHUNK)[None, :]
    sg = (r < 6 * nh) & (r % nh == c2 // CHUNK)
    return sp.astype(BF16), sg.astype(BF16)


def _ssd_scan_call(xbc, dt, u, dt_bias, a_log, d_exp, norm_w, l, nb, ctx, seq):
    t = xbc.shape[0]
    rows = SCAN_CHUNKS * CHUNK
    ncc, ncl = ctx // rows, seq // rows
    nk = ncc + ncl
    kpos = _scan_positions(ncc, nk)

    def in_row(b, sw, s):
        return _chunk_row(b, kpos(sw, s), ncc, ncl, nb)

    def out_row(b, sw, s):
        return _chunk_row(b, jnp.where(sw == 0, 0, s), ncc, ncl, nb)

    return pl.pallas_call(
        functools.partial(_ssd_scan_kernel, ncc=ncc, nk=nk),
        grid=(nb, 2, nk),
        in_specs=[pl.BlockSpec((rows, SSD_CONV_CH), lambda b, sw, s: (in_row(b, sw, s), 0)),
                  pl.BlockSpec((rows, LANES), lambda b, sw, s: (in_row(b, sw, s), 0)),
                  pl.BlockSpec((rows, SSD_INNER), lambda b, sw, s: (in_row(b, sw, s), U_Z // SSD_INNER)),
                  pl.BlockSpec((None, 2 * SSD_HEADS, CHUNK), lambda b, sw, s: (l, 0, 0)),
                  pl.BlockSpec((None, 2 * SSD_HEADS, CHUNK), lambda b, sw, s: (l, 0, 0)),
                  pl.BlockSpec((None, 1, SSD_INNER), lambda b, sw, s: (l, 0, 0)),
                  pl.BlockSpec((None, 1, SSD_INNER), lambda b, sw, s: (l, 0, 0)),
                  pl.BlockSpec((CHUNK, 3 * SSD_INNER), lambda b, sw, s: (0, 0)),
                  pl.BlockSpec((CHUNK, SSD_HEADS * CHUNK), lambda b, sw, s: (0, 0))],
        out_specs=pl.BlockSpec((rows, SSD_INNER), lambda b, sw, s: (out_row(b, sw, s), 0)),
        out_shape=jax.ShapeDtypeStruct((t, SSD_INNER), BF16),
        scratch_shapes=[pltpu.VMEM((nk * SCAN_CHUNKS, SSD_STATE, SSD_INNER), BF16),
                        pltpu.VMEM((SSD_STATE, SSD_INNER), F32),
                        pltpu.VMEM((SSD_STATE, SSD_INNER), F32),
                        pltpu.VMEM((CHUNK, SSD_INNER), F32),
                        pltpu.VMEM((CHUNK, SSD_INNER), BF16),
                        pltpu.VMEM((nk * SCAN_CHUNKS, 4 * SSD_HEADS, CHUNK), F32)],
        compiler_params=_params(("parallel", "arbitrary", "arbitrary")),
        name="ssd_scan",
    )(xbc, dt, u, dt_bias, a_log, d_exp, norm_w, *_ssd_spread_constants())


def _ret_scan_kernel(q_ref, k_ref, v_ref, g_ref, cos_ref, sin_ref, dl_ref, o_ref,
                     hb_all, hf_s, hb_s, dec_s, tab_s, *, ncc, nk):
    sw = pl.program_id(1)
    s = pl.program_id(2)
    kpos = _scan_positions(ncc, nk)(sw, s)
    row, col, lower, upper = _tri_masks()
    lane_lo = col < RET_K_HEAD
    la = jax.nn.log_sigmoid(dl_ref[...])
    rowf = row.astype(F32)
    diff = (row - col).astype(F32)
    n_pairs = RET_HEADS // 2

    first16 = (col & 31) < 16

    def chunk_rows(c):
        return slice(c * CHUNK, (c + 1) * CHUNK)

    def rope(x, rc):
        swapped = jnp.where(first16, pltpu.roll(x, LANES - 16, 1), pltpu.roll(x, 16, 1))
        return x * cos_ref[rc, :] + swapped * sin_ref[rc, :]

    def head_scalar(d, h):
        return la[d:d + 1, h:h + 1]

    def pair_k(p, rc):
        return rope(k_ref[rc, p * LANES:(p + 1) * LANES].astype(F32), rc).astype(BF16)

    def state_update(h_s, d, p, kp, w_tab, rc):
        h0, h1 = 2 * p, 2 * p + 1
        v_of = lambda h: v_ref[rc, h * RET_V_HEAD:(h + 1) * RET_V_HEAD]
        vw = jnp.concatenate([(v_of(h0).astype(F32) * tab_s[w_tab, h0]).astype(BF16),
                              (v_of(h1).astype(F32) * tab_s[w_tab, h1]).astype(BF16)], axis=1)
        st = lax.dot_general(kp, vw, (((0,), (0,)), ((), ())), preferred_element_type=F32)
        top = row < RET_K_HEAD
        acc = jnp.where(top, st[:, 0:RET_V_HEAD], st[:, RET_V_HEAD:2 * RET_V_HEAD])
        dec = jnp.where(top, jnp.exp(float(CHUNK) * head_scalar(d, h0)),
                        jnp.exp(float(CHUNK) * head_scalar(d, h1)))
        rs = slice(p * LANES, (p + 1) * LANES)
        h_s[rs, :] = h_s[rs, :] * dec + acc

    @pl.when(jnp.logical_and(sw == 0, s == 0))
    def _():
        hb_s[...] = jnp.zeros_like(hb_s)
        for h in range(RET_HEADS):
            laf = head_scalar(0, h)
            lab = head_scalar(1, h)
            dec_s[h] = (jnp.exp(jnp.where(lower, diff * laf, MASKED))
                        + jnp.exp(jnp.where(upper, -diff * lab, MASKED)))
            tab_s[0, h] = jnp.exp((rowf + 1.0) * laf)
            tab_s[1, h] = jnp.exp((float(CHUNK) - rowf) * lab)
            tab_s[2, h] = jnp.exp((float(CHUNK - 1) - rowf) * laf)
            tab_s[3, h] = jnp.exp(rowf * lab)

    @pl.when(sw == 0)
    def _():
        for c in reversed(range(SCAN_CHUNKS)):
            rc = chunk_rows(c)
            hb_all[kpos * SCAN_CHUNKS + c] = hb_s[...].astype(BF16)
            for p in range(n_pairs):
                state_update(hb_s, 1, p, pair_k(p, rc), 3, rc)

    @pl.when(sw == 1)
    def _():
        @pl.when(s == 0)
        def _():
            hf_s[...] = jnp.zeros_like(hf_s)

        for c in range(SCAN_CHUNKS):
            rc = chunk_rows(c)
            hb_in = hb_all[kpos * SCAN_CHUNKS + c]
            for p in range(n_pairs):
                h0, h1 = 2 * p, 2 * p + 1
                v_of = lambda h: v_ref[rc, h * RET_V_HEAD:(h + 1) * RET_V_HEAD]
                qp = rope(q_ref[rc, p * LANES:(p + 1) * LANES].astype(F32) * (RET_K_HEAD ** -0.5), rc)
                kp = pair_k(p, rc)
                rs = slice(p * LANES, (p + 1) * LANES)
                qcat = jnp.concatenate([jnp.where(lane_lo, qp, 0.0), jnp.where(lane_lo, 0.0, qp)],
                                       axis=0).astype(BF16)
                sc = lax.dot_general(qcat, kp, (((1,), (1,)), ((), ())), preferred_element_type=F32)
                hcat = jnp.concatenate([hf_s[rs, :].astype(BF16), hb_in[rs, :]], axis=1)
                gfb = jnp.dot(qcat, hcat, preferred_element_type=F32)
                m = jnp.concatenate([(sc[0:CHUNK] * dec_s[h0]).astype(BF16),
                                     (sc[CHUNK:2 * CHUNK] * dec_s[h1]).astype(BF16)], axis=1)
                zero = jnp.zeros((CHUNK, RET_V_HEAD), BF16)
                v_blk = jnp.concatenate([jnp.concatenate([v_of(h0), zero], axis=1),
                                         jnp.concatenate([zero, v_of(h1)], axis=1)], axis=0)
                y_intra = jnp.dot(m, v_blk, preferred_element_type=F32)
                for i, h in enumerate((h0, h1)):
                    vs = slice(h * RET_V_HEAD, (h + 1) * RET_V_HEAD)
                    g_h = gfb[i * CHUNK:(i + 1) * CHUNK]
                    y = (y_intra[:, i * RET_V_HEAD:(i + 1) * RET_V_HEAD]
                         + g_h[:, 0:RET_V_HEAD] * tab_s[0, h]
                         + g_h[:, RET_V_HEAD:2 * RET_V_HEAD] * tab_s[1, h])
                    mu = jnp.mean(y, axis=-1, keepdims=True)
                    yc = y - mu
                    var = jnp.mean(jnp.square(yc), axis=-1, keepdims=True)
                    yn = yc * lax.rsqrt(var + EPS)
                    o_ref[rc, vs] = (_silu(g_ref[rc, vs].astype(F32)) * yn).astype(BF16)
                state_update(hf_s, 0, p, kp, 2, rc)


def _ret_scan_call(u, cos_t, sin_t, decay_logit, l, nb, ctx, seq):
    t = u.shape[0]
    rows = SCAN_CHUNKS * CHUNK
    ncc, ncl = ctx // rows, seq // rows
    nk = ncc + ncl
    kpos = _scan_positions(ncc, nk)

    def in_row(b, sw, s):
        return _chunk_row(b, kpos(sw, s), ncc, ncl, nb)

    def out_row(b, sw, s):
        return _chunk_row(b, jnp.where(sw == 0, 0, s), ncc, ncl, nb)

    bw = BRANCH_WIDTH
    return pl.pallas_call(
        functools.partial(_ret_scan_kernel, ncc=ncc, nk=nk),
        grid=(nb, 2, nk),
        in_specs=[pl.BlockSpec((rows, RET_QK), lambda b, sw, s: (in_row(b, sw, s), U_RQ // RET_QK)),
                  pl.BlockSpec((rows, RET_QK), lambda b, sw, s: (in_row(b, sw, s), U_RK // RET_QK)),
                  pl.BlockSpec((rows, bw), lambda b, sw, s: (in_row(b, sw, s), U_RV // bw)),
                  pl.BlockSpec((rows, bw), lambda b, sw, s: (in_row(b, sw, s), U_RG // bw)),
                  pl.BlockSpec((rows, LANES), lambda b, sw, s: (kpos(sw, s), 0)),
                  pl.BlockSpec((rows, LANES), lambda b, sw, s: (kpos(sw, s), 0)),
                  pl.BlockSpec((None, 2, LANES), lambda b, sw, s: (l, 0, 0))],
        out_specs=pl.BlockSpec((rows, bw), lambda b, sw, s: (out_row(b, sw, s), 0)),
        out_shape=jax.ShapeDtypeStruct((t, bw), BF16),
        scratch_shapes=[pltpu.VMEM((nk * SCAN_CHUNKS, RET_QK, RET_V_HEAD), BF16),
                        pltpu.VMEM((RET_QK, RET_V_HEAD), F32),
                        pltpu.VMEM((RET_QK, RET_V_HEAD), F32),
                        pltpu.VMEM((RET_HEADS, CHUNK, CHUNK), F32),
                        pltpu.VMEM((4, RET_HEADS, CHUNK, LANES), F32)],
        compiler_params=_params(("parallel", "arbitrary", "arbitrary")),
        name="retention_scan",
    )(u, u, u, u, cos_t, sin_t, decay_logit)


def _rope_tables(ctx, seq):
    quarter = RET_K_HEAD // 4
    t = jnp.arange(seq)
    rowp = (t // GRID_W).astype(F32)
    colp = (t % GRID_W).astype(F32)
    inv = ROPE_BASE ** (-jnp.arange(quarter, dtype=F32) / quarter)
    ang_r = rowp[:, None] * inv[None, :]
    ang_c = colp[:, None] * inv[None, :]

    def blocks(ang):
        c, s = jnp.cos(ang), jnp.sin(ang)
        return jnp.concatenate([c, c], axis=-1), jnp.concatenate([-s, s], axis=-1)

    cr, sr = blocks(ang_r)
    cc, sc = blocks(ang_c)
    cos_h = jnp.concatenate([cr, cc], axis=-1)
    sin_h = jnp.concatenate([sr, sc], axis=-1)
    cos_l = jnp.concatenate([cos_h, cos_h], axis=-1)
    sin_l = jnp.concatenate([sin_h, sin_h], axis=-1)
    cos_t = jnp.concatenate([jnp.ones((ctx, LANES), F32), cos_l], axis=0)
    sin_t = jnp.concatenate([jnp.zeros((ctx, LANES), F32), sin_l], axis=0)
    return cos_t, sin_t


def _pad_lanes(a):
    return jnp.pad(a, [(0, 0)] * (a.ndim - 1) + [(0, LANES - a.shape[-1])])


def kernel(x, c, ctx, c_ctx, w_mod, b_mod, norm1_w, w_in, ssd_conv_w, ssd_conv_b, ssd_a_log,
           ssd_dt_bias, ssd_d, ssd_norm_w, pool_w, pool_scale, sconv_w, ret_decay_logit,
           w_branch, w_gate, b_gate, w_o, norm2_w, ffn_up, ffn_conv_w, ffn_conv_b, ffn_down,
           final_norm_w):
    nb, seq, d = x.shape
    nctx = ctx.shape[1]
    depth = w_mod.shape[0]
    assert d == D_MODEL and seq % GRID_W == 0
    assert nctx % POOL_ROWS == 0 and seq % POOL_ROWS == 0
    bc = nb * nctx
    t = bc + nb * seq
    tm = _pick((1024, 512, 256), bc, seq)
    rt = _pick((512, 256), bc, seq)

    def row_of(i):
        r = i * tm
        return jnp.where(r < bc, 0, 1 + (r - bc) // seq)

    sizes = (SSD_INNER, SSD_CONV_CH, SSD_HEADS, BRANCH_WIDTH, BRANCH_WIDTH, BRANCH_WIDTH,
             BRANCH_WIDTH, RET_QK, RET_QK, BRANCH_WIDTH, BRANCH_WIDTH)
    offs = [0]
    for sz in sizes:
        offs.append(offs[-1] + sz)
    w_in16 = lax.optimization_barrier(w_in.astype(BF16))
    w_main = jnp.concatenate([w_in16[:, :, :offs[2]], w_in16[:, :, offs[3]:]], axis=-1)
    w_dt = _pad_lanes(w_in16[:, :, offs[2]:offs[3]])
    assert w_main.shape[-1] == U_COLS
    wg_b = w_gate.astype(BF16)
    wb_b = w_branch.astype(BF16)
    wo_b = w_o.astype(BF16)
    up_b = ffn_up.astype(BF16)
    down_b = ffn_down.astype(BF16)
    poolw_b = pool_w.astype(BF16)

    nrows = 8 * ((1 + nb + 7) // 8)
    cvec = jnp.zeros((nrows, d), F32).at[0].set(c_ctx).at[1:1 + nb].set(c)
    mod = _mod_call(cvec, w_mod, b_mod)
    mod5 = mod.reshape(depth, nrows, 6, 1, d)

    cos_t, sin_t = _rope_tables(nctx, seq)
    rep = lambda a: jnp.broadcast_to(a.reshape(depth, 2 * SSD_HEADS, 1), (depth, 2 * SSD_HEADS, CHUNK))
    dt_bias_p = rep(ssd_dt_bias)
    a_log_p = rep(ssd_a_log)
    decay_p = _pad_lanes(ret_decay_logit)
    d_exp = jnp.repeat(ssd_d, SSD_HEAD_DIM, axis=-1)[:, None, :]

    r3 = lambda a: a[:, None, :]
    xs = jnp.concatenate([ctx.reshape(bc, d), x.reshape(nb * seq, d)], axis=0)
    for l in range(depth):
        u, h, dt = _in_proj_call(xs, r3(norm1_w), mod5, w_main, w_dt, l, 0, 1, tm, row_of)
        xbc = _ssd_conv_call(u, ssd_conv_w, r3(ssd_conv_b), l, rt, bc, nctx, seq)
        y_ssd = _ssd_scan_call(xbc, dt, u, dt_bias_p, a_log_p, d_exp, r3(ssd_norm_w), l, nb, nctx, seq)
        y_pool = _pool_call(u, poolw_b, r3(pool_scale), l, bc, nctx, seq)
        y_sc = _sconv_call(u, sconv_w, l, rt, bc, nctx, seq)
        y_ret = _ret_scan_call(u, cos_t, sin_t, decay_p, l, nb, nctx, seq)
        lat = l == depth - 1
        bc_l = 0 if lat else bc
        row_l = _mod_row(tm, bc_l, seq)
        merged = _merge_call(h, (y_ssd, y_pool, y_sc, y_ret), wg_b, b_gate[:, :, None, :], wb_b, l, tm,
                             row0=bc if lat else 0)
        xs, h2 = _outproj_norm_call(merged, wo_b, xs, mod5, r3(norm2_w), l, 2, 3, 4, rt, bc, seq)
        gact = _ffn_gate_call(h2, up_b, ffn_conv_w, r3(ffn_conv_b), l, tm, bc_l, nctx, seq)
        xs = _res_call(gact, down_b, xs, mod5, l, 5, tm, row_l, "ffn_down_residual")
    out = _final_norm_call(xs, final_norm_w[None, :], 0, nb * seq, tm)
    return out.reshape(nb, seq, d)
```

```python
import functools
import math

import jax
import jax.numpy as jnp
from jax import lax
from jax.experimental import pallas as pl
from jax.experimental.pallas import tpu as pltpu

F32 = jnp.float32
BF16 = jnp.bfloat16

D_MODEL = 2048
GRID_W = 64
EPS = 1e-6
CHUNK = 128
BRANCH_WIDTH = D_MODEL // 2
SSD_INNER = BRANCH_WIDTH
SSD_HEAD_DIM = 64
SSD_HEADS = SSD_INNER // SSD_HEAD_DIM
SSD_GROUPS = 4
SSD_STATE = 128
SSD_CONV_CH = SSD_INNER + 2 * SSD_GROUPS * SSD_STATE
POOL_WINDOWS = (2, 4, 8, 16)
POOL_GROUP = BRANCH_WIDTH // len(POOL_WINDOWS)
RET_HEADS = 8
RET_V_HEAD = BRANCH_WIDTH // RET_HEADS
RET_K_HEAD = RET_V_HEAD // 2
RET_QK = RET_HEADS * RET_K_HEAD
ROPE_BASE = 10000.0
D_FF = 256 * ((8 * D_MODEL // 3 + 255) // 256)

LANES = 128
HALO = 16
VMEM_LIMIT = 56 * 1024 * 1024

U_Z = 0
U_XBC = 1024
U_POOL = 3072
U_SCB = 4096
U_SCC = 5120
U_SCX = 6144
U_RQ = 7168
U_RK = 7680
U_RV = 8192
U_RG = 9216
U_COLS = 10240


def _params(sem):
    return pltpu.CompilerParams(dimension_semantics=sem, vmem_limit_bytes=VMEM_LIMIT)


def _silu(v):
    return v * jax.nn.sigmoid(v)


def _pick(cands, *ns):
    for c in cands:
        if all(n % c == 0 for n in ns):
            return c
    raise ValueError(f"no tile in {cands} divides {ns}")


def _mod_const(v, n):
    if n & (n - 1) == 0:
        return v & (n - 1)
    return lax.rem(v, n)


def _seq_pos(r0, rows, bc, ctx, seq):
    in_ctx = r0 < bc
    g = r0 + lax.broadcasted_iota(jnp.int32, (rows, 1), 0)
    pos = jnp.where(in_ctx, _mod_const(g, ctx), _mod_const(g - bc, seq))
    n = jnp.where(in_ctx, ctx, seq)
    return pos, n


def _mod_kernel(c_ref, w_ref, b_ref, o_ref):
    a = _silu(c_ref[...]).astype(BF16)
    o_ref[...] = jnp.dot(a, w_ref[...].astype(BF16), preferred_element_type=F32) + b_ref[...]


def _mod_call(cvec, w_mod, b_mod):
    nl, d, n6 = w_mod.shape
    r = cvec.shape[0]
    tn = 1024
    return pl.pallas_call(
        _mod_kernel,
        grid=(nl, n6 // tn),
        in_specs=[pl.BlockSpec((r, d), lambda l, j: (0, 0)),
                  pl.BlockSpec((None, d, tn), lambda l, j: (l, 0, j)),
                  pl.BlockSpec((None, 1, tn), lambda l, j: (l, 0, j))],
        out_specs=pl.BlockSpec((None, r, tn), lambda l, j: (l, 0, j)),
        out_shape=jax.ShapeDtypeStruct((nl, r, n6), F32),
        compiler_params=_params(("parallel", "parallel")),
        name="mod_vectors",
    )(cvec, w_mod, b_mod.reshape(nl, 1, n6))


def _normed(x, nw, sh, sc):
    y = x * lax.rsqrt(jnp.mean(jnp.square(x), axis=-1, keepdims=True) + EPS)
    return (y * nw) * (1.0 + sc) + sh


def _in_proj_kernel(x_ref, nw_ref, sh_ref, sc_ref, w_ref, wdt_ref, u_ref, h_ref, dt_ref):
    @pl.when(pl.program_id(1) == 0)
    def _():
        h = _normed(x_ref[...], nw_ref[...], sh_ref[...], sc_ref[...]).astype(BF16)
        h_ref[...] = h
        dt_ref[...] = jnp.dot(h, wdt_ref[...], preferred_element_type=F32)

    u_ref[...] = jnp.dot(h_ref[...], w_ref[...], preferred_element_type=F32).astype(BF16)


def _mod_spec(l, k, tn, row_fn, col_fn):
    return pl.BlockSpec((None, None, None, 1, tn),
                        lambda *g: (l, row_fn(*g), k, 0, col_fn(*g)))


def _mod_row(tile, bc, seq):
    def row(i):
        r = i * tile
        return jnp.where(r < bc, 0, 1 + (r - bc) // seq)
    return row


def _in_proj_call(x, norm_w, mod5, w, wdt, l, k_sh, k_sc, tm, row_of):
    t, d = x.shape
    n = w.shape[-1]
    tn = 1024
    zero = lambda i, j: 0
    row_of = (lambda f: lambda i, j: f(i))(row_of)
    return pl.pallas_call(
        _in_proj_kernel,
        grid=(t // tm, n // tn),
        in_specs=[pl.BlockSpec((tm, d), lambda i, j: (i, 0)),
                  pl.BlockSpec((None, 1, d), lambda i, j: (l, 0, 0)),
                  _mod_spec(l, k_sh, d, row_of, zero),
                  _mod_spec(l, k_sc, d, row_of, zero),
                  pl.BlockSpec((None, d, tn), lambda i, j: (l, 0, j)),
                  pl.BlockSpec((None, d, LANES), lambda i, j: (l, 0, 0))],
        out_specs=[pl.BlockSpec((tm, tn), lambda i, j: (i, j)),
                   pl.BlockSpec((tm, d), lambda i, j: (i, 0)),
                   pl.BlockSpec((tm, LANES), lambda i, j: (i, 0))],
        out_shape=[jax.ShapeDtypeStruct((t, n), BF16),
                   jax.ShapeDtypeStruct((t, d), BF16),
                   jax.ShapeDtypeStruct((t, LANES), F32)],
        compiler_params=_params(("parallel", "arbitrary")),
        name="in_proj",
    )(x, norm_w, mod5, mod5, w, wdt)


def _merge_kernel(h_ref, y0_ref, y1_ref, y2_ref, y3_ref, wg_ref, bg_ref, wb_ref, o_ref):
    h = h_ref[...]
    acc = None
    for i, y_ref in enumerate((y0_ref, y1_ref, y2_ref, y3_ref)):
        gate = jax.nn.sigmoid(jnp.dot(h, wg_ref[i], preferred_element_type=F32) + bg_ref[i])
        term = gate * jnp.dot(y_ref[...], wb_ref[i], preferred_element_type=F32)
        acc = term if acc is None else acc + term
    o_ref[...] = acc.astype(BF16)


def _merge_call(h, ys, wg, bg, wb, l, tm, row0=0):
    t, d = h.shape
    bw = ys[0].shape[-1]
    tn = 256
    nb = len(ys)
    off = row0 // tm
    t_out = t - row0
    return pl.pallas_call(
        _merge_kernel,
        grid=(t_out // tm, d // tn),
        in_specs=[pl.BlockSpec((tm, d), lambda i, j: (i + off, 0))]
                 + [pl.BlockSpec((tm, bw), lambda i, j: (i + off, 0)) for _ in ys]
                 + [pl.BlockSpec((None, nb, d, tn), lambda i, j: (l, 0, 0, j)),
                    pl.BlockSpec((None, nb, 1, tn), lambda i, j: (l, 0, 0, j)),
                    pl.BlockSpec((None, nb, bw, tn), lambda i, j: (l, 0, 0, j))],
        out_specs=pl.BlockSpec((tm, tn), lambda i, j: (i, j)),
        out_shape=jax.ShapeDtypeStruct((t_out, d), BF16),
        compiler_params=_params(("parallel", "arbitrary")),
        name="branch_merge",
    )(h, *ys, wg, bg, wb)


def _res_kernel(a_ref, w_ref, x_ref, g_ref, o_ref):
    o_ref[...] = x_ref[...] + g_ref[...] * jnp.dot(a_ref[...], w_ref[...],
                                                   preferred_element_type=F32)


def _res_call(a, w, x, mod5, l, k_gate, tm, row_of, name):
    t, kdim = a.shape
    d = x.shape[-1]
    tn = 512
    return pl.pallas_call(
        _res_kernel,
        grid=(t // tm, d // tn),
        in_specs=[pl.BlockSpec((tm, kdim), lambda i, j: (i, 0)),
                  pl.BlockSpec((None, kdim, tn), lambda i, j: (l, 0, j)),
                  pl.BlockSpec((tm, tn), lambda i, j: (i, j)),
                  _mod_spec(l, k_gate, tn, lambda i, j: row_of(i), lambda i, j: j)],
        out_specs=pl.BlockSpec((tm, tn), lambda i, j: (i, j)),
        out_shape=jax.ShapeDtypeStruct((t, d), F32),
        compiler_params=_params(("parallel", "arbitrary")),
        name=name,
    )(a, w, x, mod5)


EPI_ROWS = 128


def _outproj_norm_kernel(a_ref, w_ref, x_ref, g_ref, nw_ref, sh_ref, sc_ref, xo_ref, ho_ref, y0, y1):
    s = pl.program_id(0)
    tm = a_ref.shape[0]

    @pl.when(s == 0)
    def _():
        y1[...] = jnp.zeros_like(y1)

    def step(y_cur, y_prev):
        nslab = tm // EPI_ROWS
        cw = w_ref.shape[1] // nslab
        for k in range(nslab):
            rs = slice(k * EPI_ROWS, (k + 1) * EPI_ROWS)
            cs = slice(k * cw, (k + 1) * cw)
            y_cur[:, cs] = jnp.dot(a_ref[...], w_ref[:, cs], preferred_element_type=F32)
            xn = x_ref[rs, :] + g_ref[...] * y_prev[rs, :]
            xo_ref[rs, :] = xn
            ho_ref[rs, :] = _normed(xn, nw_ref[...], sh_ref[...], sc_ref[...]).astype(BF16)

    @pl.when(s % 2 == 0)
    def _():
        step(y0, y1)

    @pl.when(s % 2 == 1)
    def _():
        step(y1, y0)


def _outproj_norm_call(a, w, x, mod5, norm_w, l, k_gate, k_sh, k_sc, tm, bc, seq):
    t, kdim = a.shape
    d = x.shape[-1]
    ni = t // tm
    row0 = x.shape[0] - t
    assert row0 in (0, bc)
    off = row0 // tm
    row = _mod_row(tm, bc - row0, seq)
    cur = lambda s: jnp.minimum(s, ni - 1)
    prev = lambda s: jnp.maximum(s - 1, 0)
    zero = lambda s: 0
    return pl.pallas_call(
        _outproj_norm_kernel,
        grid=(ni + 1,),
        in_specs=[pl.BlockSpec((tm, kdim), lambda s: (cur(s), 0)),
                  pl.BlockSpec((None, kdim, d), lambda s: (l, 0, 0), pipeline_mode=pl.Buffered(1)),
                  pl.BlockSpec((tm, d), lambda s: (prev(s) + off, 0)),
                  _mod_spec(l, k_gate, d, lambda s: row(prev(s)), zero),
                  pl.BlockSpec((None, 1, d), lambda s: (l, 0, 0)),
                  _mod_spec(l, k_sh, d, lambda s: row(prev(s)), zero),
                  _mod_spec(l, k_sc, d, lambda s: row(prev(s)), zero)],
        out_specs=[pl.BlockSpec((tm, d), lambda s: (prev(s), 0)),
                   pl.BlockSpec((tm, d), lambda s: (prev(s), 0))],
        out_shape=[jax.ShapeDtypeStruct((t, d), F32),
                   jax.ShapeDtypeStruct((t, d), BF16)],
        scratch_shapes=[pltpu.VMEM((tm, d), F32), pltpu.VMEM((tm, d), F32)],
        compiler_params=_params(("arbitrary",)),
        name="out_proj_residual_norm",
    )(a, w, x, mod5, norm_w, mod5, mod5)


FFN_PIECE_ROWS = 64
FFN_SLABS = 4
SEQ_ALIGN = 256


def _ffn_gate_kernel(h_ref, hp_ref, hn_ref, wa_ref, wb_ref, cwa_ref, cwb_ref, ba_ref, bb_ref, o_ref,
                     h_s, a_s, b_s, *, nj, bc, ctx, seq):
    s = pl.program_id(0)
    tm = h_ref.shape[0]

    @pl.when(s == 0)
    def _():
        a_s[1] = jnp.zeros(a_s.shape[1:], F32)
        b_s[1] = jnp.zeros(b_s.shape[1:], F32)

    @pl.when(s % nj == 0)
    def _():
        h_s[0:HALO, :] = hp_ref[...]
        h_s[HALO:HALO + tm, :] = h_ref[...]
        h_s[HALO + tm:, :] = hn_ref[...]

    r0 = (jnp.maximum(s - 1, 0) // nj) * tm
    in_ctx = r0 < bc
    tn = o_ref.shape[1]
    rows_ext = tm + 2 * HALO
    pr = FFN_PIECE_ROWS
    row_first = lax.broadcasted_iota(jnp.int32, (pr, 1), 0) == 0
    row_last = lax.broadcasted_iota(jnp.int32, (pr, 1), 0) == pr - 1

    def seq_start(g):
        return jnp.where(in_ctx, _mod_const(g, ctx), _mod_const(g - bc, seq)) == 0

    def taps(w_ref, bias_ref, cs):
        return [jnp.broadcast_to(w_ref[j:j + 1, cs], (pr, LANES)) for j in range(3)] + \
               [jnp.broadcast_to(bias_ref[:, cs], (pr, LANES))]

    col_slabs = [slice(c, c + LANES) for c in range(0, tn, LANES)]
    taps_a = [taps(cwa_ref, ba_ref, cs) for cs in col_slabs]
    taps_b = [taps(cwb_ref, bb_ref, cs) for cs in col_slabs]

    def conv_piece(src, r, cs, tp, kill_prev, kill_next):
        base = HALO + r
        ext = src[base - 8:base + pr + 8, cs]
        xm1 = pltpu.roll(ext, 1, 0)[8:8 + pr]
        x00 = ext[8:8 + pr]
        xp1 = pltpu.roll(ext, pr + 15, 0)[8:8 + pr]
        if kill_prev is not None:
            xm1 = jnp.where(kill_prev, 0.0, xm1)
        if kill_next is not None:
            xp1 = jnp.where(kill_next, 0.0, xp1)
        return xm1 * tp[0] + x00 * tp[1] + xp1 * tp[2] + tp[3]

    def epilogue(a_prev, b_prev, lo, hi):
        for r in range(lo, hi, pr):
            kill_prev = kill_next = None
            if r % SEQ_ALIGN == 0:
                kill_prev = jnp.logical_and(row_first, seq_start(r0 + r))
            if (r + pr) % SEQ_ALIGN == 0:
                kill_next = jnp.logical_and(row_last, seq_start(r0 + r + pr))
            for ci, cs in enumerate(col_slabs):
                a = conv_piece(a_prev, r, cs, taps_a[ci], kill_prev, kill_next)
                b = conv_piece(b_prev, r, cs, taps_b[ci], kill_prev, kill_next)
                o_ref[r:r + pr, cs] = (_silu(a) * b).astype(BF16)

    def step(a_cur, b_cur, a_prev, b_prev):
        nslab = FFN_SLABS
        for k in range(nslab):
            m_lo = (k * rows_ext // nslab) // HALO * HALO
            m_hi = rows_ext if k == nslab - 1 else ((k + 1) * rows_ext // nslab) // HALO * HALO
            h = h_s[m_lo:m_hi, :]
            a_cur[m_lo:m_hi, :] = jnp.dot(h, wa_ref[...], preferred_element_type=F32)
            b_cur[m_lo:m_hi, :] = jnp.dot(h, wb_ref[...], preferred_element_type=F32)
            epilogue(a_prev, b_prev, k * tm // nslab, (k + 1) * tm // nslab)

    cur = s % 2
    step(a_s.at[cur], b_s.at[cur], a_s.at[1 - cur], b_s.at[1 - cur])


def _ffn_gate_call(h, w, cw, cb, l, tm, bc, ctx, seq):
    t, d = h.shape
    tn = 512
    nj = D_FF // tn
    ni = t // tm
    per = tm // HALO
    last = t // HALO - 1
    ci = lambda s: jnp.minimum(s // nj, ni - 1)
    cj = lambda s: s % nj
    pi = lambda s: jnp.maximum(s - 1, 0) // nj
    pj = lambda s: jnp.maximum(s - 1, 0) % nj
    return pl.pallas_call(
        functools.partial(_ffn_gate_kernel, nj=nj, bc=bc, ctx=ctx, seq=seq),
        grid=(ni * nj + 1,),
        in_specs=[pl.BlockSpec((tm, d), lambda s: (ci(s), 0)),
                  pl.BlockSpec((HALO, d), lambda s: (jnp.maximum(ci(s) * per - 1, 0), 0)),
                  pl.BlockSpec((HALO, d), lambda s: (jnp.minimum((ci(s) + 1) * per, last), 0)),
                  pl.BlockSpec((None, d, tn), lambda s: (l, 0, cj(s))),
                  pl.BlockSpec((None, d, tn), lambda s: (l, 0, nj + cj(s))),
                  pl.BlockSpec((None, 3, tn), lambda s: (l, 0, pj(s))),
                  pl.BlockSpec((None, 3, tn), lambda s: (l, 0, nj + pj(s))),
                  pl.BlockSpec((None, 1, tn), lambda s: (l, 0, pj(s))),
                  pl.BlockSpec((None, 1, tn), lambda s: (l, 0, nj + pj(s)))],
        out_specs=pl.BlockSpec((tm, tn), lambda s: (pi(s), pj(s))),
        out_shape=jax.ShapeDtypeStruct((t, D_FF), BF16),
        scratch_shapes=[pltpu.VMEM((tm + 2 * HALO, d), BF16)]
                       + [pltpu.VMEM((2, tm + 2 * HALO, tn), F32) for _ in range(2)],
        compiler_params=_params(("arbitrary",)),
        name="ffn_up_conv_gate",
    )(h, h, h, w, w, cw, cw, cb, cb)


def _final_norm_kernel(x_ref, w_ref, o_ref):
    x = x_ref[...]
    o_ref[...] = (x * lax.rsqrt(jnp.mean(jnp.square(x), axis=-1, keepdims=True) + EPS)) * w_ref[...]


def _final_norm_call(x, w, row0, rows, tm):
    d = x.shape[-1]
    off = row0 // tm
    return pl.pallas_call(
        _final_norm_kernel,
        grid=(rows // tm,),
        in_specs=[pl.BlockSpec((tm, d), lambda i: (i + off, 0)),
                  pl.BlockSpec((1, d), lambda i: (0, 0))],
        out_specs=pl.BlockSpec((tm, d), lambda i: (i, 0)),
        out_shape=jax.ShapeDtypeStruct((rows, d), F32),
        compiler_params=_params(("parallel",)),
        name="final_norm",
    )(x, w)


def _halo_specs(rt, cols, col_idx, t):
    per = rt // HALO
    last = t // HALO - 1
    return [pl.BlockSpec((rt, cols), lambda i, j: (i, col_idx(j))),
            pl.BlockSpec((HALO, cols), lambda i, j: (jnp.maximum(i * per - 1, 0), col_idx(j))),
            pl.BlockSpec((HALO, cols), lambda i, j: (jnp.minimum((i + 1) * per, last), col_idx(j)))]


def _shift_pm1(x, prev_row, next_row, pos, n):
    rt = x.shape[0]
    row = lax.broadcasted_iota(jnp.int32, (rt, 1), 0)
    xm1 = jnp.where(row == 0, prev_row, pltpu.roll(x, 1, 0))
    xm1 = jnp.where(pos == 0, 0.0, xm1)
    xp1 = jnp.where(row == rt - 1, next_row, pltpu.roll(x, rt - 1, 0))
    xp1 = jnp.where(pos == n - 1, 0.0, xp1)
    return xm1, xp1


def _conv3(x, prev_row, next_row, w, pos, n):
    xm1, xp1 = _shift_pm1(x, prev_row, next_row, pos, n)
    return xm1 * w[0:1] + x * w[1:2] + xp1 * w[2:3]


def _ssd_conv_kernel(x_ref, p_ref, n_ref, w_ref, b_ref, o_ref, *, bc, ctx, seq):
    rt = x_ref.shape[0]
    pos, n = _seq_pos(pl.program_id(0) * rt, rt, bc, ctx, seq)
    x = x_ref[...].astype(F32)
    prev_row = p_ref[...].astype(F32)[HALO - 1:HALO]
    next_row = n_ref[...].astype(F32)[0:1]
    y = _conv3(x, prev_row, next_row, w_ref[...], pos, n) + b_ref[...]
    o_ref[...] = _silu(y).astype(BF16)


def _ssd_conv_call(u, w, b, l, rt, bc, ctx, seq):
    t = u.shape[0]
    cols = 1024
    nj = SSD_CONV_CH // cols
    base = U_XBC // cols
    return pl.pallas_call(
        functools.partial(_ssd_conv_kernel, bc=bc, ctx=ctx, seq=seq),
        grid=(t // rt, nj),
        in_specs=_halo_specs(rt, cols, lambda j: base + j, t)
                 + [pl.BlockSpec((None, 3, cols), lambda i, j: (l, 0, j)),
                    pl.BlockSpec((None, 1, cols), lambda i, j: (l, 0, j))],
        out_specs=pl.BlockSpec((rt, cols), lambda i, j: (i, j)),
        out_shape=jax.ShapeDtypeStruct((t, SSD_CONV_CH), BF16),
        compiler_params=_params(("parallel", "parallel")),
        name="ssd_conv",
    )(u, u, u, w, b)


def _sconv_body(r0, b_ref, c_ref, cp_ref, cn_ref, x_ref, xp_ref, xn_ref, w_ref, o_ref, *, bc, ctx, seq):
    rt = b_ref.shape[0]
    pos, n = _seq_pos(r0, rt, bc, ctx, seq)
    cx = c_ref[...].astype(F32) * x_ref[...].astype(F32)
    prev_row = (cp_ref[...].astype(F32) * xp_ref[...].astype(F32))[HALO - 1:HALO]
    next_row = (cn_ref[...].astype(F32) * xn_ref[...].astype(F32))[0:1]
    y = b_ref[...].astype(F32) * _conv3(cx, prev_row, next_row, w_ref[...], pos, n)
    o_ref[...] = y.astype(BF16)


POOL_ROWS = 256
POOL_EXT = 384


def _pool_body(r0, x_ref, p_ref, n_ref, band_ref, cnt_ref, w_ref, s_ref, o_ref, *, bc, ctx, seq):
    rt = POOL_ROWS
    in_ctx = r0 < bc
    p0 = jnp.where(in_ctx, _mod_const(r0, ctx), _mod_const(r0 - bc, seq))
    n = jnp.where(in_ctx, ctx, seq)
    has_prev = p0 != 0
    has_next = p0 + rt != n
    zero = jnp.zeros((HALO, POOL_GROUP), BF16)
    pad = jnp.zeros((POOL_EXT - rt - 2 * HALO, POOL_GROUP), BF16)
    case = jnp.where(has_prev, 0, 2) + jnp.where(has_next, 0, 1)
    groups = [slice(gi * POOL_GROUP, (gi + 1) * POOL_GROUP) for gi in range(len(POOL_WINDOWS))]
    sums = []
    for gi, cs in enumerate(groups):
        ext = jnp.concatenate([jnp.where(has_prev, p_ref[:, cs], zero), x_ref[:, cs],
                               jnp.where(has_next, n_ref[:, cs], zero), pad], axis=0)
        sums.append(jnp.dot(band_ref[gi], ext, preferred_element_type=F32))
    for gi, cs in enumerate(groups):
        cnt = cnt_ref[gi, case]
        mean = sums[gi] / jnp.concatenate([cnt] * (POOL_GROUP // LANES), axis=1)
        pooled = (mean - x_ref[:, cs].astype(F32)).astype(BF16)
        y = jnp.dot(pooled, w_ref[gi], preferred_element_type=F32) * s_ref[:, cs]
        o_ref[:, cs] = y.astype(BF16)


def _pool_bands():
    t = jnp.arange(POOL_ROWS)[:, None]
    s = jnp.arange(POOL_EXT)[None, :] - HALO
    return jnp.stack([((s >= t - w // 2) & (s < t - w // 2 + w)) for w in POOL_WINDOWS]).astype(BF16)


def _pool_counts():
    t = jnp.arange(POOL_ROWS)
    tabs = []
    for w in POOL_WINDOWS:
        before = jnp.maximum(w // 2 - t, 0)
        after = jnp.maximum(t - w // 2 + w - POOL_ROWS, 0)
        cases = [w - (before if c >= 2 else 0) - (after if c % 2 else 0) for c in range(4)]
        tabs.append(jnp.stack([jnp.broadcast_to(c, (POOL_ROWS,)) for c in cases]))
    tab = jnp.stack(tabs).astype(F32)
    return jnp.broadcast_to(tab[..., None], tab.shape + (LANES,))


def _scan_positions(ncc, nk):
    def kpos(sw, s):
        back = jnp.where(s < ncc, ncc - 1 - s, nk - 1 - (s - ncc))
        return jnp.where(sw == 0, back, s)
    return kpos


def _chunk_row(b, k, ncc, ncl, nb):
    return jnp.where(k < ncc, b * ncc + k, nb * ncc + b * ncl + (k - ncc))


def _tri_masks():
    row = lax.broadcasted_iota(jnp.int32, (CHUNK, CHUNK), 0)
    col = lax.broadcasted_iota(jnp.int32, (CHUNK, CHUNK), 1)
    return row, col, col <= row, col >= row


def _colb(x, h):
    return jnp.broadcast_to(x[:, h:h + 1], (x.shape[0], LANES))


MASKED = -1e30


SCAN_CHUNKS = 2


def _ssd_scan_parts(xbc_ref, dt_ref, z_ref, bias_ref, alog_ref, dexp_ref, nw_ref, sp_ref, sg_ref, o_ref,
                    hb_all, hf_s, hb_s, y_s, vw_s, pre_s, *, kpos):
    row, col, lower, upper = _tri_masks()
    lane_lo = col < SSD_HEAD_DIM
    row_lo1 = lane_lo[0:1]
    nh = SSD_HEADS
    n_pairs = nh // 2
    pairs_per_group = n_pairs // SSD_GROUPS
    gw = SSD_INNER // SSD_GROUPS

    def chunk_rows(c):
        return slice(c * CHUNK, (c + 1) * CHUNK)

    def split2(x):
        hi = x.astype(BF16).astype(F32)
        return hi, x - hi

    def split3(x):
        hi, r = split2(x)
        mid, lo = split2(r)
        return hi, mid, lo

    def to_tokens(parts):
        used = sum(p.shape[0] for p in parts)
        return jnp.concatenate(list(parts) + [jnp.zeros((CHUNK - used, CHUNK), F32)], axis=0).T.astype(BF16)

    def store_state_operand(wt, rc):
        lhs = to_tokens((jnp.zeros((4 * nh, CHUNK), F32),) + split2(wt))
        w_full = jnp.dot(lhs, sp_ref[:, 2 * SSD_INNER:3 * SSD_INNER], preferred_element_type=F32)
        vw_s[...] = (xbc_ref[rc, 0:SSD_INNER].astype(F32) * w_full).astype(BF16)

    def seg_all(cum):
        parts = split3(cum)
        ones = jnp.ones((nh, CHUNK), F32)
        a = to_tokens(parts + (ones, ones, ones))
        neg = jnp.concatenate([-p for p in parts], axis=0)
        tiled = jnp.concatenate([neg] * nh, axis=1)
        rows_b = jnp.where(sg_ref[3 * nh:6 * nh, :] != 0, tiled, 0.0).astype(BF16)
        b = jnp.concatenate([sg_ref[0:3 * nh, :], rows_b,
                             jnp.zeros((CHUNK - 6 * nh, nh * CHUNK), BF16)], axis=0)
        return jnp.dot(a, b, preferred_element_type=F32)

    def pair_scalar(x, p, j):
        return jnp.where(row_lo1, jnp.broadcast_to(x[2 * p:2 * p + 1, j:j + 1], (1, LANES)),
                         jnp.broadcast_to(x[2 * p + 1:2 * p + 2, j:j + 1], (1, LANES)))

    def state_update(h_s, g, dec_rows, rc):
        gs = slice(g * gw, (g + 1) * gw)
        bm = xbc_ref[rc, SSD_INNER + g * SSD_STATE:SSD_INNER + (g + 1) * SSD_STATE]
        st = lax.dot_general(bm, vw_s[:, gs], (((0,), (0,)), ((), ())), preferred_element_type=F32)
        h_s[:, gs] = h_s[:, gs] * jnp.concatenate(dec_rows, axis=1) + st

    def init_backward():
        hb_s[...] = jnp.zeros_like(hb_s)

    def init_forward():
        hf_s[...] = jnp.zeros_like(hf_s)

    def backward_step():
        for c in reversed(range(SCAN_CHUNKS)):
            rc = chunk_rows(c)
            ck = kpos * SCAN_CHUNKS + c
            dtt = dt_ref[rc, :].T[0:nh]
            dt2 = jax.nn.softplus(jnp.concatenate([dtt, dtt], axis=0) + bias_ref[...])
            la2 = dt2 * (-jnp.exp(alog_ref[...]))
            tri2 = jnp.concatenate([upper.astype(F32), lower.astype(F32)], axis=1)
            cum2 = jnp.dot(la2, tri2, precision=lax.Precision.HIGHEST, preferred_element_type=F32)
            cum_f = cum2[0:nh, 0:CHUNK]
            cum_b = cum2[nh:2 * nh, CHUNK:2 * CHUNK]
            pre_s[ck, 0:2 * nh] = dt2
            pre_s[ck, 2 * nh:3 * nh] = cum_f
            pre_s[ck, 3 * nh:4 * nh] = cum_b

            hb_all[ck] = hb_s[...].astype(BF16)
            store_state_operand(jnp.exp(cum_b[:, 0:1] - cum_b) * dt2[nh:2 * nh], rc)
            for g in range(SSD_GROUPS):
                dec_rows = [jnp.exp(pair_scalar(cum_b, g * pairs_per_group + pp, 0))
                            for pp in range(pairs_per_group)]
                state_update(hb_s, g, dec_rows, rc)

    def forward_step():
        for c in range(SCAN_CHUNKS):
            rc = chunk_rows(c)
            ck = kpos * SCAN_CHUNKS + c
            dt2 = pre_s[ck, 0:2 * nh]
            cum_f = pre_s[ck, 2 * nh:3 * nh]
            cum_b = pre_s[ck, 3 * nh:4 * nh]
            hb_in = hb_all[ck]
            wt = jnp.exp(cum_f[:, CHUNK - 1:CHUNK] - cum_f) * dt2[0:nh]
            ef, eb = split2(jnp.exp(cum_f)), split2(jnp.exp(cum_b))
            lhs = to_tokens((ef[0], eb[0], ef[1], eb[1]) + split2(wt))
            spread = jnp.dot(lhs, sp_ref[...], preferred_element_type=F32)
            vw_s[...] = (xbc_ref[rc, 0:SSD_INNER].astype(F32)
                         * spread[:, 2 * SSD_INNER:3 * SSD_INNER]).astype(BF16)
            seg_f = seg_all(cum_f)
            seg_b = seg_all(cum_b)
            for g in range(SSD_GROUPS):
                gs = slice(g * gw, (g + 1) * gw)
                bm = xbc_ref[rc, SSD_INNER + g * SSD_STATE:SSD_INNER + (g + 1) * SSD_STATE]
                cm = xbc_ref[rc, SSD_INNER + SSD_GROUPS * SSD_STATE + g * SSD_STATE:
                             SSD_INNER + SSD_GROUPS * SSD_STATE + (g + 1) * SSD_STATE]
                sc = lax.dot_general(cm, bm, (((1,), (1,)), ((), ())), preferred_element_type=F32)
                hcat = jnp.concatenate([hf_s[:, gs].astype(BF16), hb_in[:, gs]], axis=1)
                gfb = jnp.dot(cm, hcat, preferred_element_type=F32)
                dec_rows = []
                for pp in range(pairs_per_group):
                    p = g * pairs_per_group + pp
                    ps = slice(p * LANES, (p + 1) * LANES)
                    xs_pair = xbc_ref[rc, ps]
                    ms = []
                    for h in (2 * p, 2 * p + 1):
                        hs = slice(h * CHUNK, (h + 1) * CHUNK)
                        dec_f = jnp.exp(jnp.where(lower, seg_f[:, hs], MASKED))
                        dec_b = jnp.exp(jnp.where(upper, seg_b[:, hs], MASKED))
                        m = sc * (dec_f * dt2[h:h + 1] + dec_b * dt2[nh + h:nh + h + 1])
                        ms.append(m.astype(BF16))
                    zero = jnp.zeros_like(xs_pair)
                    x_split = jnp.concatenate([jnp.where(lane_lo, xs_pair, zero),
                                               jnp.where(lane_lo, zero, xs_pair)], axis=0)
                    y_intra = jnp.dot(jnp.concatenate(ms, axis=1), x_split, preferred_element_type=F32)
                    lo = pp * LANES
                    y_inter = (gfb[:, lo:lo + LANES] * spread[:, ps]
                               + gfb[:, gw + lo:gw + lo + LANES]
                               * spread[:, SSD_INNER + p * LANES:SSD_INNER + (p + 1) * LANES])
                    y_s[:, ps] = y_intra + y_inter
                    dec_rows.append(jnp.exp(pair_scalar(cum_f, p, CHUNK - 1)))
                state_update(hf_s, g, dec_rows, rc)

            xs32 = xbc_ref[rc, 0:SSD_INNER].astype(F32)
            y = (y_s[...] + xs32 * dexp_ref[...]) * _silu(z_ref[rc, :].astype(F32))
            y = y * lax.rsqrt(jnp.mean(jnp.square(y), axis=-1, keepdims=True) + EPS)
            o_ref[rc, :] = (y * nw_ref[...]).astype(BF16)

    return init_backward, backward_step, init_forward, forward_step


def _ssd_spread_constants():
    nh = SSD_HEADS
    r = jnp.arange(CHUNK)[:, None]
    c = jnp.arange(3 * SSD_INNER)[None, :]
    block_of_row = jnp.array([0, 1, 0, 1, 2, 2, -1, -1])[r // nh]
    sp = (block_of_row == c // SSD_INNER) & (r % nh == (c % SSD_INNER) // SSD_HEAD_DIM)
    c2 = jnp.arange(nh * CHUNK)[None, :]
    sg = (r < 6 * nh) & (r % nh == c2 // CHUNK)
    return sp.astype(BF16), sg.astype(BF16)


N_SSD_IN, N_SSD_SCRATCH = 9, 6
N_RET_IN, N_RET_SCRATCH = 7, 5


def _ssd_scan_specs(l, rows, in_row, nk):
    in_specs = [pl.BlockSpec((rows, SSD_CONV_CH), lambda b, sw, s: (in_row(b, sw, s), 0)),
                pl.BlockSpec((rows, LANES), lambda b, sw, s: (in_row(b, sw, s), 0)),
                pl.BlockSpec((rows, SSD_INNER), lambda b, sw, s: (in_row(b, sw, s), U_Z // SSD_INNER)),
                pl.BlockSpec((None, 2 * SSD_HEADS, CHUNK), lambda b, sw, s: (l, 0, 0)),
                pl.BlockSpec((None, 2 * SSD_HEADS, CHUNK), lambda b, sw, s: (l, 0, 0)),
                pl.BlockSpec((None, 1, SSD_INNER), lambda b, sw, s: (l, 0, 0)),
                pl.BlockSpec((None, 1, SSD_INNER), lambda b, sw, s: (l, 0, 0)),
                pl.BlockSpec((CHUNK, 3 * SSD_INNER), lambda b, sw, s: (0, 0)),
                pl.BlockSpec((CHUNK, SSD_HEADS * CHUNK), lambda b, sw, s: (0, 0))]
    scratch = [pltpu.VMEM((nk * SCAN_CHUNKS, SSD_STATE, SSD_INNER), BF16),
               pltpu.VMEM((SSD_STATE, SSD_INNER), F32),
               pltpu.VMEM((SSD_STATE, SSD_INNER), F32),
               pltpu.VMEM((CHUNK, SSD_INNER), F32),
               pltpu.VMEM((CHUNK, SSD_INNER), BF16),
               pltpu.VMEM((nk * SCAN_CHUNKS, 4 * SSD_HEADS, CHUNK), F32)]
    assert len(in_specs) == N_SSD_IN and len(scratch) == N_SSD_SCRATCH
    return in_specs, scratch


def _ret_scan_parts(q_ref, k_ref, v_ref, g_ref, cos_ref, sin_ref, dl_ref, o_ref,
                    hb_all, hf_s, hb_s, dec_s, tab_s, *, kpos):
    row, col, lower, upper = _tri_masks()
    lane_lo = col < RET_K_HEAD
    la = jax.nn.log_sigmoid(dl_ref[...])
    rowf = row.astype(F32)
    diff = (row - col).astype(F32)
    n_pairs = RET_HEADS // 2

    first16 = (col & 31) < 16

    def chunk_rows(c):
        return slice(c * CHUNK, (c + 1) * CHUNK)

    def rope(x, rc):
        swapped = jnp.where(first16, pltpu.roll(x, LANES - 16, 1), pltpu.roll(x, 16, 1))
        return x * cos_ref[rc, :] + swapped * sin_ref[rc, :]

    def head_scalar(d, h):
        return la[d:d + 1, h:h + 1]

    def pair_k(p, rc):
        return rope(k_ref[rc, p * LANES:(p + 1) * LANES].astype(F32), rc).astype(BF16)

    def state_update(h_s, d, p, kp, w_tab, rc):
        h0, h1 = 2 * p, 2 * p + 1
        v_of = lambda h: v_ref[rc, h * RET_V_HEAD:(h + 1) * RET_V_HEAD]
        vw = jnp.concatenate([(v_of(h0).astype(F32) * tab_s[w_tab, h0]).astype(BF16),
                              (v_of(h1).astype(F32) * tab_s[w_tab, h1]).astype(BF16)], axis=1)
        st = lax.dot_general(kp, vw, (((0,), (0,)), ((), ())), preferred_element_type=F32)
        top = row < RET_K_HEAD
        acc = jnp.where(top, st[:, 0:RET_V_HEAD], st[:, RET_V_HEAD:2 * RET_V_HEAD])
        dec = jnp.where(top, jnp.exp(float(CHUNK) * head_scalar(d, h0)),
                        jnp.exp(float(CHUNK) * head_scalar(d, h1)))
        rs = slice(p * LANES, (p + 1) * LANES)
        h_s[rs, :] = h_s[rs, :] * dec + acc

    def init_backward():
        hb_s[...] = jnp.zeros_like(hb_s)
        for h in range(RET_HEADS):
            laf = head_scalar(0, h)
            lab = head_scalar(1, h)
            dec_s[h] = (jnp.exp(jnp.where(lower, diff * laf, MASKED))
                        + jnp.exp(jnp.where(upper, -diff * lab, MASKED)))
            tab_s[0, h] = jnp.exp((rowf + 1.0) * laf)
            tab_s[1, h] = jnp.exp((float(CHUNK) - rowf) * lab)
            tab_s[2, h] = jnp.exp((float(CHUNK - 1) - rowf) * laf)
            tab_s[3, h] = jnp.exp(rowf * lab)

    def init_forward():
        hf_s[...] = jnp.zeros_like(hf_s)

    def backward_step():
        for c in reversed(range(SCAN_CHUNKS)):
            rc = chunk_rows(c)
            hb_all[kpos * SCAN_CHUNKS + c] = hb_s[...].astype(BF16)
            for p in range(n_pairs):
                state_update(hb_s, 1, p, pair_k(p, rc), 3, rc)

    def forward_step():
        for c in range(SCAN_CHUNKS):
            rc = chunk_rows(c)
            hb_in = hb_all[kpos * SCAN_CHUNKS + c]
            for p in range(n_pairs):
                h0, h1 = 2 * p, 2 * p + 1
                v_of = lambda h: v_ref[rc, h * RET_V_HEAD:(h + 1) * RET_V_HEAD]
                qp = rope(q_ref[rc, p * LANES:(p + 1) * LANES].astype(F32) * (RET_K_HEAD ** -0.5), rc)
                kp = pair_k(p, rc)
                rs = slice(p * LANES, (p + 1) * LANES)
                qcat = jnp.concatenate([jnp.where(lane_lo, qp, 0.0), jnp.where(lane_lo, 0.0, qp)],
                                       axis=0).astype(BF16)
                sc = lax.dot_general(qcat, kp, (((1,), (1,)), ((), ())), preferred_element_type=F32)
                hcat = jnp.concatenate([hf_s[rs, :].astype(BF16), hb_in[rs, :]], axis=1)
                gfb = jnp.dot(qcat, hcat, preferred_element_type=F32)
                m = jnp.concatenate([(sc[0:CHUNK] * dec_s[h0]).astype(BF16),
                                     (sc[CHUNK:2 * CHUNK] * dec_s[h1]).astype(BF16)], axis=1)
                zero = jnp.zeros((CHUNK, RET_V_HEAD), BF16)
                v_blk = jnp.concatenate([jnp.concatenate([v_of(h0), zero], axis=1),
                                         jnp.concatenate([zero, v_of(h1)], axis=1)], axis=0)
                y_intra = jnp.dot(m, v_blk, preferred_element_type=F32)
                for i, h in enumerate((h0, h1)):
                    vs = slice(h * RET_V_HEAD, (h + 1) * RET_V_HEAD)
                    g_h = gfb[i * CHUNK:(i + 1) * CHUNK]
                    y = (y_intra[:, i * RET_V_HEAD:(i + 1) * RET_V_HEAD]
                         + g_h[:, 0:RET_V_HEAD] * tab_s[0, h]
                         + g_h[:, RET_V_HEAD:2 * RET_V_HEAD] * tab_s[1, h])
                    mu = jnp.mean(y, axis=-1, keepdims=True)
                    yc = y - mu
                    var = jnp.mean(jnp.square(yc), axis=-1, keepdims=True)
                    yn = yc * lax.rsqrt(var + EPS)
                    o_ref[rc, vs] = (_silu(g_ref[rc, vs].astype(F32)) * yn).astype(BF16)
                state_update(hf_s, 0, p, kp, 2, rc)

    return init_backward, backward_step, init_forward, forward_step


def _ret_scan_specs(l, rows, in_row, kpos, nk):
    bw = BRANCH_WIDTH
    in_specs = [pl.BlockSpec((rows, RET_QK), lambda b, sw, s: (in_row(b, sw, s), U_RQ // RET_QK)),
                pl.BlockSpec((rows, RET_QK), lambda b, sw, s: (in_row(b, sw, s), U_RK // RET_QK)),
                pl.BlockSpec((rows, bw), lambda b, sw, s: (in_row(b, sw, s), U_RV // bw)),
                pl.BlockSpec((rows, bw), lambda b, sw, s: (in_row(b, sw, s), U_RG // bw)),
                pl.BlockSpec((rows, LANES), lambda b, sw, s: (kpos(sw, s), 0)),
                pl.BlockSpec((rows, LANES), lambda b, sw, s: (kpos(sw, s), 0)),
                pl.BlockSpec((None, 2, LANES), lambda b, sw, s: (l, 0, 0))]
    scratch = [pltpu.VMEM((nk * SCAN_CHUNKS, RET_QK, RET_V_HEAD), BF16),
               pltpu.VMEM((RET_QK, RET_V_HEAD), F32),
               pltpu.VMEM((RET_QK, RET_V_HEAD), F32),
               pltpu.VMEM((RET_HEADS, CHUNK, CHUNK), F32),
               pltpu.VMEM((4, RET_HEADS, CHUNK, LANES), F32)]
    assert len(in_specs) == N_RET_IN and len(scratch) == N_RET_SCRATCH
    return in_specs, scratch


N_POOL_IN, N_SCONV_IN = 7, 8


def _scans_kernel(*refs, ncc, ncl, nk, nb, bc, ctx, seq):
    bounds = [0]
    for n in (N_SSD_IN, N_RET_IN, N_POOL_IN, N_SCONV_IN):
        bounds.append(bounds[-1] + n)
    ssd_in, ret_in, pool_in, sconv_in = (refs[bounds[i]:bounds[i + 1]] for i in range(4))
    n_in = bounds[-1]
    ssd_out, ret_out, pool_out, sconv_out = refs[n_in:n_in + 4]
    scratch = refs[n_in + 4:]
    ssd_scr, ret_scr = scratch[:N_SSD_SCRATCH], scratch[N_SSD_SCRATCH:]
    sw = pl.program_id(1)
    s = pl.program_id(2)
    kpos = _scan_positions(ncc, nk)(sw, s)
    ssd = _ssd_scan_parts(*ssd_in, ssd_out, *ssd_scr, kpos=kpos)
    ret = _ret_scan_parts(*ret_in, ret_out, *ret_scr, kpos=kpos)
    r0 = _chunk_row(pl.program_id(0), kpos, ncc, ncl, nb) * (SCAN_CHUNKS * CHUNK)

    @pl.when(jnp.logical_and(sw == 0, s == 0))
    def _():
        ssd[0]()
        ret[0]()

    @pl.when(sw == 0)
    def _():
        _sconv_body(r0, *sconv_in, sconv_out, bc=bc, ctx=ctx, seq=seq)
        ssd[1]()
        _pool_body(r0, *pool_in, pool_out, bc=bc, ctx=ctx, seq=seq)
        ret[1]()

    @pl.when(jnp.logical_and(sw == 1, s == 0))
    def _():
        ssd[2]()
        ret[2]()

    @pl.when(sw == 1)
    def _():
        ssd[3]()
        ret[3]()


def _mixers_call(xbc, dt, u, dt_bias, a_log, d_exp, norm_w, cos_t, sin_t, decay_logit,
                 pool_w, pool_scale, sconv_w, l, nb, bc, ctx, seq):
    t = u.shape[0]
    rows = SCAN_CHUNKS * CHUNK
    assert rows == POOL_ROWS and SSD_INNER == BRANCH_WIDTH
    ncc, ncl = ctx // rows, seq // rows
    nk = ncc + ncl
    kpos = _scan_positions(ncc, nk)
    bw = BRANCH_WIDTH
    per = rows // HALO
    last = t // HALO - 1

    def in_row(b, sw, s):
        return _chunk_row(b, kpos(sw, s), ncc, ncl, nb)

    def out_row(b, sw, s):
        return _chunk_row(b, jnp.where(sw == 0, 0, s), ncc, ncl, nb)

    def once_row(b, sw, s):
        return in_row(b, 0, jnp.where(sw == 0, s, nk - 1))

    def halo(col):
        c = col // bw
        return [pl.BlockSpec((rows, bw), lambda b, sw, s: (once_row(b, sw, s), c)),
                pl.BlockSpec((HALO, bw), lambda b, sw, s: (jnp.maximum(once_row(b, sw, s) * per - 1, 0), c)),
                pl.BlockSpec((HALO, bw), lambda b, sw, s: (jnp.minimum((once_row(b, sw, s) + 1) * per, last), c))]

    ng = len(POOL_WINDOWS)
    ssd_in, ssd_scr = _ssd_scan_specs(l, rows, in_row, nk)
    ret_in, ret_scr = _ret_scan_specs(l, rows, in_row, kpos, nk)
    pool_in = halo(U_POOL) + [pl.BlockSpec((ng, rows, POOL_EXT), lambda b, sw, s: (0, 0, 0)),
                              pl.BlockSpec((ng, 4, rows, LANES), lambda b, sw, s: (0, 0, 0, 0)),
                              pl.BlockSpec((None, ng, POOL_GROUP, POOL_GROUP), lambda b, sw, s: (l, 0, 0, 0)),
                              pl.BlockSpec((None, 1, bw), lambda b, sw, s: (l, 0, 0))]
    sconv_in = ([pl.BlockSpec((rows, bw), lambda b, sw, s: (once_row(b, sw, s), U_SCB // bw))]
                + halo(U_SCC) + halo(U_SCX) + [pl.BlockSpec((None, 3, bw), lambda b, sw, s: (l, 0, 0))])
    assert len(pool_in) == N_POOL_IN and len(sconv_in) == N_SCONV_IN
    scan_out = pl.BlockSpec((rows, bw), lambda b, sw, s: (out_row(b, sw, s), 0))
    once_out = pl.BlockSpec((rows, bw), lambda b, sw, s: (once_row(b, sw, s), 0))
    return pl.pallas_call(
        functools.partial(_scans_kernel, ncc=ncc, ncl=ncl, nk=nk, nb=nb, bc=bc, ctx=ctx, seq=seq),
        grid=(nb, 2, nk),
        in_specs=ssd_in + ret_in + pool_in + sconv_in,
        out_specs=[scan_out, scan_out, once_out, once_out],
        out_shape=[jax.ShapeDtypeStruct((t, bw), BF16)] * 4,
        scratch_shapes=ssd_scr + ret_scr,
        compiler_params=_params(("parallel", "arbitrary", "arbitrary")),
        name="mixer_branches",
    )(xbc, dt, u, dt_bias, a_log, d_exp, norm_w, *_ssd_spread_constants(),
      u, u, u, u, cos_t, sin_t, decay_logit,
      u, u, u, _pool_bands(), _pool_counts(), pool_w, pool_scale,
      u, u, u, u, u, u, u, sconv_w)


def _rope_tables(ctx, seq):
    quarter = RET_K_HEAD // 4
    t = jnp.arange(seq)
    rowp = (t // GRID_W).astype(F32)
    colp = (t % GRID_W).astype(F32)
    inv = ROPE_BASE ** (-jnp.arange(quarter, dtype=F32) / quarter)
    ang_r = rowp[:, None] * inv[None, :]
    ang_c = colp[:, None] * inv[None, :]

    def blocks(ang):
        c, s = jnp.cos(ang), jnp.sin(ang)
        return jnp.concatenate([c, c], axis=-1), jnp.concatenate([-s, s], axis=-1)

    cr, sr = blocks(ang_r)
    cc, sc = blocks(ang_c)
    cos_h = jnp.concatenate([cr, cc], axis=-1)
    sin_h = jnp.concatenate([sr, sc], axis=-1)
    cos_l = jnp.concatenate([cos_h, cos_h], axis=-1)
    sin_l = jnp.concatenate([sin_h, sin_h], axis=-1)
    cos_t = jnp.concatenate([jnp.ones((ctx, LANES), F32), cos_l], axis=0)
    sin_t = jnp.concatenate([jnp.zeros((ctx, LANES), F32), sin_l], axis=0)
    return cos_t, sin_t


def _pad_lanes(a):
    return jnp.pad(a, [(0, 0)] * (a.ndim - 1) + [(0, LANES - a.shape[-1])])


def kernel(x, c, ctx, c_ctx, w_mod, b_mod, norm1_w, w_in, ssd_conv_w, ssd_conv_b, ssd_a_log,
           ssd_dt_bias, ssd_d, ssd_norm_w, pool_w, pool_scale, sconv_w, ret_decay_logit,
           w_branch, w_gate, b_gate, w_o, norm2_w, ffn_up, ffn_conv_w, ffn_conv_b, ffn_down,
           final_norm_w):
    nb, seq, d = x.shape
    nctx = ctx.shape[1]
    depth = w_mod.shape[0]
    assert d == D_MODEL and seq % GRID_W == 0
    assert nctx % POOL_ROWS == 0 and seq % POOL_ROWS == 0
    bc = nb * nctx
    t = bc + nb * seq
    tm = _pick((1024, 512, 256), bc, seq)
    rt = _pick((512, 256), bc, seq)

    def row_of(i):
        r = i * tm
        return jnp.where(r < bc, 0, 1 + (r - bc) // seq)

    sizes = (SSD_INNER, SSD_CONV_CH, SSD_HEADS, BRANCH_WIDTH, BRANCH_WIDTH, BRANCH_WIDTH,
             BRANCH_WIDTH, RET_QK, RET_QK, BRANCH_WIDTH, BRANCH_WIDTH)
    offs = [0]
    for sz in sizes:
        offs.append(offs[-1] + sz)
    parts = [w_in[:, :, offs[i]:offs[i + 1]] for i in range(len(sizes))]
    w_main = jnp.concatenate(parts[:2] + parts[3:], axis=-1).astype(BF16)
    w_dt = _pad_lanes(parts[2]).astype(BF16)
    assert w_main.shape[-1] == U_COLS
    wg_b = w_gate.astype(BF16)
    wb_b = w_branch.astype(BF16)
    wo_b = w_o.astype(BF16)
    up_b = ffn_up.astype(BF16)
    down_b = ffn_down.astype(BF16)
    poolw_b = pool_w.astype(BF16)

    nrows = 8 * ((1 + nb + 7) // 8)
    cvec = jnp.zeros((nrows, d), F32).at[0].set(c_ctx).at[1:1 + nb].set(c)
    mod = _mod_call(cvec, w_mod, b_mod)
    mod5 = mod.reshape(depth, nrows, 6, 1, d)

    cos_t, sin_t = _rope_tables(nctx, seq)
    rep = lambda a: jnp.broadcast_to(a.reshape(depth, 2 * SSD_HEADS, 1), (depth, 2 * SSD_HEADS, CHUNK))
    dt_bias_p = rep(ssd_dt_bias)
    a_log_p = rep(ssd_a_log)
    decay_p = _pad_lanes(ret_decay_logit)
    d_exp = jnp.repeat(ssd_d, SSD_HEAD_DIM, axis=-1)[:, None, :]

    r3 = lambda a: a[:, None, :]
    xs = jnp.concatenate([ctx.reshape(bc, d), x.reshape(nb * seq, d)], axis=0)
    for l in range(depth):
        u, h, dt = _in_proj_call(xs, r3(norm1_w), mod5, w_main, w_dt, l, 0, 1, tm, row_of)
        xbc = _ssd_conv_call(u, ssd_conv_w, r3(ssd_conv_b), l, rt, bc, nctx, seq)
        y_ssd, y_ret, y_pool, y_sc = _mixers_call(
            xbc, dt, u, dt_bias_p, a_log_p, d_exp, r3(ssd_norm_w), cos_t, sin_t, decay_p,
            poolw_b, r3(pool_scale), sconv_w, l, nb, bc, nctx, seq)
        lat = l == depth - 1
        bc_l = 0 if lat else bc
        row_l = _mod_row(tm, bc_l, seq)
        merged = _merge_call(h, (y_ssd, y_pool, y_sc, y_ret), wg_b, b_gate[:, :, None, :], wb_b, l, tm,
                             row0=bc if lat else 0)
        xs, h2 = _outproj_norm_call(merged, wo_b, xs, mod5, r3(norm2_w), l, 2, 3, 4, rt, bc, seq)
        gact = _ffn_gate_call(h2, up_b, ffn_conv_w, r3(ffn_conv_b), l, tm, bc_l, nctx, seq)
        xs = _res_call(gact, down_b, xs, mod5, l, 5, tm, row_l, "ffn_down_residual")
    out = _final_norm_call(xs, final_norm_w[None, :], 0, nb * seq, tm)
    return out.reshape(nb, seq, d)
```

```python
import functools

import jax
import jax.numpy as jnp
from jax import lax
from jax.experimental import pallas as pl
from jax.experimental.pallas import tpu as pltpu

F32 = jnp.float32
BF16 = jnp.bfloat16

D_MODEL = 2048
GRID_W = 64
EPS = 1e-6
CHUNK = 128
BRANCH_WIDTH = D_MODEL // 2
SSD_INNER = BRANCH_WIDTH
SSD_HEAD_DIM = 64
SSD_HEADS = SSD_INNER // SSD_HEAD_DIM
SSD_GROUPS = 4
SSD_STATE = 128
SSD_CONV_CH = SSD_INNER + 2 * SSD_GROUPS * SSD_STATE
POOL_WINDOWS = (2, 4, 8, 16)
POOL_GROUP = BRANCH_WIDTH // len(POOL_WINDOWS)
RET_HEADS = 8
RET_V_HEAD = BRANCH_WIDTH // RET_HEADS
RET_K_HEAD = RET_V_HEAD // 2
RET_QK = RET_HEADS * RET_K_HEAD
ROPE_BASE = 10000.0
D_FF = 256 * ((8 * D_MODEL // 3 + 255) // 256)

LANES = 128
HALO = 16
VMEM_LIMIT = 56 * 1024 * 1024

U_Z = 0
U_XBC = 1024
U_POOL = 3072
U_SCB = 4096
U_SCC = 5120
U_SCX = 6144
U_RQ = 7168
U_RK = 7680
U_RV = 8192
U_RG = 9216
U_COLS = 10240


def _params(sem):
    return pltpu.CompilerParams(dimension_semantics=sem, vmem_limit_bytes=VMEM_LIMIT)


def _silu(v):
    return v * jax.nn.sigmoid(v)


def _pick(cands, *ns):
    for c in cands:
        if all(n % c == 0 for n in ns):
            return c
    raise ValueError(f"no tile in {cands} divides {ns}")


def _mod_const(v, n):
    if n & (n - 1) == 0:
        return v & (n - 1)
    return lax.rem(v, n)


def _seq_pos(r0, rows, bc, ctx, seq):
    in_ctx = r0 < bc
    g = r0 + lax.broadcasted_iota(jnp.int32, (rows, 1), 0)
    pos = jnp.where(in_ctx, _mod_const(g, ctx), _mod_const(g - bc, seq))
    n = jnp.where(in_ctx, ctx, seq)
    return pos, n


def _mod_kernel(c_ref, w_ref, b_ref, o_ref):
    a = _silu(c_ref[...]).astype(BF16)
    o_ref[...] = jnp.dot(a, w_ref[...].astype(BF16), preferred_element_type=F32) + b_ref[...]


def _mod_call(cvec, w_mod, b_mod):
    nl, d, n6 = w_mod.shape
    r = cvec.shape[0]
    tn = 1024
    return pl.pallas_call(
        _mod_kernel,
        grid=(nl, n6 // tn),
        in_specs=[pl.BlockSpec((r, d), lambda l, j: (0, 0)),
                  pl.BlockSpec((None, d, tn), lambda l, j: (l, 0, j)),
                  pl.BlockSpec((None, 1, tn), lambda l, j: (l, 0, j))],
        out_specs=pl.BlockSpec((None, r, tn), lambda l, j: (l, 0, j)),
        out_shape=jax.ShapeDtypeStruct((nl, r, n6), F32),
        compiler_params=_params(("parallel", "parallel")),
        name="mod_vectors",
    )(cvec, w_mod, b_mod.reshape(nl, 1, n6))


def _normed(x, nw, sh, sc):
    y = x * lax.rsqrt(jnp.mean(jnp.square(x), axis=-1, keepdims=True) + EPS)
    return (y * nw) * (1.0 + sc) + sh


def _in_proj_kernel(x_ref, nw_ref, sh_ref, sc_ref, w_ref, wdt_ref, u_ref, h_ref, dt_ref):
    @pl.when(pl.program_id(1) == 0)
    def _():
        h = _normed(x_ref[...], nw_ref[...], sh_ref[...], sc_ref[...]).astype(BF16)
        h_ref[...] = h
        dt_ref[...] = jnp.dot(h, wdt_ref[...], preferred_element_type=F32)

    u_ref[...] = jnp.dot(h_ref[...], w_ref[...], preferred_element_type=F32).astype(BF16)


def _mod_spec(l, k, tn, row_fn, col_fn):
    return pl.BlockSpec((None, None, None, 1, tn),
                        lambda *g: (l, row_fn(*g), k, 0, col_fn(*g)))


def _mod_row(tile, bc, seq):
    def row(i):
        r = i * tile
        return jnp.where(r < bc, 0, 1 + (r - bc) // seq)
    return row


def _in_proj_call(x, norm_w, mod5, w, wdt, l, k_sh, k_sc, tm, row_of):
    t, d = x.shape
    n = w.shape[-1]
    tn = 1024
    zero = lambda i, j: 0
    row_of = (lambda f: lambda i, j: f(i))(row_of)
    return pl.pallas_call(
        _in_proj_kernel,
        grid=(t // tm, n // tn),
        in_specs=[pl.BlockSpec((tm, d), lambda i, j: (i, 0)),
                  pl.BlockSpec((None, 1, d), lambda i, j: (l, 0, 0)),
                  _mod_spec(l, k_sh, d, row_of, zero),
                  _mod_spec(l, k_sc, d, row_of, zero),
                  pl.BlockSpec((None, d, tn), lambda i, j: (l, 0, j)),
                  pl.BlockSpec((None, d, LANES), lambda i, j: (l, 0, 0))],
        out_specs=[pl.BlockSpec((tm, tn), lambda i, j: (i, j)),
                   pl.BlockSpec((tm, d), lambda i, j: (i, 0)),
                   pl.BlockSpec((tm, LANES), lambda i, j: (i, 0))],
        out_shape=[jax.ShapeDtypeStruct((t, n), BF16),
                   jax.ShapeDtypeStruct((t, d), BF16),
                   jax.ShapeDtypeStruct((t, LANES), F32)],
        compiler_params=_params(("parallel", "arbitrary")),
        name="in_proj",
    )(x, norm_w, mod5, mod5, w, wdt)


def _merge_kernel(h_ref, y0_ref, y1_ref, y2_ref, y3_ref, wg_ref, bg_ref, wb_ref, o_ref):
    h = h_ref[...]
    acc = None
    for i, y_ref in enumerate((y0_ref, y1_ref, y2_ref, y3_ref)):
        gate = jax.nn.sigmoid(jnp.dot(h, wg_ref[i], preferred_element_type=F32) + bg_ref[i])
        term = gate * jnp.dot(y_ref[...], wb_ref[i], preferred_element_type=F32)
        acc = term if acc is None else acc + term
    o_ref[...] = acc.astype(BF16)


def _merge_call(h, ys, wg, bg, wb, l, tm, row0=0):
    t, d = h.shape
    bw = ys[0].shape[-1]
    tn = 256
    nb = len(ys)
    off = row0 // tm
    t_out = t - row0
    return pl.pallas_call(
        _merge_kernel,
        grid=(t_out // tm, d // tn),
        in_specs=[pl.BlockSpec((tm, d), lambda i, j: (i + off, 0))]
                 + [pl.BlockSpec((tm, bw), lambda i, j: (i + off, 0)) for _ in ys]
                 + [pl.BlockSpec((None, nb, d, tn), lambda i, j: (l, 0, 0, j)),
                    pl.BlockSpec((None, nb, 1, tn), lambda i, j: (l, 0, 0, j)),
                    pl.BlockSpec((None, nb, bw, tn), lambda i, j: (l, 0, 0, j))],
        out_specs=pl.BlockSpec((tm, tn), lambda i, j: (i, j)),
        out_shape=jax.ShapeDtypeStruct((t_out, d), BF16),
        compiler_params=_params(("parallel", "arbitrary")),
        name="branch_merge",
    )(h, *ys, wg, bg, wb)


def _res_kernel(a_ref, w_ref, x_ref, g_ref, o_ref):
    o_ref[...] = x_ref[...] + g_ref[...] * jnp.dot(a_ref[...], w_ref[...],
                                                   preferred_element_type=F32)


def _res_call(a, w, x, mod5, l, k_gate, tm, row_of, name):
    t, kdim = a.shape
    d = x.shape[-1]
    tn = 512
    return pl.pallas_call(
        _res_kernel,
        grid=(t // tm, d // tn),
        in_specs=[pl.BlockSpec((tm, kdim), lambda i, j: (i, 0)),
                  pl.BlockSpec((None, kdim, tn), lambda i, j: (l, 0, j)),
                  pl.BlockSpec((tm, tn), lambda i, j: (i, j)),
                  _mod_spec(l, k_gate, tn, lambda i, j: row_of(i), lambda i, j: j)],
        out_specs=pl.BlockSpec((tm, tn), lambda i, j: (i, j)),
        out_shape=jax.ShapeDtypeStruct((t, d), F32),
        compiler_params=_params(("parallel", "arbitrary")),
        name=name,
    )(a, w, x, mod5)


EPI_ROWS = 128


def _outproj_norm_kernel(a_ref, w_ref, x_ref, g_ref, nw_ref, sh_ref, sc_ref, xo_ref, ho_ref, y0, y1):
    s = pl.program_id(0)
    tm = a_ref.shape[0]

    @pl.when(s == 0)
    def _():
        y1[...] = jnp.zeros_like(y1)

    def step(y_cur, y_prev):
        nslab = tm // EPI_ROWS
        cw = w_ref.shape[1] // nslab
        for k in range(nslab):
            rs = slice(k * EPI_ROWS, (k + 1) * EPI_ROWS)
            cs = slice(k * cw, (k + 1) * cw)
            y_cur[:, cs] = jnp.dot(a_ref[...], w_ref[:, cs], preferred_element_type=F32)
            xn = x_ref[rs, :] + g_ref[...] * y_prev[rs, :]
            xo_ref[rs, :] = xn
            ho_ref[rs, :] = _normed(xn, nw_ref[...], sh_ref[...], sc_ref[...]).astype(BF16)

    @pl.when(s % 2 == 0)
    def _():
        step(y0, y1)

    @pl.when(s % 2 == 1)
    def _():
        step(y1, y0)


def _outproj_norm_call(a, w, x, mod5, norm_w, l, k_gate, k_sh, k_sc, tm, bc, seq):
    t, kdim = a.shape
    d = x.shape[-1]
    ni = t // tm
    row0 = x.shape[0] - t
    assert row0 in (0, bc)
    off = row0 // tm
    row = _mod_row(tm, bc - row0, seq)
    cur = lambda s: jnp.minimum(s, ni - 1)
    prev = lambda s: jnp.maximum(s - 1, 0)
    zero = lambda s: 0
    return pl.pallas_call(
        _outproj_norm_kernel,
        grid=(ni + 1,),
        in_specs=[pl.BlockSpec((tm, kdim), lambda s: (cur(s), 0)),
                  pl.BlockSpec((None, kdim, d), lambda s: (l, 0, 0), pipeline_mode=pl.Buffered(1)),
                  pl.BlockSpec((tm, d), lambda s: (prev(s) + off, 0)),
                  _mod_spec(l, k_gate, d, lambda s: row(prev(s)), zero),
                  pl.BlockSpec((None, 1, d), lambda s: (l, 0, 0)),
                  _mod_spec(l, k_sh, d, lambda s: row(prev(s)), zero),
                  _mod_spec(l, k_sc, d, lambda s: row(prev(s)), zero)],
        out_specs=[pl.BlockSpec((tm, d), lambda s: (prev(s), 0)),
                   pl.BlockSpec((tm, d), lambda s: (prev(s), 0))],
        out_shape=[jax.ShapeDtypeStruct((t, d), F32),
                   jax.ShapeDtypeStruct((t, d), BF16)],
        scratch_shapes=[pltpu.VMEM((tm, d), F32), pltpu.VMEM((tm, d), F32)],
        compiler_params=_params(("arbitrary",)),
        name="out_proj_residual_norm",
    )(a, w, x, mod5, norm_w, mod5, mod5)


FFN_PIECE_ROWS = 64
FFN_SLABS = 4
SEQ_ALIGN = 256


def _ffn_gate_kernel(h_ref, hp_ref, hn_ref, wa_ref, wb_ref, cwa_ref, cwb_ref, ba_ref, bb_ref, o_ref,
                     h_s, a_s, b_s, *, nj, bc, ctx, seq):
    s = pl.program_id(0)
    tm = h_ref.shape[0]

    @pl.when(s == 0)
    def _():
        a_s[1] = jnp.zeros(a_s.shape[1:], F32)
        b_s[1] = jnp.zeros(b_s.shape[1:], F32)

    @pl.when(s % nj == 0)
    def _():
        h_s[0:HALO, :] = hp_ref[...]
        h_s[HALO:HALO + tm, :] = h_ref[...]
        h_s[HALO + tm:, :] = hn_ref[...]

    r0 = (jnp.maximum(s - 1, 0) // nj) * tm
    in_ctx = r0 < bc
    tn = o_ref.shape[1]
    rows_ext = tm + 2 * HALO
    pr = FFN_PIECE_ROWS
    row_first = lax.broadcasted_iota(jnp.int32, (pr, 1), 0) == 0
    row_last = lax.broadcasted_iota(jnp.int32, (pr, 1), 0) == pr - 1

    def seq_start(g):
        return jnp.where(in_ctx, _mod_const(g, ctx), _mod_const(g - bc, seq)) == 0

    def taps(w_ref, bias_ref, cs):
        return [jnp.broadcast_to(w_ref[j:j + 1, cs], (pr, LANES)) for j in range(3)] + \
               [jnp.broadcast_to(bias_ref[:, cs], (pr, LANES))]

    col_slabs = [slice(c, c + LANES) for c in range(0, tn, LANES)]
    taps_a = [taps(cwa_ref, ba_ref, cs) for cs in col_slabs]
    taps_b = [taps(cwb_ref, bb_ref, cs) for cs in col_slabs]

    def conv_piece(src, r, cs, tp, kill_prev, kill_next):
        base = HALO + r
        ext = src[base - 8:base + pr + 8, cs]
        xm1 = pltpu.roll(ext, 1, 0)[8:8 + pr]
        x00 = ext[8:8 + pr]
        xp1 = pltpu.roll(ext, pr + 15, 0)[8:8 + pr]
        if kill_prev is not None:
            xm1 = jnp.where(kill_prev, 0.0, xm1)
        if kill_next is not None:
            xp1 = jnp.where(kill_next, 0.0, xp1)
        return xm1 * tp[0] + x00 * tp[1] + xp1 * tp[2] + tp[3]

    def epilogue(a_prev, b_prev, lo, hi):
        for r in range(lo, hi, pr):
            kill_prev = kill_next = None
            if r % SEQ_ALIGN == 0:
                kill_prev = jnp.logical_and(row_first, seq_start(r0 + r))
            if (r + pr) % SEQ_ALIGN == 0:
                kill_next = jnp.logical_and(row_last, seq_start(r0 + r + pr))
            for ci, cs in enumerate(col_slabs):
                a = conv_piece(a_prev, r, cs, taps_a[ci], kill_prev, kill_next)
                b = conv_piece(b_prev, r, cs, taps_b[ci], kill_prev, kill_next)
                o_ref[r:r + pr, cs] = (_silu(a) * b).astype(BF16)

    def step(a_cur, b_cur, a_prev, b_prev):
        nslab = FFN_SLABS
        for k in range(nslab):
            m_lo = (k * rows_ext // nslab) // HALO * HALO
            m_hi = rows_ext if k == nslab - 1 else ((k + 1) * rows_ext // nslab) // HALO * HALO
            h = h_s[m_lo:m_hi, :]
            a_cur[m_lo:m_hi, :] = jnp.dot(h, wa_ref[...], preferred_element_type=F32)
            b_cur[m_lo:m_hi, :] = jnp.dot(h, wb_ref[...], preferred_element_type=F32)
            epilogue(a_prev, b_prev, k * tm // nslab, (k + 1) * tm // nslab)

    cur = s % 2
    step(a_s.at[cur], b_s.at[cur], a_s.at[1 - cur], b_s.at[1 - cur])


def _ffn_gate_call(h, w, cw, cb, l, tm, bc, ctx, seq):
    t, d = h.shape
    tn = 512
    nj = D_FF // tn
    ni = t // tm
    per = tm // HALO
    last = t // HALO - 1
    ci = lambda s: jnp.minimum(s // nj, ni - 1)
    cj = lambda s: s % nj
    pi = lambda s: jnp.maximum(s - 1, 0) // nj
    pj = lambda s: jnp.maximum(s - 1, 0) % nj
    return pl.pallas_call(
        functools.partial(_ffn_gate_kernel, nj=nj, bc=bc, ctx=ctx, seq=seq),
        grid=(ni * nj + 1,),
        in_specs=[pl.BlockSpec((tm, d), lambda s: (ci(s), 0)),
                  pl.BlockSpec((HALO, d), lambda s: (jnp.maximum(ci(s) * per - 1, 0), 0)),
                  pl.BlockSpec((HALO, d), lambda s: (jnp.minimum((ci(s) + 1) * per, last), 0)),
                  pl.BlockSpec((None, d, tn), lambda s: (l, 0, cj(s))),
                  pl.BlockSpec((None, d, tn), lambda s: (l, 0, nj + cj(s))),
                  pl.BlockSpec((None, 3, tn), lambda s: (l, 0, pj(s))),
                  pl.BlockSpec((None, 3, tn), lambda s: (l, 0, nj + pj(s))),
                  pl.BlockSpec((None, 1, tn), lambda s: (l, 0, pj(s))),
                  pl.BlockSpec((None, 1, tn), lambda s: (l, 0, nj + pj(s)))],
        out_specs=pl.BlockSpec((tm, tn), lambda s: (pi(s), pj(s))),
        out_shape=jax.ShapeDtypeStruct((t, D_FF), BF16),
        scratch_shapes=[pltpu.VMEM((tm + 2 * HALO, d), BF16)]
                       + [pltpu.VMEM((2, tm + 2 * HALO, tn), F32) for _ in range(2)],
        compiler_params=_params(("arbitrary",)),
        name="ffn_up_conv_gate",
    )(h, h, h, w, w, cw, cw, cb, cb)


def _final_norm_kernel(x_ref, w_ref, o_ref):
    x = x_ref[...]
    o_ref[...] = (x * lax.rsqrt(jnp.mean(jnp.square(x), axis=-1, keepdims=True) + EPS)) * w_ref[...]


def _final_norm_call(x, w, row0, rows, tm):
    d = x.shape[-1]
    off = row0 // tm
    return pl.pallas_call(
        _final_norm_kernel,
        grid=(rows // tm,),
        in_specs=[pl.BlockSpec((tm, d), lambda i: (i + off, 0)),
                  pl.BlockSpec((1, d), lambda i: (0, 0))],
        out_specs=pl.BlockSpec((tm, d), lambda i: (i, 0)),
        out_shape=jax.ShapeDtypeStruct((rows, d), F32),
        compiler_params=_params(("parallel",)),
        name="final_norm",
    )(x, w)


def _halo_specs(rt, cols, col_idx, t):
    per = rt // HALO
    last = t // HALO - 1
    return [pl.BlockSpec((rt, cols), lambda i, j: (i, col_idx(j))),
            pl.BlockSpec((HALO, cols), lambda i, j: (jnp.maximum(i * per - 1, 0), col_idx(j))),
            pl.BlockSpec((HALO, cols), lambda i, j: (jnp.minimum((i + 1) * per, last), col_idx(j)))]


def _shift_pm1(x, prev_row, next_row, pos, n):
    rt = x.shape[0]
    row = lax.broadcasted_iota(jnp.int32, (rt, 1), 0)
    xm1 = jnp.where(row == 0, prev_row, pltpu.roll(x, 1, 0))
    xm1 = jnp.where(pos == 0, 0.0, xm1)
    xp1 = jnp.where(row == rt - 1, next_row, pltpu.roll(x, rt - 1, 0))
    xp1 = jnp.where(pos == n - 1, 0.0, xp1)
    return xm1, xp1


def _conv3(x, prev_row, next_row, w, pos, n):
    xm1, xp1 = _shift_pm1(x, prev_row, next_row, pos, n)
    return xm1 * w[0:1] + x * w[1:2] + xp1 * w[2:3]


def _ssd_conv_kernel(x_ref, p_ref, n_ref, w_ref, b_ref, o_ref, *, bc, ctx, seq):
    rt = x_ref.shape[0]
    pos, n = _seq_pos(pl.program_id(0) * rt, rt, bc, ctx, seq)
    x = x_ref[...].astype(F32)
    prev_row = p_ref[...].astype(F32)[HALO - 1:HALO]
    next_row = n_ref[...].astype(F32)[0:1]
    y = _conv3(x, prev_row, next_row, w_ref[...], pos, n) + b_ref[...]
    o_ref[...] = _silu(y).astype(BF16)


def _ssd_conv_call(u, w, b, l, rt, bc, ctx, seq):
    t = u.shape[0]
    cols = 1024
    nj = SSD_CONV_CH // cols
    base = U_XBC // cols
    return pl.pallas_call(
        functools.partial(_ssd_conv_kernel, bc=bc, ctx=ctx, seq=seq),
        grid=(t // rt, nj),
        in_specs=_halo_specs(rt, cols, lambda j: base + j, t)
                 + [pl.BlockSpec((None, 3, cols), lambda i, j: (l, 0, j)),
                    pl.BlockSpec((None, 1, cols), lambda i, j: (l, 0, j))],
        out_specs=pl.BlockSpec((rt, cols), lambda i, j: (i, j)),
        out_shape=jax.ShapeDtypeStruct((t, SSD_CONV_CH), BF16),
        compiler_params=_params(("parallel", "parallel")),
        name="ssd_conv",
    )(u, u, u, w, b)


def _sconv_body(r0, b_ref, c_ref, cp_ref, cn_ref, x_ref, xp_ref, xn_ref, w_ref, o_ref, *, bc, ctx, seq):
    rt = b_ref.shape[0]
    pos, n = _seq_pos(r0, rt, bc, ctx, seq)
    cx = c_ref[...].astype(F32) * x_ref[...].astype(F32)
    prev_row = (cp_ref[...].astype(F32) * xp_ref[...].astype(F32))[HALO - 1:HALO]
    next_row = (cn_ref[...].astype(F32) * xn_ref[...].astype(F32))[0:1]
    y = b_ref[...].astype(F32) * _conv3(cx, prev_row, next_row, w_ref[...], pos, n)
    o_ref[...] = y.astype(BF16)


POOL_ROWS = 256
POOL_EXT = 384


def _pool_body(r0, x_ref, p_ref, n_ref, band_ref, cnt_ref, w_ref, s_ref, o_ref, *, bc, ctx, seq):
    rt = POOL_ROWS
    in_ctx = r0 < bc
    p0 = jnp.where(in_ctx, _mod_const(r0, ctx), _mod_const(r0 - bc, seq))
    n = jnp.where(in_ctx, ctx, seq)
    has_prev = p0 != 0
    has_next = p0 + rt != n
    zero = jnp.zeros((HALO, POOL_GROUP), BF16)
    pad = jnp.zeros((POOL_EXT - rt - 2 * HALO, POOL_GROUP), BF16)
    case = jnp.where(has_prev, 0, 2) + jnp.where(has_next, 0, 1)
    groups = [slice(gi * POOL_GROUP, (gi + 1) * POOL_GROUP) for gi in range(len(POOL_WINDOWS))]
    sums = []
    for gi, cs in enumerate(groups):
        ext = jnp.concatenate([jnp.where(has_prev, p_ref[:, cs], zero), x_ref[:, cs],
                               jnp.where(has_next, n_ref[:, cs], zero), pad], axis=0)
        sums.append(jnp.dot(band_ref[gi], ext, preferred_element_type=F32))
    for gi, cs in enumerate(groups):
        cnt = cnt_ref[gi, case]
        mean = sums[gi] / jnp.concatenate([cnt] * (POOL_GROUP // LANES), axis=1)
        pooled = (mean - x_ref[:, cs].astype(F32)).astype(BF16)
        y = jnp.dot(pooled, w_ref[gi], preferred_element_type=F32) * s_ref[:, cs]
        o_ref[:, cs] = y.astype(BF16)


def _pool_bands():
    t = jnp.arange(POOL_ROWS)[:, None]
    s = jnp.arange(POOL_EXT)[None, :] - HALO
    return jnp.stack([((s >= t - w // 2) & (s < t - w // 2 + w)) for w in POOL_WINDOWS]).astype(BF16)


def _pool_counts():
    t = jnp.arange(POOL_ROWS)
    tabs = []
    for w in POOL_WINDOWS:
        before = jnp.maximum(w // 2 - t, 0)
        after = jnp.maximum(t - w // 2 + w - POOL_ROWS, 0)
        cases = [w - (before if c >= 2 else 0) - (after if c % 2 else 0) for c in range(4)]
        tabs.append(jnp.stack([jnp.broadcast_to(c, (POOL_ROWS,)) for c in cases]))
    tab = jnp.stack(tabs).astype(F32)
    return jnp.broadcast_to(tab[..., None], tab.shape + (LANES,))


def _scan_positions(ncc, nk):
    def kpos(sw, s):
        back = jnp.where(s < ncc, ncc - 1 - s, nk - 1 - (s - ncc))
        return jnp.where(sw == 0, back, s)
    return kpos


def _chunk_row(b, k, ncc, ncl, nb):
    return jnp.where(k < ncc, b * ncc + k, nb * ncc + b * ncl + (k - ncc))


def _tri_masks():
    row = lax.broadcasted_iota(jnp.int32, (CHUNK, CHUNK), 0)
    col = lax.broadcasted_iota(jnp.int32, (CHUNK, CHUNK), 1)
    return row, col, col <= row, col >= row


MASKED = -1e30


SCAN_CHUNKS = 2


def _ssd_scan_parts(xbc_ref, dt_ref, z_ref, bias_ref, alog_ref, dexp_ref, nw_ref, sp_ref, sg_ref, o_ref,
                    hb_all, hf_s, hb_s, y_s, vw_s, pre_s, *, kpos):
    row, col, lower, upper = _tri_masks()
    lane_lo = col < SSD_HEAD_DIM
    row_lo1 = lane_lo[0:1]
    nh = SSD_HEADS
    n_pairs = nh // 2
    pairs_per_group = n_pairs // SSD_GROUPS
    gw = SSD_INNER // SSD_GROUPS

    def chunk_rows(c):
        return slice(c * CHUNK, (c + 1) * CHUNK)

    def split2(x):
        hi = x.astype(BF16).astype(F32)
        return hi, x - hi

    def split3(x):
        hi, r = split2(x)
        mid, lo = split2(r)
        return hi, mid, lo

    def to_tokens(parts):
        used = sum(p.shape[0] for p in parts)
        return jnp.concatenate(list(parts) + [jnp.zeros((CHUNK - used, CHUNK), F32)], axis=0).T.astype(BF16)

    def store_state_operand(wt, rc):
        lhs = to_tokens((jnp.zeros((4 * nh, CHUNK), F32),) + split2(wt))
        w_full = jnp.dot(lhs, sp_ref[:, 2 * SSD_INNER:3 * SSD_INNER], preferred_element_type=F32)
        vw_s[...] = (xbc_ref[rc, 0:SSD_INNER].astype(F32) * w_full).astype(BF16)

    def seg_all(cum):
        parts = split3(cum)
        ones = jnp.ones((nh, CHUNK), F32)
        a = to_tokens(parts + (ones, ones, ones))
        neg = jnp.concatenate([-p for p in parts], axis=0)
        tiled = jnp.concatenate([neg] * nh, axis=1)
        rows_b = jnp.where(sg_ref[3 * nh:6 * nh, :] != 0, tiled, 0.0).astype(BF16)
        b = jnp.concatenate([sg_ref[0:3 * nh, :], rows_b,
                             jnp.zeros((CHUNK - 6 * nh, nh * CHUNK), BF16)], axis=0)
        return jnp.dot(a, b, preferred_element_type=F32)

    def pair_scalar(x, p, j):
        return jnp.where(row_lo1, jnp.broadcast_to(x[2 * p:2 * p + 1, j:j + 1], (1, LANES)),
                         jnp.broadcast_to(x[2 * p + 1:2 * p + 2, j:j + 1], (1, LANES)))

    def state_update(h_s, g, dec_rows, rc):
        gs = slice(g * gw, (g + 1) * gw)
        bm = xbc_ref[rc, SSD_INNER + g * SSD_STATE:SSD_INNER + (g + 1) * SSD_STATE]
        st = lax.dot_general(bm, vw_s[:, gs], (((0,), (0,)), ((), ())), preferred_element_type=F32)
        h_s[:, gs] = h_s[:, gs] * jnp.concatenate(dec_rows, axis=1) + st

    def init_backward():
        hb_s[...] = jnp.zeros_like(hb_s)

    def init_forward():
        hf_s[...] = jnp.zeros_like(hf_s)

    def backward_step():
        for c in reversed(range(SCAN_CHUNKS)):
            rc = chunk_rows(c)
            ck = kpos * SCAN_CHUNKS + c
            dtt = dt_ref[rc, :].T[0:nh]
            dt2 = jax.nn.softplus(jnp.concatenate([dtt, dtt], axis=0) + bias_ref[...])
            la2 = dt2 * (-jnp.exp(alog_ref[...]))
            tri2 = jnp.concatenate([upper.astype(F32), lower.astype(F32)], axis=1)
            cum2 = jnp.dot(la2, tri2, precision=lax.Precision.HIGHEST, preferred_element_type=F32)
            cum_f = cum2[0:nh, 0:CHUNK]
            cum_b = cum2[nh:2 * nh, CHUNK:2 * CHUNK]
            pre_s[ck, 0:2 * nh] = dt2
            pre_s[ck, 2 * nh:3 * nh] = cum_f
            pre_s[ck, 3 * nh:4 * nh] = cum_b

            hb_all[ck] = hb_s[...].astype(BF16)
            store_state_operand(jnp.exp(cum_b[:, 0:1] - cum_b) * dt2[nh:2 * nh], rc)
            for g in range(SSD_GROUPS):
                dec_rows = [jnp.exp(pair_scalar(cum_b, g * pairs_per_group + pp, 0))
                            for pp in range(pairs_per_group)]
                state_update(hb_s, g, dec_rows, rc)

    def forward_step():
        for c in range(SCAN_CHUNKS):
            rc = chunk_rows(c)
            ck = kpos * SCAN_CHUNKS + c
            dt2 = pre_s[ck, 0:2 * nh]
            cum_f = pre_s[ck, 2 * nh:3 * nh]
            cum_b = pre_s[ck, 3 * nh:4 * nh]
            hb_in = hb_all[ck]
            wt = jnp.exp(cum_f[:, CHUNK - 1:CHUNK] - cum_f) * dt2[0:nh]
            ef, eb = split2(jnp.exp(cum_f)), split2(jnp.exp(cum_b))
            lhs = to_tokens((ef[0], eb[0], ef[1], eb[1]) + split2(wt))
            spread = jnp.dot(lhs, sp_ref[...], preferred_element_type=F32)
            vw_s[...] = (xbc_ref[rc, 0:SSD_INNER].astype(F32)
                         * spread[:, 2 * SSD_INNER:3 * SSD_INNER]).astype(BF16)
            seg_f = seg_all(cum_f)
            seg_b = seg_all(cum_b)
            for g in range(SSD_GROUPS):
                gs = slice(g * gw, (g + 1) * gw)
                bm = xbc_ref[rc, SSD_INNER + g * SSD_STATE:SSD_INNER + (g + 1) * SSD_STATE]
                cm = xbc_ref[rc, SSD_INNER + SSD_GROUPS * SSD_STATE + g * SSD_STATE:
                             SSD_INNER + SSD_GROUPS * SSD_STATE + (g + 1) * SSD_STATE]
                sc = lax.dot_general(cm, bm, (((1,), (1,)), ((), ())), preferred_element_type=F32)
                hcat = jnp.concatenate([hf_s[:, gs].astype(BF16), hb_in[:, gs]], axis=1)
                gfb = jnp.dot(cm, hcat, preferred_element_type=F32)
                dec_rows = []
                for pp in range(pairs_per_group):
                    p = g * pairs_per_group + pp
                    ps = slice(p * LANES, (p + 1) * LANES)
                    xs_pair = xbc_ref[rc, ps]
                    ms = []
                    for h in (2 * p, 2 * p + 1):
                        hs = slice(h * CHUNK, (h + 1) * CHUNK)
                        dec_f = jnp.exp(jnp.where(lower, seg_f[:, hs], MASKED))
                        dec_b = jnp.exp(jnp.where(upper, seg_b[:, hs], MASKED))
                        m = sc * (dec_f * dt2[h:h + 1] + dec_b * dt2[nh + h:nh + h + 1])
                        ms.append(m.astype(BF16))
                    zero = jnp.zeros_like(xs_pair)
                    x_split = jnp.concatenate([jnp.where(lane_lo, xs_pair, zero),
                                               jnp.where(lane_lo, zero, xs_pair)], axis=0)
                    y_intra = jnp.dot(jnp.concatenate(ms, axis=1), x_split, preferred_element_type=F32)
                    lo = pp * LANES
                    y_inter = (gfb[:, lo:lo + LANES] * spread[:, ps]
                               + gfb[:, gw + lo:gw + lo + LANES]
                               * spread[:, SSD_INNER + p * LANES:SSD_INNER + (p + 1) * LANES])
                    y_s[:, ps] = y_intra + y_inter
                    dec_rows.append(jnp.exp(pair_scalar(cum_f, p, CHUNK - 1)))
                state_update(hf_s, g, dec_rows, rc)

            xs32 = xbc_ref[rc, 0:SSD_INNER].astype(F32)
            y = (y_s[...] + xs32 * dexp_ref[...]) * _silu(z_ref[rc, :].astype(F32))
            y = y * lax.rsqrt(jnp.mean(jnp.square(y), axis=-1, keepdims=True) + EPS)
            o_ref[rc, :] = (y * nw_ref[...]).astype(BF16)

    return init_backward, backward_step, init_forward, forward_step


def _ssd_spread_constants():
    nh = SSD_HEADS
    r = jnp.arange(CHUNK)[:, None]
    c = jnp.arange(3 * SSD_INNER)[None, :]
    block_of_row = jnp.array([0, 1, 0, 1, 2, 2, -1, -1])[r // nh]
    sp = (block_of_row == c // SSD_INNER) & (r % nh == (c % SSD_INNER) // SSD_HEAD_DIM)
    c2 = jnp.arange(nh * CHUNK)[None, :]
    sg = (r < 6 * nh) & (r % nh == c2 // CHUNK)
    return sp.astype(BF16), sg.astype(BF16)


N_SSD_IN, N_SSD_SCRATCH = 9, 6
N_RET_IN, N_RET_SCRATCH = 7, 5


def _ssd_scan_specs(l, rows, in_row, nk):
    in_specs = [pl.BlockSpec((rows, SSD_CONV_CH), lambda b, sw, s: (in_row(b, sw, s), 0)),
                pl.BlockSpec((rows, LANES), lambda b, sw, s: (in_row(b, sw, s), 0)),
                pl.BlockSpec((rows, SSD_INNER), lambda b, sw, s: (in_row(b, sw, s), U_Z // SSD_INNER)),
                pl.BlockSpec((None, 2 * SSD_HEADS, CHUNK), lambda b, sw, s: (l, 0, 0)),
                pl.BlockSpec((None, 2 * SSD_HEADS, CHUNK), lambda b, sw, s: (l, 0, 0)),
                pl.BlockSpec((None, 1, SSD_INNER), lambda b, sw, s: (l, 0, 0)),
                pl.BlockSpec((None, 1, SSD_INNER), lambda b, sw, s: (l, 0, 0)),
                pl.BlockSpec((CHUNK, 3 * SSD_INNER), lambda b, sw, s: (0, 0)),
                pl.BlockSpec((CHUNK, SSD_HEADS * CHUNK), lambda b, sw, s: (0, 0))]
    scratch = [pltpu.VMEM((nk * SCAN_CHUNKS, SSD_STATE, SSD_INNER), BF16),
               pltpu.VMEM((SSD_STATE, SSD_INNER), F32),
               pltpu.VMEM((SSD_STATE, SSD_INNER), F32),
               pltpu.VMEM((CHUNK, SSD_INNER), F32),
               pltpu.VMEM((CHUNK, SSD_INNER), BF16),
               pltpu.VMEM((nk * SCAN_CHUNKS, 4 * SSD_HEADS, CHUNK), F32)]
    assert len(in_specs) == N_SSD_IN and len(scratch) == N_SSD_SCRATCH
    return in_specs, scratch


def _ret_scan_parts(q_ref, k_ref, v_ref, g_ref, cos_ref, sin_ref, dl_ref, o_ref,
                    hb_all, hf_s, hb_s, dec_s, tab_s, *, kpos):
    row, col, lower, upper = _tri_masks()
    lane_lo = col < RET_K_HEAD
    la = jax.nn.log_sigmoid(dl_ref[...])
    rowf = row.astype(F32)
    diff = (row - col).astype(F32)
    n_pairs = RET_HEADS // 2

    first16 = (col & 31) < 16

    def chunk_rows(c):
        return slice(c * CHUNK, (c + 1) * CHUNK)

    def rope(x, rc):
        swapped = jnp.where(first16, pltpu.roll(x, LANES - 16, 1), pltpu.roll(x, 16, 1))
        return x * cos_ref[rc, :] + swapped * sin_ref[rc, :]

    def head_scalar(d, h):
        return la[d:d + 1, h:h + 1]

    def pair_k(p, rc):
        return rope(k_ref[rc, p * LANES:(p + 1) * LANES].astype(F32), rc).astype(BF16)

    def state_update(h_s, d, p, kp, w_tab, rc):
        h0, h1 = 2 * p, 2 * p + 1
        v_of = lambda h: v_ref[rc, h * RET_V_HEAD:(h + 1) * RET_V_HEAD]
        vw = jnp.concatenate([(v_of(h0).astype(F32) * tab_s[w_tab, h0]).astype(BF16),
                              (v_of(h1).astype(F32) * tab_s[w_tab, h1]).astype(BF16)], axis=1)
        st = lax.dot_general(kp, vw, (((0,), (0,)), ((), ())), preferred_element_type=F32)
        top = row < RET_K_HEAD
        acc = jnp.where(top, st[:, 0:RET_V_HEAD], st[:, RET_V_HEAD:2 * RET_V_HEAD])
        dec = jnp.where(top, jnp.exp(float(CHUNK) * head_scalar(d, h0)),
                        jnp.exp(float(CHUNK) * head_scalar(d, h1)))
        rs = slice(p * LANES, (p + 1) * LANES)
        h_s[rs, :] = h_s[rs, :] * dec + acc

    def init_backward():
        hb_s[...] = jnp.zeros_like(hb_s)
        for h in range(RET_HEADS):
            laf = head_scalar(0, h)
            lab = head_scalar(1, h)
            dec_s[h] = (jnp.exp(jnp.where(lower, diff * laf, MASKED))
                        + jnp.exp(jnp.where(upper, -diff * lab, MASKED)))
            tab_s[0, h] = jnp.exp((rowf + 1.0) * laf)
            tab_s[1, h] = jnp.exp((float(CHUNK) - rowf) * lab)
            tab_s[2, h] = jnp.exp((float(CHUNK - 1) - rowf) * laf)
            tab_s[3, h] = jnp.exp(rowf * lab)

    def init_forward():
        hf_s[...] = jnp.zeros_like(hf_s)

    def backward_step():
        for c in reversed(range(SCAN_CHUNKS)):
            rc = chunk_rows(c)
            hb_all[kpos * SCAN_CHUNKS + c] = hb_s[...].astype(BF16)
            for p in range(n_pairs):
                state_update(hb_s, 1, p, pair_k(p, rc), 3, rc)

    def forward_step():
        for c in range(SCAN_CHUNKS):
            rc = chunk_rows(c)
            hb_in = hb_all[kpos * SCAN_CHUNKS + c]
            for p in range(n_pairs):
                h0, h1 = 2 * p, 2 * p + 1
                v_of = lambda h: v_ref[rc, h * RET_V_HEAD:(h + 1) * RET_V_HEAD]
                qp = rope(q_ref[rc, p * LANES:(p + 1) * LANES].astype(F32) * (RET_K_HEAD ** -0.5), rc)
                kp = pair_k(p, rc)
                rs = slice(p * LANES, (p + 1) * LANES)
                qcat = jnp.concatenate([jnp.where(lane_lo, qp, 0.0), jnp.where(lane_lo, 0.0, qp)],
                                       axis=0).astype(BF16)
                sc = lax.dot_general(qcat, kp, (((1,), (1,)), ((), ())), preferred_element_type=F32)
                hcat = jnp.concatenate([hf_s[rs, :].astype(BF16), hb_in[rs, :]], axis=1)
                gfb = jnp.dot(qcat, hcat, preferred_element_type=F32)
                m = jnp.concatenate([(sc[0:CHUNK] * dec_s[h0]).astype(BF16),
                                     (sc[CHUNK:2 * CHUNK] * dec_s[h1]).astype(BF16)], axis=1)
                zero = jnp.zeros((CHUNK, RET_V_HEAD), BF16)
                v_blk = jnp.concatenate([jnp.concatenate([v_of(h0), zero], axis=1),
                                         jnp.concatenate([zero, v_of(h1)], axis=1)], axis=0)
                y_intra = jnp.dot(m, v_blk, preferred_element_type=F32)
                for i, h in enumerate((h0, h1)):
                    vs = slice(h * RET_V_HEAD, (h + 1) * RET_V_HEAD)
                    g_h = gfb[i * CHUNK:(i + 1) * CHUNK]
                    y = (y_intra[:, i * RET_V_HEAD:(i + 1) * RET_V_HEAD]
                         + g_h[:, 0:RET_V_HEAD] * tab_s[0, h]
                         + g_h[:, RET_V_HEAD:2 * RET_V_HEAD] * tab_s[1, h])
                    mu = jnp.mean(y, axis=-1, keepdims=True)
                    yc = y - mu
                    var = jnp.mean(jnp.square(yc), axis=-1, keepdims=True)
                    yn = yc * lax.rsqrt(var + EPS)
                    o_ref[rc, vs] = (_silu(g_ref[rc, vs].astype(F32)) * yn).astype(BF16)
                state_update(hf_s, 0, p, kp, 2, rc)

    return init_backward, backward_step, init_forward, forward_step


def _ret_scan_specs(l, rows, in_row, kpos, nk):
    bw = BRANCH_WIDTH
    in_specs = [pl.BlockSpec((rows, RET_QK), lambda b, sw, s: (in_row(b, sw, s), U_RQ // RET_QK)),
                pl.BlockSpec((rows, RET_QK), lambda b, sw, s: (in_row(b, sw, s), U_RK // RET_QK)),
                pl.BlockSpec((rows, bw), lambda b, sw, s: (in_row(b, sw, s), U_RV // bw)),
                pl.BlockSpec((rows, bw), lambda b, sw, s: (in_row(b, sw, s), U_RG // bw)),
                pl.BlockSpec((rows, LANES), lambda b, sw, s: (kpos(sw, s), 0)),
                pl.BlockSpec((rows, LANES), lambda b, sw, s: (kpos(sw, s), 0)),
                pl.BlockSpec((None, 2, LANES), lambda b, sw, s: (l, 0, 0))]
    scratch = [pltpu.VMEM((nk * SCAN_CHUNKS, RET_QK, RET_V_HEAD), BF16),
               pltpu.VMEM((RET_QK, RET_V_HEAD), F32),
               pltpu.VMEM((RET_QK, RET_V_HEAD), F32),
               pltpu.VMEM((RET_HEADS, CHUNK, CHUNK), F32),
               pltpu.VMEM((4, RET_HEADS, CHUNK, LANES), F32)]
    assert len(in_specs) == N_RET_IN and len(scratch) == N_RET_SCRATCH
    return in_specs, scratch


N_POOL_IN, N_SCONV_IN = 7, 8


def _scans_kernel(*refs, ncc, ncl, nk, nb, bc, ctx, seq):
    bounds = [0]
    for n in (N_SSD_IN, N_RET_IN, N_POOL_IN, N_SCONV_IN):
        bounds.append(bounds[-1] + n)
    ssd_in, ret_in, pool_in, sconv_in = (refs[bounds[i]:bounds[i + 1]] for i in range(4))
    n_in = bounds[-1]
    ssd_out, ret_out, pool_out, sconv_out = refs[n_in:n_in + 4]
    scratch = refs[n_in + 4:]
    ssd_scr, ret_scr = scratch[:N_SSD_SCRATCH], scratch[N_SSD_SCRATCH:]
    sw = pl.program_id(1)
    s = pl.program_id(2)
    kpos = _scan_positions(ncc, nk)(sw, s)
    ssd = _ssd_scan_parts(*ssd_in, ssd_out, *ssd_scr, kpos=kpos)
    ret = _ret_scan_parts(*ret_in, ret_out, *ret_scr, kpos=kpos)
    r0 = _chunk_row(pl.program_id(0), kpos, ncc, ncl, nb) * (SCAN_CHUNKS * CHUNK)

    @pl.when(jnp.logical_and(sw == 0, s == 0))
    def _():
        ssd[0]()
        ret[0]()

    @pl.when(sw == 0)
    def _():
        _sconv_body(r0, *sconv_in, sconv_out, bc=bc, ctx=ctx, seq=seq)
        ssd[1]()
        _pool_body(r0, *pool_in, pool_out, bc=bc, ctx=ctx, seq=seq)
        ret[1]()

    @pl.when(jnp.logical_and(sw == 1, s == 0))
    def _():
        ssd[2]()
        ret[2]()

    @pl.when(sw == 1)
    def _():
        ssd[3]()
        ret[3]()


def _mixers_call(xbc, dt, u, dt_bias, a_log, d_exp, norm_w, cos_t, sin_t, decay_logit,
                 pool_w, pool_scale, sconv_w, l, nb, bc, ctx, seq):
    t = u.shape[0]
    rows = SCAN_CHUNKS * CHUNK
    assert rows == POOL_ROWS and SSD_INNER == BRANCH_WIDTH
    ncc, ncl = ctx // rows, seq // rows
    nk = ncc + ncl
    kpos = _scan_positions(ncc, nk)
    bw = BRANCH_WIDTH
    per = rows // HALO
    last = t // HALO - 1

    def in_row(b, sw, s):
        return _chunk_row(b, kpos(sw, s), ncc, ncl, nb)

    def out_row(b, sw, s):
        return _chunk_row(b, jnp.where(sw == 0, 0, s), ncc, ncl, nb)

    def once_row(b, sw, s):
        return in_row(b, 0, jnp.where(sw == 0, s, nk - 1))

    def halo(col):
        c = col // bw
        return [pl.BlockSpec((rows, bw), lambda b, sw, s: (once_row(b, sw, s), c)),
                pl.BlockSpec((HALO, bw), lambda b, sw, s: (jnp.maximum(once_row(b, sw, s) * per - 1, 0), c)),
                pl.BlockSpec((HALO, bw), lambda b, sw, s: (jnp.minimum((once_row(b, sw, s) + 1) * per, last), c))]

    ng = len(POOL_WINDOWS)
    ssd_in, ssd_scr = _ssd_scan_specs(l, rows, in_row, nk)
    ret_in, ret_scr = _ret_scan_specs(l, rows, in_row, kpos, nk)
    pool_in = halo(U_POOL) + [pl.BlockSpec((ng, rows, POOL_EXT), lambda b, sw, s: (0, 0, 0)),
                              pl.BlockSpec((ng, 4, rows, LANES), lambda b, sw, s: (0, 0, 0, 0)),
                              pl.BlockSpec((None, ng, POOL_GROUP, POOL_GROUP), lambda b, sw, s: (l, 0, 0, 0)),
                              pl.BlockSpec((None, 1, bw), lambda b, sw, s: (l, 0, 0))]
    sconv_in = ([pl.BlockSpec((rows, bw), lambda b, sw, s: (once_row(b, sw, s), U_SCB // bw))]
                + halo(U_SCC) + halo(U_SCX) + [pl.BlockSpec((None, 3, bw), lambda b, sw, s: (l, 0, 0))])
    assert len(pool_in) == N_POOL_IN and len(sconv_in) == N_SCONV_IN
    scan_out = pl.BlockSpec((rows, bw), lambda b, sw, s: (out_row(b, sw, s), 0))
    once_out = pl.BlockSpec((rows, bw), lambda b, sw, s: (once_row(b, sw, s), 0))
    return pl.pallas_call(
        functools.partial(_scans_kernel, ncc=ncc, ncl=ncl, nk=nk, nb=nb, bc=bc, ctx=ctx, seq=seq),
        grid=(nb, 2, nk),
        in_specs=ssd_in + ret_in + pool_in + sconv_in,
        out_specs=[scan_out, scan_out, once_out, once_out],
        out_shape=[jax.ShapeDtypeStruct((t, bw), BF16)] * 4,
        scratch_shapes=ssd_scr + ret_scr,
        compiler_params=_params(("parallel", "arbitrary", "arbitrary")),
        name="mixer_branches",
    )(xbc, dt, u, dt_bias, a_log, d_exp, norm_w, *_ssd_spread_constants(),
      u, u, u, u, cos_t, sin_t, decay_logit,
      u, u, u, _pool_bands(), _pool_counts(), pool_w, pool_scale,
      u, u, u, u, u, u, u, sconv_w)


def _rope_tables(ctx, seq):
    quarter = RET_K_HEAD // 4
    t = jnp.arange(seq)
    rowp = (t // GRID_W).astype(F32)
    colp = (t % GRID_W).astype(F32)
    inv = ROPE_BASE ** (-jnp.arange(quarter, dtype=F32) / quarter)
    ang_r = rowp[:, None] * inv[None, :]
    ang_c = colp[:, None] * inv[None, :]

    def blocks(ang):
        c, s = jnp.cos(ang), jnp.sin(ang)
        return jnp.concatenate([c, c], axis=-1), jnp.concatenate([-s, s], axis=-1)

    cr, sr = blocks(ang_r)
    cc, sc = blocks(ang_c)
    cos_h = jnp.concatenate([cr, cc], axis=-1)
    sin_h = jnp.concatenate([sr, sc], axis=-1)
    cos_l = jnp.concatenate([cos_h, cos_h], axis=-1)
    sin_l = jnp.concatenate([sin_h, sin_h], axis=-1)
    cos_t = jnp.concatenate([jnp.ones((ctx, LANES), F32), cos_l], axis=0)
    sin_t = jnp.concatenate([jnp.zeros((ctx, LANES), F32), sin_l], axis=0)
    return cos_t, sin_t


def _pad_lanes(a):
    return jnp.pad(a, [(0, 0)] * (a.ndim - 1) + [(0, LANES - a.shape[-1])])


def kernel(x, c, ctx, c_ctx, w_mod, b_mod, norm1_w, w_in, ssd_conv_w, ssd_conv_b, ssd_a_log,
           ssd_dt_bias, ssd_d, ssd_norm_w, pool_w, pool_scale, sconv_w, ret_decay_logit,
           w_branch, w_gate, b_gate, w_o, norm2_w, ffn_up, ffn_conv_w, ffn_conv_b, ffn_down,
           final_norm_w):
    nb, seq, d = x.shape
    nctx = ctx.shape[1]
    depth = w_mod.shape[0]
    assert d == D_MODEL and seq % GRID_W == 0
    assert nctx % POOL_ROWS == 0 and seq % POOL_ROWS == 0
    bc = nb * nctx
    t = bc + nb * seq
    tm = _pick((1024, 512, 256), bc, seq)
    rt = _pick((512, 256), bc, seq)

    def row_of(i):
        r = i * tm
        return jnp.where(r < bc, 0, 1 + (r - bc) // seq)

    sizes = (SSD_INNER, SSD_CONV_CH, SSD_HEADS, BRANCH_WIDTH, BRANCH_WIDTH, BRANCH_WIDTH,
             BRANCH_WIDTH, RET_QK, RET_QK, BRANCH_WIDTH, BRANCH_WIDTH)
    offs = [0]
    for sz in sizes:
        offs.append(offs[-1] + sz)
    parts = [w_in[:, :, offs[i]:offs[i + 1]] for i in range(len(sizes))]
    w_main = jnp.concatenate(parts[:2] + parts[3:], axis=-1).astype(BF16)
    w_dt = _pad_lanes(parts[2]).astype(BF16)
    assert w_main.shape[-1] == U_COLS
    wg_b = w_gate.astype(BF16)
    wb_b = w_branch.astype(BF16)
    wo_b = w_o.astype(BF16)
    up_b = ffn_up.astype(BF16)
    down_b = ffn_down.astype(BF16)
    poolw_b = pool_w.astype(BF16)

    nrows = 8 * ((1 + nb + 7) // 8)
    cvec = jnp.zeros((nrows, d), F32).at[0].set(c_ctx).at[1:1 + nb].set(c)
    mod = _mod_call(cvec, w_mod, b_mod)
    mod5 = mod.reshape(depth, nrows, 6, 1, d)

    cos_t, sin_t = _rope_tables(nctx, seq)
    rep = lambda a: jnp.broadcast_to(a.reshape(depth, 2 * SSD_HEADS, 1), (depth, 2 * SSD_HEADS, CHUNK))
    dt_bias_p = rep(ssd_dt_bias)
    a_log_p = rep(ssd_a_log)
    decay_p = _pad_lanes(ret_decay_logit)
    d_exp = jnp.repeat(ssd_d, SSD_HEAD_DIM, axis=-1)[:, None, :]

    r3 = lambda a: a[:, None, :]
    xs = jnp.concatenate([ctx.reshape(bc, d), x.reshape(nb * seq, d)], axis=0)
    for l in range(depth):
        u, h, dt = _in_proj_call(xs, r3(norm1_w), mod5, w_main, w_dt, l, 0, 1, tm, row_of)
        xbc = _ssd_conv_call(u, ssd_conv_w, r3(ssd_conv_b), l, rt, bc, nctx, seq)
        y_ssd, y_ret, y_pool, y_sc = _mixers_call(
            xbc, dt, u, dt_bias_p, a_log_p, d_exp, r3(ssd_norm_w), cos_t, sin_t, decay_p,
            poolw_b, r3(pool_scale), sconv_w, l, nb, bc, nctx, seq)
        lat = l == depth - 1
        bc_l = 0 if lat else bc
        row_l = _mod_row(tm, bc_l, seq)
        merged = _merge_call(h, (y_ssd, y_pool, y_sc, y_ret), wg_b, b_gate[:, :, None, :], wb_b, l, tm,
                             row0=bc if lat else 0)
        xs, h2 = _outproj_norm_call(merged, wo_b, xs, mod5, r3(norm2_w), l, 2, 3, 4, rt, bc, seq)
        gact = _ffn_gate_call(h2, up_b, ffn_conv_w, r3(ffn_conv_b), l, tm, bc_l, nctx, seq)
        xs = _res_call(gact, down_b, xs, mod5, l, 5, tm, row_l, "ffn_down_residual")
    out = _final_norm_call(xs, final_norm_w[None, :], 0, nb * seq, tm)
    return out.reshape(nb, seq, d)
```

```python
import functools

import jax
import jax.numpy as jnp
from jax import lax
from jax.experimental import pallas as pl
from jax.experimental.pallas import tpu as pltpu

F32 = jnp.float32
BF16 = jnp.bfloat16

D_MODEL = 2048
GRID_W = 64
EPS = 1e-6
CHUNK = 128
BRANCH_WIDTH = D_MODEL // 2
SSD_INNER = BRANCH_WIDTH
SSD_HEAD_DIM = 64
SSD_HEADS = SSD_INNER // SSD_HEAD_DIM
SSD_GROUPS = 4
SSD_STATE = 128
SSD_CONV_CH = SSD_INNER + 2 * SSD_GROUPS * SSD_STATE
POOL_WINDOWS = (2, 4, 8, 16)
POOL_GROUP = BRANCH_WIDTH // len(POOL_WINDOWS)
RET_HEADS = 8
RET_V_HEAD = BRANCH_WIDTH // RET_HEADS
RET_K_HEAD = RET_V_HEAD // 2
RET_QK = RET_HEADS * RET_K_HEAD
ROPE_BASE = 10000.0
D_FF = 256 * ((8 * D_MODEL // 3 + 255) // 256)

LANES = 128
HALO = 16
VMEM_LIMIT = 56 * 1024 * 1024

U_Z = 0
U_XBC = 1024
U_POOL = 3072
U_SCB = 4096
U_SCC = 5120
U_SCX = 6144
U_RQ = 7168
U_RK = 7680
U_RV = 8192
U_RG = 9216
U_COLS = 10240


def _params(sem):
    return pltpu.CompilerParams(dimension_semantics=sem, vmem_limit_bytes=VMEM_LIMIT)


def _silu(v):
    return v * jax.nn.sigmoid(v)


def _pick(cands, *ns):
    for c in cands:
        if all(n % c == 0 for n in ns):
            return c
    raise ValueError(f"no tile in {cands} divides {ns}")


def _mod_const(v, n):
    if n & (n - 1) == 0:
        return v & (n - 1)
    return lax.rem(v, n)


def _seq_pos(r0, rows, bc, ctx, seq):
    in_ctx = r0 < bc
    g = r0 + lax.broadcasted_iota(jnp.int32, (rows, 1), 0)
    pos = jnp.where(in_ctx, _mod_const(g, ctx), _mod_const(g - bc, seq))
    n = jnp.where(in_ctx, ctx, seq)
    return pos, n


def _mod_kernel(c_ref, w_ref, b_ref, o_ref):
    a = _silu(c_ref[...]).astype(BF16)
    o_ref[...] = jnp.dot(a, w_ref[...].astype(BF16), preferred_element_type=F32) + b_ref[...]


def _mod_call(cvec, w_mod, b_mod):
    nl, d, n6 = w_mod.shape
    r = cvec.shape[0]
    tn = 1024
    return pl.pallas_call(
        _mod_kernel,
        grid=(nl, n6 // tn),
        in_specs=[pl.BlockSpec((r, d), lambda l, j: (0, 0)),
                  pl.BlockSpec((None, d, tn), lambda l, j: (l, 0, j)),
                  pl.BlockSpec((None, 1, tn), lambda l, j: (l, 0, j))],
        out_specs=pl.BlockSpec((None, r, tn), lambda l, j: (l, 0, j)),
        out_shape=jax.ShapeDtypeStruct((nl, r, n6), F32),
        compiler_params=_params(("parallel", "parallel")),
        name="mod_vectors",
    )(cvec, w_mod, b_mod.reshape(nl, 1, n6))


def _normed(x, nw, sh, sc):
    y = x * lax.rsqrt(jnp.mean(jnp.square(x), axis=-1, keepdims=True) + EPS)
    return (y * nw) * (1.0 + sc) + sh


def _in_proj_kernel(x_ref, nw_ref, sh_ref, sc_ref, w_ref, wdt_ref, u_ref, h_ref, dt_ref):
    @pl.when(pl.program_id(1) == 0)
    def _():
        h = _normed(x_ref[...], nw_ref[...], sh_ref[...], sc_ref[...]).astype(BF16)
        h_ref[...] = h
        dt_ref[...] = jnp.dot(h, wdt_ref[...], preferred_element_type=F32)

    u_ref[...] = jnp.dot(h_ref[...], w_ref[...], preferred_element_type=F32).astype(BF16)


def _mod_spec(l, k, tn, row_fn, col_fn):
    return pl.BlockSpec((None, None, None, 1, tn),
                        lambda *g: (l, row_fn(*g), k, 0, col_fn(*g)))


def _mod_row(tile, bc, seq):
    def row(i):
        r = i * tile
        return jnp.where(r < bc, 0, 1 + (r - bc) // seq)
    return row


def _in_proj_call(x, norm_w, mod5, w, wdt, l, k_sh, k_sc, tm, row_of):
    t, d = x.shape
    n = w.shape[-1]
    tn = 1024
    zero = lambda i, j: 0
    row_of = (lambda f: lambda i, j: f(i))(row_of)
    return pl.pallas_call(
        _in_proj_kernel,
        grid=(t // tm, n // tn),
        in_specs=[pl.BlockSpec((tm, d), lambda i, j: (i, 0)),
                  pl.BlockSpec((None, 1, d), lambda i, j: (l, 0, 0)),
                  _mod_spec(l, k_sh, d, row_of, zero),
                  _mod_spec(l, k_sc, d, row_of, zero),
                  pl.BlockSpec((None, d, tn), lambda i, j: (l, 0, j)),
                  pl.BlockSpec((None, d, LANES), lambda i, j: (l, 0, 0))],
        out_specs=[pl.BlockSpec((tm, tn), lambda i, j: (i, j)),
                   pl.BlockSpec((tm, d), lambda i, j: (i, 0)),
                   pl.BlockSpec((tm, LANES), lambda i, j: (i, 0))],
        out_shape=[jax.ShapeDtypeStruct((t, n), BF16),
                   jax.ShapeDtypeStruct((t, d), BF16),
                   jax.ShapeDtypeStruct((t, LANES), F32)],
        compiler_params=_params(("parallel", "arbitrary")),
        name="in_proj",
    )(x, norm_w, mod5, mod5, w, wdt)


def _merge_kernel(h_ref, y0_ref, y1_ref, y2_ref, y3_ref, wg_ref, bg_ref, wb_ref, o_ref):
    h = h_ref[...]
    acc = None
    for i, y_ref in enumerate((y0_ref, y1_ref, y2_ref, y3_ref)):
        gate = jax.nn.sigmoid(jnp.dot(h, wg_ref[i], preferred_element_type=F32) + bg_ref[i])
        term = gate * jnp.dot(y_ref[...], wb_ref[i], preferred_element_type=F32)
        acc = term if acc is None else acc + term
    o_ref[...] = acc.astype(BF16)


def _merge_call(h, ys, wg, bg, wb, l, tm, row0=0):
    t, d = h.shape
    bw = ys[0].shape[-1]
    tn = 256
    nb = len(ys)
    off = row0 // tm
    t_out = t - row0
    return pl.pallas_call(
        _merge_kernel,
        grid=(t_out // tm, d // tn),
        in_specs=[pl.BlockSpec((tm, d), lambda i, j: (i + off, 0))]
                 + [pl.BlockSpec((tm, bw), lambda i, j: (i + off, 0)) for _ in ys]
                 + [pl.BlockSpec((None, nb, d, tn), lambda i, j: (l, 0, 0, j)),
                    pl.BlockSpec((None, nb, 1, tn), lambda i, j: (l, 0, 0, j)),
                    pl.BlockSpec((None, nb, bw, tn), lambda i, j: (l, 0, 0, j))],
        out_specs=pl.BlockSpec((tm, tn), lambda i, j: (i, j)),
        out_shape=jax.ShapeDtypeStruct((t_out, d), BF16),
        compiler_params=_params(("parallel", "arbitrary")),
        name="branch_merge",
    )(h, *ys, wg, bg, wb)


def _res_kernel(a_ref, w_ref, x_ref, g_ref, o_ref):
    o_ref[...] = x_ref[...] + g_ref[...] * jnp.dot(a_ref[...], w_ref[...],
                                                   preferred_element_type=F32)


def _res_call(a, w, x, mod5, l, k_gate, tm, row_of, name):
    t, kdim = a.shape
    d = x.shape[-1]
    tn = 512
    return pl.pallas_call(
        _res_kernel,
        grid=(t // tm, d // tn),
        in_specs=[pl.BlockSpec((tm, kdim), lambda i, j: (i, 0)),
                  pl.BlockSpec((None, kdim, tn), lambda i, j: (l, 0, j)),
                  pl.BlockSpec((tm, tn), lambda i, j: (i, j)),
                  _mod_spec(l, k_gate, tn, lambda i, j: row_of(i), lambda i, j: j)],
        out_specs=pl.BlockSpec((tm, tn), lambda i, j: (i, j)),
        out_shape=jax.ShapeDtypeStruct((t, d), F32),
        compiler_params=_params(("parallel", "arbitrary")),
        name=name,
    )(a, w, x, mod5)


EPI_ROWS = 128


def _outproj_norm_kernel(a_ref, w_ref, x_ref, g_ref, nw_ref, sh_ref, sc_ref, xo_ref, ho_ref, y0, y1):
    s = pl.program_id(0)
    tm = a_ref.shape[0]

    @pl.when(s == 0)
    def _():
        y1[...] = jnp.zeros_like(y1)

    def step(y_cur, y_prev):
        nslab = tm // EPI_ROWS
        cw = w_ref.shape[1] // nslab
        for k in range(nslab):
            rs = slice(k * EPI_ROWS, (k + 1) * EPI_ROWS)
            cs = slice(k * cw, (k + 1) * cw)
            y_cur[:, cs] = jnp.dot(a_ref[...], w_ref[:, cs], preferred_element_type=F32)
            xn = x_ref[rs, :] + g_ref[...] * y_prev[rs, :]
            xo_ref[rs, :] = xn
            ho_ref[rs, :] = _normed(xn, nw_ref[...], sh_ref[...], sc_ref[...]).astype(BF16)

    @pl.when(s % 2 == 0)
    def _():
        step(y0, y1)

    @pl.when(s % 2 == 1)
    def _():
        step(y1, y0)


def _outproj_norm_call(a, w, x, mod5, norm_w, l, k_gate, k_sh, k_sc, tm, bc, seq):
    t, kdim = a.shape
    d = x.shape[-1]
    ni = t // tm
    row0 = x.shape[0] - t
    assert row0 in (0, bc)
    off = row0 // tm
    row = _mod_row(tm, bc - row0, seq)
    cur = lambda s: jnp.minimum(s, ni - 1)
    prev = lambda s: jnp.maximum(s - 1, 0)
    zero = lambda s: 0
    return pl.pallas_call(
        _outproj_norm_kernel,
        grid=(ni + 1,),
        in_specs=[pl.BlockSpec((tm, kdim), lambda s: (cur(s), 0)),
                  pl.BlockSpec((None, kdim, d), lambda s: (l, 0, 0), pipeline_mode=pl.Buffered(1)),
                  pl.BlockSpec((tm, d), lambda s: (prev(s) + off, 0)),
                  _mod_spec(l, k_gate, d, lambda s: row(prev(s)), zero),
                  pl.BlockSpec((None, 1, d), lambda s: (l, 0, 0)),
                  _mod_spec(l, k_sh, d, lambda s: row(prev(s)), zero),
                  _mod_spec(l, k_sc, d, lambda s: row(prev(s)), zero)],
        out_specs=[pl.BlockSpec((tm, d), lambda s: (prev(s), 0)),
                   pl.BlockSpec((tm, d), lambda s: (prev(s), 0))],
        out_shape=[jax.ShapeDtypeStruct((t, d), F32),
                   jax.ShapeDtypeStruct((t, d), BF16)],
        scratch_shapes=[pltpu.VMEM((tm, d), F32), pltpu.VMEM((tm, d), F32)],
        compiler_params=_params(("arbitrary",)),
        name="out_proj_residual_norm",
    )(a, w, x, mod5, norm_w, mod5, mod5)


FFN_PIECE_ROWS = 64
FFN_SLABS = 4
SEQ_ALIGN = 256


def _ffn_gate_kernel(h_ref, hp_ref, hn_ref, wa_ref, wb_ref, cwa_ref, cwb_ref, ba_ref, bb_ref, o_ref,
                     h_s, a_s, b_s, *, nj, bc, ctx, seq):
    s = pl.program_id(0)
    tm = h_ref.shape[0]

    @pl.when(s == 0)
    def _():
        a_s[1] = jnp.zeros(a_s.shape[1:], F32)
        b_s[1] = jnp.zeros(b_s.shape[1:], F32)

    @pl.when(s % nj == 0)
    def _():
        h_s[0:HALO, :] = hp_ref[...]
        h_s[HALO:HALO + tm, :] = h_ref[...]
        h_s[HALO + tm:, :] = hn_ref[...]

    r0 = (jnp.maximum(s - 1, 0) // nj) * tm
    in_ctx = r0 < bc
    tn = o_ref.shape[1]
    rows_ext = tm + 2 * HALO
    pr = FFN_PIECE_ROWS
    row_first = lax.broadcasted_iota(jnp.int32, (pr, 1), 0) == 0
    row_last = lax.broadcasted_iota(jnp.int32, (pr, 1), 0) == pr - 1

    def seq_start(g):
        return jnp.where(in_ctx, _mod_const(g, ctx), _mod_const(g - bc, seq)) == 0

    def taps(w_ref, bias_ref, cs):
        return [jnp.broadcast_to(w_ref[j:j + 1, cs], (pr, LANES)) for j in range(3)] + \
               [jnp.broadcast_to(bias_ref[:, cs], (pr, LANES))]

    col_slabs = [slice(c, c + LANES) for c in range(0, tn, LANES)]
    taps_a = [taps(cwa_ref, ba_ref, cs) for cs in col_slabs]
    taps_b = [taps(cwb_ref, bb_ref, cs) for cs in col_slabs]

    def conv_piece(src, r, cs, tp, kill_prev, kill_next):
        base = HALO + r
        ext = src[base - 8:base + pr + 8, cs]
        xm1 = pltpu.roll(ext, 1, 0)[8:8 + pr]
        x00 = ext[8:8 + pr]
        xp1 = pltpu.roll(ext, pr + 15, 0)[8:8 + pr]
        if kill_prev is not None:
            xm1 = jnp.where(kill_prev, 0.0, xm1)
        if kill_next is not None:
            xp1 = jnp.where(kill_next, 0.0, xp1)
        return xm1 * tp[0] + x00 * tp[1] + xp1 * tp[2] + tp[3]

    def epilogue(a_prev, b_prev, lo, hi):
        for r in range(lo, hi, pr):
            kill_prev = kill_next = None
            if r % SEQ_ALIGN == 0:
                kill_prev = jnp.logical_and(row_first, seq_start(r0 + r))
            if (r + pr) % SEQ_ALIGN == 0:
                kill_next = jnp.logical_and(row_last, seq_start(r0 + r + pr))
            for ci, cs in enumerate(col_slabs):
                a = conv_piece(a_prev, r, cs, taps_a[ci], kill_prev, kill_next)
                b = conv_piece(b_prev, r, cs, taps_b[ci], kill_prev, kill_next)
                o_ref[r:r + pr, cs] = (_silu(a) * b).astype(BF16)

    def step(a_cur, b_cur, a_prev, b_prev):
        nslab = FFN_SLABS
        for k in range(nslab):
            m_lo = (k * rows_ext // nslab) // HALO * HALO
            m_hi = rows_ext if k == nslab - 1 else ((k + 1) * rows_ext // nslab) // HALO * HALO
            h = h_s[m_lo:m_hi, :]
            a_cur[m_lo:m_hi, :] = jnp.dot(h, wa_ref[...], preferred_element_type=F32)
            b_cur[m_lo:m_hi, :] = jnp.dot(h, wb_ref[...], preferred_element_type=F32)
            epilogue(a_prev, b_prev, k * tm // nslab, (k + 1) * tm // nslab)

    cur = s % 2
    step(a_s.at[cur], b_s.at[cur], a_s.at[1 - cur], b_s.at[1 - cur])


def _ffn_gate_call(h, w, cw, cb, l, tm, bc, ctx, seq):
    t, d = h.shape
    tn = 512
    nj = D_FF // tn
    ni = t // tm
    per = tm // HALO
    last = t // HALO - 1
    ci = lambda s: jnp.minimum(s // nj, ni - 1)
    cj = lambda s: s % nj
    pi = lambda s: jnp.maximum(s - 1, 0) // nj
    pj = lambda s: jnp.maximum(s - 1, 0) % nj
    return pl.pallas_call(
        functools.partial(_ffn_gate_kernel, nj=nj, bc=bc, ctx=ctx, seq=seq),
        grid=(ni * nj + 1,),
        in_specs=[pl.BlockSpec((tm, d), lambda s: (ci(s), 0)),
                  pl.BlockSpec((HALO, d), lambda s: (jnp.maximum(ci(s) * per - 1, 0), 0)),
                  pl.BlockSpec((HALO, d), lambda s: (jnp.minimum((ci(s) + 1) * per, last), 0)),
                  pl.BlockSpec((None, d, tn), lambda s: (l, 0, cj(s))),
                  pl.BlockSpec((None, d, tn), lambda s: (l, 0, nj + cj(s))),
                  pl.BlockSpec((None, 3, tn), lambda s: (l, 0, pj(s))),
                  pl.BlockSpec((None, 3, tn), lambda s: (l, 0, nj + pj(s))),
                  pl.BlockSpec((None, 1, tn), lambda s: (l, 0, pj(s))),
                  pl.BlockSpec((None, 1, tn), lambda s: (l, 0, nj + pj(s)))],
        out_specs=pl.BlockSpec((tm, tn), lambda s: (pi(s), pj(s))),
        out_shape=jax.ShapeDtypeStruct((t, D_FF), BF16),
        scratch_shapes=[pltpu.VMEM((tm + 2 * HALO, d), BF16)]
                       + [pltpu.VMEM((2, tm + 2 * HALO, tn), F32) for _ in range(2)],
        compiler_params=_params(("arbitrary",)),
        name="ffn_up_conv_gate",
    )(h, h, h, w, w, cw, cw, cb, cb)


def _final_norm_kernel(x_ref, w_ref, o_ref):
    x = x_ref[...]
    o_ref[...] = (x * lax.rsqrt(jnp.mean(jnp.square(x), axis=-1, keepdims=True) + EPS)) * w_ref[...]


def _final_norm_call(x, w, row0, rows, tm):
    d = x.shape[-1]
    off = row0 // tm
    return pl.pallas_call(
        _final_norm_kernel,
        grid=(rows // tm,),
        in_specs=[pl.BlockSpec((tm, d), lambda i: (i + off, 0)),
                  pl.BlockSpec((1, d), lambda i: (0, 0))],
        out_specs=pl.BlockSpec((tm, d), lambda i: (i, 0)),
        out_shape=jax.ShapeDtypeStruct((rows, d), F32),
        compiler_params=_params(("parallel",)),
        name="final_norm",
    )(x, w)


def _halo_specs(rt, cols, col_idx, t):
    per = rt // HALO
    last = t // HALO - 1
    return [pl.BlockSpec((rt, cols), lambda i, j: (i, col_idx(j))),
            pl.BlockSpec((HALO, cols), lambda i, j: (jnp.maximum(i * per - 1, 0), col_idx(j))),
            pl.BlockSpec((HALO, cols), lambda i, j: (jnp.minimum((i + 1) * per, last), col_idx(j)))]


def _shift_pm1(x, prev_row, next_row, pos, n):
    rt = x.shape[0]
    g = 8
    first = lax.broadcasted_iota(jnp.int32, (g, 1), 0) == 0
    last = lax.broadcasted_iota(jnp.int32, (g, 1), 0) == g - 1
    xm1 = pltpu.roll(x, 1, 0)
    xp1 = pltpu.roll(x, rt - 1, 0)
    m_parts, p_parts = [], []
    for s in range(0, rt, SEQ_ALIGN):
        e = s + SEQ_ALIGN
        head = xm1[s:s + g]
        row0 = prev_row if s == 0 else head[0:1]
        row0 = jnp.where(pos[s:s + 1] == 0, 0.0, row0)
        m_parts += [jnp.where(first, row0, head), xm1[s + g:e]]
        tail = xp1[e - g:e]
        row7 = next_row if e == rt else tail[g - 1:g]
        row7 = jnp.where(pos[e - 1:e] == n - 1, 0.0, row7)
        p_parts += [xp1[s:e - g], jnp.where(last, row7, tail)]
    return jnp.concatenate(m_parts, axis=0), jnp.concatenate(p_parts, axis=0)


def _conv3(x, prev_row, next_row, w, pos, n):
    xm1, xp1 = _shift_pm1(x, prev_row, next_row, pos, n)
    return xm1 * w[0:1] + x * w[1:2] + xp1 * w[2:3]


def _ssd_conv_kernel(x_ref, p_ref, n_ref, w_ref, b_ref, o_ref, *, bc, ctx, seq):
    rt = x_ref.shape[0]
    pos, n = _seq_pos(pl.program_id(0) * rt, rt, bc, ctx, seq)
    x = x_ref[...].astype(F32)
    prev_row = p_ref[...].astype(F32)[HALO - 1:HALO]
    next_row = n_ref[...].astype(F32)[0:1]
    y = _conv3(x, prev_row, next_row, w_ref[...], pos, n) + b_ref[...]
    o_ref[...] = _silu(y).astype(BF16)


def _ssd_conv_call(u, w, b, l, rt, bc, ctx, seq):
    t = u.shape[0]
    cols = 1024
    nj = SSD_CONV_CH // cols
    base = U_XBC // cols
    return pl.pallas_call(
        functools.partial(_ssd_conv_kernel, bc=bc, ctx=ctx, seq=seq),
        grid=(t // rt, nj),
        in_specs=_halo_specs(rt, cols, lambda j: base + j, t)
                 + [pl.BlockSpec((None, 3, cols), lambda i, j: (l, 0, j)),
                    pl.BlockSpec((None, 1, cols), lambda i, j: (l, 0, j))],
        out_specs=pl.BlockSpec((rt, cols), lambda i, j: (i, j)),
        out_shape=jax.ShapeDtypeStruct((t, SSD_CONV_CH), BF16),
        compiler_params=_params(("parallel", "parallel")),
        name="ssd_conv",
    )(u, u, u, w, b)


def _sconv_body(r0, b_ref, c_ref, cp_ref, cn_ref, x_ref, xp_ref, xn_ref, w_ref, o_ref, *, bc, ctx, seq):
    rt = b_ref.shape[0]
    pos, n = _seq_pos(r0, rt, bc, ctx, seq)
    cx = c_ref[...].astype(F32) * x_ref[...].astype(F32)
    prev_row = (cp_ref[...].astype(F32) * xp_ref[...].astype(F32))[HALO - 1:HALO]
    next_row = (cn_ref[...].astype(F32) * xn_ref[...].astype(F32))[0:1]
    y = b_ref[...].astype(F32) * _conv3(cx, prev_row, next_row, w_ref[...], pos, n)
    o_ref[...] = y.astype(BF16)


POOL_ROWS = 256
POOL_EXT = 384


def _pool_body(r0, x_ref, p_ref, n_ref, band_ref, cnt_ref, w_ref, s_ref, o_ref, *, bc, ctx, seq):
    rt = POOL_ROWS
    in_ctx = r0 < bc
    p0 = jnp.where(in_ctx, _mod_const(r0, ctx), _mod_const(r0 - bc, seq))
    n = jnp.where(in_ctx, ctx, seq)
    has_prev = p0 != 0
    has_next = p0 + rt != n
    zero = jnp.zeros((HALO, POOL_GROUP), BF16)
    pad = jnp.zeros((POOL_EXT - rt - 2 * HALO, POOL_GROUP), BF16)
    case = jnp.where(has_prev, 0, 2) + jnp.where(has_next, 0, 1)
    groups = [slice(gi * POOL_GROUP, (gi + 1) * POOL_GROUP) for gi in range(len(POOL_WINDOWS))]
    sums = []
    for gi, cs in enumerate(groups):
        ext = jnp.concatenate([jnp.where(has_prev, p_ref[:, cs], zero), x_ref[:, cs],
                               jnp.where(has_next, n_ref[:, cs], zero), pad], axis=0)
        sums.append(jnp.dot(band_ref[gi], ext, preferred_element_type=F32))
    for gi, cs in enumerate(groups):
        cnt = cnt_ref[gi, case]
        mean = sums[gi] / jnp.concatenate([cnt] * (POOL_GROUP // LANES), axis=1)
        pooled = (mean - x_ref[:, cs].astype(F32)).astype(BF16)
        y = jnp.dot(pooled, w_ref[gi], preferred_element_type=F32) * s_ref[:, cs]
        o_ref[:, cs] = y.astype(BF16)


def _pool_bands():
    t = jnp.arange(POOL_ROWS)[:, None]
    s = jnp.arange(POOL_EXT)[None, :] - HALO
    return jnp.stack([((s >= t - w // 2) & (s < t - w // 2 + w)) for w in POOL_WINDOWS]).astype(BF16)


def _pool_counts():
    t = jnp.arange(POOL_ROWS)
    tabs = []
    for w in POOL_WINDOWS:
        before = jnp.maximum(w // 2 - t, 0)
        after = jnp.maximum(t - w // 2 + w - POOL_ROWS, 0)
        cases = [w - (before if c >= 2 else 0) - (after if c % 2 else 0) for c in range(4)]
        tabs.append(jnp.stack([jnp.broadcast_to(c, (POOL_ROWS,)) for c in cases]))
    tab = jnp.stack(tabs).astype(F32)
    return jnp.broadcast_to(tab[..., None], tab.shape + (LANES,))


def _scan_positions(ncc, nk):
    def kpos(sw, s):
        back = jnp.where(s < ncc, ncc - 1 - s, nk - 1 - (s - ncc))
        return jnp.where(sw == 0, back, s)
    return kpos


def _chunk_row(b, k, ncc, ncl, nb):
    return jnp.where(k < ncc, b * ncc + k, nb * ncc + b * ncl + (k - ncc))


def _tri_masks():
    row = lax.broadcasted_iota(jnp.int32, (CHUNK, CHUNK), 0)
    col = lax.broadcasted_iota(jnp.int32, (CHUNK, CHUNK), 1)
    return row, col, col <= row, col >= row


MASKED = -1e30


SCAN_CHUNKS = 2


def _ssd_scan_parts(xbc_ref, dt_ref, z_ref, bias_ref, alog_ref, dexp_ref, nw_ref, sp_ref, sg_ref, o_ref,
                    hb_all, hf_s, hb_s, y_s, vw_s, pre_s, *, kpos):
    row, col, lower, upper = _tri_masks()
    lane_lo = col < SSD_HEAD_DIM
    row_lo1 = lane_lo[0:1]
    nh = SSD_HEADS
    n_pairs = nh // 2
    pairs_per_group = n_pairs // SSD_GROUPS
    gw = SSD_INNER // SSD_GROUPS

    def chunk_rows(c):
        return slice(c * CHUNK, (c + 1) * CHUNK)

    def split2(x):
        hi = x.astype(BF16).astype(F32)
        return hi, x - hi

    def split3(x):
        hi, r = split2(x)
        mid, lo = split2(r)
        return hi, mid, lo

    def to_tokens(parts):
        used = sum(p.shape[0] for p in parts)
        return jnp.concatenate(list(parts) + [jnp.zeros((CHUNK - used, CHUNK), F32)], axis=0).T.astype(BF16)

    def store_state_operand(wt, rc):
        lhs = to_tokens((jnp.zeros((4 * nh, CHUNK), F32),) + split2(wt))
        w_full = jnp.dot(lhs, sp_ref[:, 2 * SSD_INNER:3 * SSD_INNER], preferred_element_type=F32)
        vw_s[...] = (xbc_ref[rc, 0:SSD_INNER].astype(F32) * w_full).astype(BF16)

    def seg_all(cum):
        parts = split3(cum)
        ones = jnp.ones((nh, CHUNK), F32)
        a = to_tokens(parts + (ones, ones, ones))
        neg = jnp.concatenate([-p for p in parts], axis=0)
        tiled = jnp.concatenate([neg] * nh, axis=1)
        rows_b = jnp.where(sg_ref[3 * nh:6 * nh, :] != 0, tiled, 0.0).astype(BF16)
        b = jnp.concatenate([sg_ref[0:3 * nh, :], rows_b,
                             jnp.zeros((CHUNK - 6 * nh, nh * CHUNK), BF16)], axis=0)
        return jnp.dot(a, b, preferred_element_type=F32)

    def pair_scalar(x, p, j):
        return jnp.where(row_lo1, jnp.broadcast_to(x[2 * p:2 * p + 1, j:j + 1], (1, LANES)),
                         jnp.broadcast_to(x[2 * p + 1:2 * p + 2, j:j + 1], (1, LANES)))

    def state_update(h_s, g, dec_rows, rc):
        gs = slice(g * gw, (g + 1) * gw)
        bm = xbc_ref[rc, SSD_INNER + g * SSD_STATE:SSD_INNER + (g + 1) * SSD_STATE]
        st = lax.dot_general(bm, vw_s[:, gs], (((0,), (0,)), ((), ())), preferred_element_type=F32)
        h_s[:, gs] = h_s[:, gs] * jnp.concatenate(dec_rows, axis=1) + st

    def init_backward():
        hb_s[...] = jnp.zeros_like(hb_s)

    def init_forward():
        hf_s[...] = jnp.zeros_like(hf_s)

    def backward_step():
        for c in reversed(range(SCAN_CHUNKS)):
            rc = chunk_rows(c)
            ck = kpos * SCAN_CHUNKS + c
            dtt = dt_ref[rc, :].T[0:nh]
            dt2 = jax.nn.softplus(jnp.concatenate([dtt, dtt], axis=0) + bias_ref[...])
            la2 = dt2 * (-jnp.exp(alog_ref[...]))
            tri2 = jnp.concatenate([upper.astype(F32), lower.astype(F32)], axis=1)
            cum2 = jnp.dot(la2, tri2, precision=lax.Precision.HIGHEST, preferred_element_type=F32)
            cum_f = cum2[0:nh, 0:CHUNK]
            cum_b = cum2[nh:2 * nh, CHUNK:2 * CHUNK]
            pre_s[ck, 0:2 * nh] = dt2
            pre_s[ck, 2 * nh:3 * nh] = cum_f
            pre_s[ck, 3 * nh:4 * nh] = cum_b

            hb_all[ck] = hb_s[...].astype(BF16)
            store_state_operand(jnp.exp(cum_b[:, 0:1] - cum_b) * dt2[nh:2 * nh], rc)
            for g in range(SSD_GROUPS):
                dec_rows = [jnp.exp(pair_scalar(cum_b, g * pairs_per_group + pp, 0))
                            for pp in range(pairs_per_group)]
                state_update(hb_s, g, dec_rows, rc)

    def forward_step():
        for c in range(SCAN_CHUNKS):
            rc = chunk_rows(c)
            ck = kpos * SCAN_CHUNKS + c
            dt2 = pre_s[ck, 0:2 * nh]
            cum_f = pre_s[ck, 2 * nh:3 * nh]
            cum_b = pre_s[ck, 3 * nh:4 * nh]
            hb_in = hb_all[ck]
            wt = jnp.exp(cum_f[:, CHUNK - 1:CHUNK] - cum_f) * dt2[0:nh]
            ef, eb = split2(jnp.exp(cum_f)), split2(jnp.exp(cum_b))
            lhs = to_tokens((ef[0], eb[0], ef[1], eb[1]) + split2(wt))
            spread = jnp.dot(lhs, sp_ref[...], preferred_element_type=F32)
            vw_s[...] = (xbc_ref[rc, 0:SSD_INNER].astype(F32)
                         * spread[:, 2 * SSD_INNER:3 * SSD_INNER]).astype(BF16)
            seg_f = seg_all(cum_f)
            seg_b = seg_all(cum_b)
            for g in range(SSD_GROUPS):
                gs = slice(g * gw, (g + 1) * gw)
                bm = xbc_ref[rc, SSD_INNER + g * SSD_STATE:SSD_INNER + (g + 1) * SSD_STATE]
                cm = xbc_ref[rc, SSD_INNER + SSD_GROUPS * SSD_STATE + g * SSD_STATE:
                             SSD_INNER + SSD_GROUPS * SSD_STATE + (g + 1) * SSD_STATE]
                sc = lax.dot_general(cm, bm, (((1,), (1,)), ((), ())), preferred_element_type=F32)
                hcat = jnp.concatenate([hf_s[:, gs].astype(BF16), hb_in[:, gs]], axis=1)
                gfb = jnp.dot(cm, hcat, preferred_element_type=F32)
                dec_rows = []
                for pp in range(pairs_per_group):
                    p = g * pairs_per_group + pp
                    ps = slice(p * LANES, (p + 1) * LANES)
                    xs_pair = xbc_ref[rc, ps]
                    ms = []
                    for h in (2 * p, 2 * p + 1):
                        hs = slice(h * CHUNK, (h + 1) * CHUNK)
                        dec_f = jnp.exp(jnp.where(lower, seg_f[:, hs], MASKED))
                        dec_b = jnp.exp(jnp.where(upper, seg_b[:, hs], MASKED))
                        m = sc * (dec_f * dt2[h:h + 1] + dec_b * dt2[nh + h:nh + h + 1])
                        ms.append(m.astype(BF16))
                    zero = jnp.zeros_like(xs_pair)
                    x_split = jnp.concatenate([jnp.where(lane_lo, xs_pair, zero),
                                               jnp.where(lane_lo, zero, xs_pair)], axis=0)
                    y_intra = jnp.dot(jnp.concatenate(ms, axis=1), x_split, preferred_element_type=F32)
                    lo = pp * LANES
                    y_inter = (gfb[:, lo:lo + LANES] * spread[:, ps]
                               + gfb[:, gw + lo:gw + lo + LANES]
                               * spread[:, SSD_INNER + p * LANES:SSD_INNER + (p + 1) * LANES])
                    y_s[:, ps] = y_intra + y_inter
                    dec_rows.append(jnp.exp(pair_scalar(cum_f, p, CHUNK - 1)))
                state_update(hf_s, g, dec_rows, rc)

            xs32 = xbc_ref[rc, 0:SSD_INNER].astype(F32)
            y = (y_s[...] + xs32 * dexp_ref[...]) * _silu(z_ref[rc, :].astype(F32))
            y = y * lax.rsqrt(jnp.mean(jnp.square(y), axis=-1, keepdims=True) + EPS)
            o_ref[rc, :] = (y * nw_ref[...]).astype(BF16)

    return init_backward, backward_step, init_forward, forward_step


def _ssd_spread_constants():
    nh = SSD_HEADS
    r = jnp.arange(CHUNK)[:, None]
    c = jnp.arange(3 * SSD_INNER)[None, :]
    block_of_row = jnp.array([0, 1, 0, 1, 2, 2, -1, -1])[r // nh]
    sp = (block_of_row == c // SSD_INNER) & (r % nh == (c % SSD_INNER) // SSD_HEAD_DIM)
    c2 = jnp.arange(nh * CHUNK)[None, :]
    sg = (r < 6 * nh) & (r % nh == c2 // CHUNK)
    return sp.astype(BF16), sg.astype(BF16)


N_SSD_IN, N_SSD_SCRATCH = 9, 6
N_RET_IN, N_RET_SCRATCH = 7, 5


def _ssd_scan_specs(l, rows, in_row, nk):
    in_specs = [pl.BlockSpec((rows, SSD_CONV_CH), lambda b, sw, s: (in_row(b, sw, s), 0)),
                pl.BlockSpec((rows, LANES), lambda b, sw, s: (in_row(b, sw, s), 0)),
                pl.BlockSpec((rows, SSD_INNER), lambda b, sw, s: (in_row(b, sw, s), U_Z // SSD_INNER)),
                pl.BlockSpec((None, 2 * SSD_HEADS, CHUNK), lambda b, sw, s: (l, 0, 0)),
                pl.BlockSpec((None, 2 * SSD_HEADS, CHUNK), lambda b, sw, s: (l, 0, 0)),
                pl.BlockSpec((None, 1, SSD_INNER), lambda b, sw, s: (l, 0, 0)),
                pl.BlockSpec((None, 1, SSD_INNER), lambda b, sw, s: (l, 0, 0)),
                pl.BlockSpec((CHUNK, 3 * SSD_INNER), lambda b, sw, s: (0, 0)),
                pl.BlockSpec((CHUNK, SSD_HEADS * CHUNK), lambda b, sw, s: (0, 0))]
    scratch = [pltpu.VMEM((nk * SCAN_CHUNKS, SSD_STATE, SSD_INNER), BF16),
               pltpu.VMEM((SSD_STATE, SSD_INNER), F32),
               pltpu.VMEM((SSD_STATE, SSD_INNER), F32),
               pltpu.VMEM((CHUNK, SSD_INNER), F32),
               pltpu.VMEM((CHUNK, SSD_INNER), BF16),
               pltpu.VMEM((nk * SCAN_CHUNKS, 4 * SSD_HEADS, CHUNK), F32)]
    assert len(in_specs) == N_SSD_IN and len(scratch) == N_SSD_SCRATCH
    return in_specs, scratch


def _ret_scan_parts(q_ref, k_ref, v_ref, g_ref, cos_ref, sin_ref, dl_ref, o_ref,
                    hb_all, hf_s, hb_s, dec_s, tab_s, *, kpos):
    row, col, lower, upper = _tri_masks()
    lane_lo = col < RET_K_HEAD
    la = jax.nn.log_sigmoid(dl_ref[...])
    rowf = row.astype(F32)
    diff = (row - col).astype(F32)
    n_pairs = RET_HEADS // 2

    first16 = (col & 31) < 16

    def chunk_rows(c):
        return slice(c * CHUNK, (c + 1) * CHUNK)

    def rope(x, rc):
        swapped = jnp.where(first16, pltpu.roll(x, LANES - 16, 1), pltpu.roll(x, 16, 1))
        return x * cos_ref[rc, :] + swapped * sin_ref[rc, :]

    def head_scalar(d, h):
        return la[d:d + 1, h:h + 1]

    def pair_k(p, rc):
        return rope(k_ref[rc, p * LANES:(p + 1) * LANES].astype(F32), rc).astype(BF16)

    def state_update(h_s, d, p, kp, w_tab, rc):
        h0, h1 = 2 * p, 2 * p + 1
        v_of = lambda h: v_ref[rc, h * RET_V_HEAD:(h + 1) * RET_V_HEAD]
        vw = jnp.concatenate([(v_of(h0).astype(F32) * tab_s[w_tab, h0]).astype(BF16),
                              (v_of(h1).astype(F32) * tab_s[w_tab, h1]).astype(BF16)], axis=1)
        st = lax.dot_general(kp, vw, (((0,), (0,)), ((), ())), preferred_element_type=F32)
        top = row < RET_K_HEAD
        acc = jnp.where(top, st[:, 0:RET_V_HEAD], st[:, RET_V_HEAD:2 * RET_V_HEAD])
        dec = jnp.where(top, jnp.exp(float(CHUNK) * head_scalar(d, h0)),
                        jnp.exp(float(CHUNK) * head_scalar(d, h1)))
        rs = slice(p * LANES, (p + 1) * LANES)
        h_s[rs, :] = h_s[rs, :] * dec + acc

    def init_backward():
        hb_s[...] = jnp.zeros_like(hb_s)
        for h in range(RET_HEADS):
            laf = head_scalar(0, h)
            lab = head_scalar(1, h)
            dec_s[h] = (jnp.exp(jnp.where(lower, diff * laf, MASKED))
                        + jnp.exp(jnp.where(upper, -diff * lab, MASKED)))
            tab_s[0, h] = jnp.exp((rowf + 1.0) * laf)
            tab_s[1, h] = jnp.exp((float(CHUNK) - rowf) * lab)
            tab_s[2, h] = jnp.exp((float(CHUNK - 1) - rowf) * laf)
            tab_s[3, h] = jnp.exp(rowf * lab)

    def init_forward():
        hf_s[...] = jnp.zeros_like(hf_s)

    def backward_step():
        for c in reversed(range(SCAN_CHUNKS)):
            rc = chunk_rows(c)
            hb_all[kpos * SCAN_CHUNKS + c] = hb_s[...].astype(BF16)
            for p in range(n_pairs):
                state_update(hb_s, 1, p, pair_k(p, rc), 3, rc)

    def forward_step():
        for c in range(SCAN_CHUNKS):
            rc = chunk_rows(c)
            hb_in = hb_all[kpos * SCAN_CHUNKS + c]
            for p in range(n_pairs):
                h0, h1 = 2 * p, 2 * p + 1
                v_of = lambda h: v_ref[rc, h * RET_V_HEAD:(h + 1) * RET_V_HEAD]
                qp = rope(q_ref[rc, p * LANES:(p + 1) * LANES].astype(F32) * (RET_K_HEAD ** -0.5), rc)
                kp = pair_k(p, rc)
                rs = slice(p * LANES, (p + 1) * LANES)
                qcat = jnp.concatenate([jnp.where(lane_lo, qp, 0.0), jnp.where(lane_lo, 0.0, qp)],
                                       axis=0).astype(BF16)
                sc = lax.dot_general(qcat, kp, (((1,), (1,)), ((), ())), preferred_element_type=F32)
                hcat = jnp.concatenate([hf_s[rs, :].astype(BF16), hb_in[rs, :]], axis=1)
                gfb = jnp.dot(qcat, hcat, preferred_element_type=F32)
                m = jnp.concatenate([(sc[0:CHUNK] * dec_s[h0]).astype(BF16),
                                     (sc[CHUNK:2 * CHUNK] * dec_s[h1]).astype(BF16)], axis=1)
                zero = jnp.zeros((CHUNK, RET_V_HEAD), BF16)
                v_blk = jnp.concatenate([jnp.concatenate([v_of(h0), zero], axis=1),
                                         jnp.concatenate([zero, v_of(h1)], axis=1)], axis=0)
                y_intra = jnp.dot(m, v_blk, preferred_element_type=F32)
                for i, h in enumerate((h0, h1)):
                    vs = slice(h * RET_V_HEAD, (h + 1) * RET_V_HEAD)
                    g_h = gfb[i * CHUNK:(i + 1) * CHUNK]
                    y = (y_intra[:, i * RET_V_HEAD:(i + 1) * RET_V_HEAD]
                         + g_h[:, 0:RET_V_HEAD] * tab_s[0, h]
                         + g_h[:, RET_V_HEAD:2 * RET_V_HEAD] * tab_s[1, h])
                    mu = jnp.mean(y, axis=-1, keepdims=True)
                    yc = y - mu
                    var = jnp.mean(jnp.square(yc), axis=-1, keepdims=True)
                    yn = yc * lax.rsqrt(var + EPS)
                    o_ref[rc, vs] = (_silu(g_ref[rc, vs].astype(F32)) * yn).astype(BF16)
                state_update(hf_s, 0, p, kp, 2, rc)

    return init_backward, backward_step, init_forward, forward_step


def _ret_scan_specs(l, rows, in_row, kpos, nk):
    bw = BRANCH_WIDTH
    in_specs = [pl.BlockSpec((rows, RET_QK), lambda b, sw, s: (in_row(b, sw, s), U_RQ // RET_QK)),
                pl.BlockSpec((rows, RET_QK), lambda b, sw, s: (in_row(b, sw, s), U_RK // RET_QK)),
                pl.BlockSpec((rows, bw), lambda b, sw, s: (in_row(b, sw, s), U_RV // bw)),
                pl.BlockSpec((rows, bw), lambda b, sw, s: (in_row(b, sw, s), U_RG // bw)),
                pl.BlockSpec((rows, LANES), lambda b, sw, s: (kpos(sw, s), 0)),
                pl.BlockSpec((rows, LANES), lambda b, sw, s: (kpos(sw, s), 0)),
                pl.BlockSpec((None, 2, LANES), lambda b, sw, s: (l, 0, 0))]
    scratch = [pltpu.VMEM((nk * SCAN_CHUNKS, RET_QK, RET_V_HEAD), BF16),
               pltpu.VMEM((RET_QK, RET_V_HEAD), F32),
               pltpu.VMEM((RET_QK, RET_V_HEAD), F32),
               pltpu.VMEM((RET_HEADS, CHUNK, CHUNK), F32),
               pltpu.VMEM((4, RET_HEADS, CHUNK, LANES), F32)]
    assert len(in_specs) == N_RET_IN and len(scratch) == N_RET_SCRATCH
    return in_specs, scratch


N_POOL_IN, N_SCONV_IN = 7, 8


def _scans_kernel(*refs, ncc, ncl, nk, nb, bc, ctx, seq):
    bounds = [0]
    for n in (N_SSD_IN, N_RET_IN, N_POOL_IN, N_SCONV_IN):
        bounds.append(bounds[-1] + n)
    ssd_in, ret_in, pool_in, sconv_in = (refs[bounds[i]:bounds[i + 1]] for i in range(4))
    n_in = bounds[-1]
    ssd_out, ret_out, pool_out, sconv_out = refs[n_in:n_in + 4]
    scratch = refs[n_in + 4:]
    ssd_scr, ret_scr = scratch[:N_SSD_SCRATCH], scratch[N_SSD_SCRATCH:]
    sw = pl.program_id(1)
    s = pl.program_id(2)
    kpos = _scan_positions(ncc, nk)(sw, s)
    ssd = _ssd_scan_parts(*ssd_in, ssd_out, *ssd_scr, kpos=kpos)
    ret = _ret_scan_parts(*ret_in, ret_out, *ret_scr, kpos=kpos)
    r0 = _chunk_row(pl.program_id(0), kpos, ncc, ncl, nb) * (SCAN_CHUNKS * CHUNK)

    @pl.when(jnp.logical_and(sw == 0, s == 0))
    def _():
        ssd[0]()
        ret[0]()

    @pl.when(sw == 0)
    def _():
        _sconv_body(r0, *sconv_in, sconv_out, bc=bc, ctx=ctx, seq=seq)
        ssd[1]()
        _pool_body(r0, *pool_in, pool_out, bc=bc, ctx=ctx, seq=seq)
        ret[1]()

    @pl.when(jnp.logical_and(sw == 1, s == 0))
    def _():
        ssd[2]()
        ret[2]()

    @pl.when(sw == 1)
    def _():
        ssd[3]()
        ret[3]()


def _mixers_call(xbc, dt, u, dt_bias, a_log, d_exp, norm_w, cos_t, sin_t, decay_logit,
                 pool_w, pool_scale, sconv_w, l, nb, bc, ctx, seq):
    t = u.shape[0]
    rows = SCAN_CHUNKS * CHUNK
    assert rows == POOL_ROWS and SSD_INNER == BRANCH_WIDTH
    ncc, ncl = ctx // rows, seq // rows
    nk = ncc + ncl
    kpos = _scan_positions(ncc, nk)
    bw = BRANCH_WIDTH
    per = rows // HALO
    last = t // HALO - 1

    def in_row(b, sw, s):
        return _chunk_row(b, kpos(sw, s), ncc, ncl, nb)

    def out_row(b, sw, s):
        return _chunk_row(b, jnp.where(sw == 0, 0, s), ncc, ncl, nb)

    def once_row(b, sw, s):
        return in_row(b, 0, jnp.where(sw == 0, s, nk - 1))

    def halo(col):
        c = col // bw
        return [pl.BlockSpec((rows, bw), lambda b, sw, s: (once_row(b, sw, s), c)),
                pl.BlockSpec((HALO, bw), lambda b, sw, s: (jnp.maximum(once_row(b, sw, s) * per - 1, 0), c)),
                pl.BlockSpec((HALO, bw), lambda b, sw, s: (jnp.minimum((once_row(b, sw, s) + 1) * per, last), c))]

    ng = len(POOL_WINDOWS)
    ssd_in, ssd_scr = _ssd_scan_specs(l, rows, in_row, nk)
    ret_in, ret_scr = _ret_scan_specs(l, rows, in_row, kpos, nk)
    pool_in = halo(U_POOL) + [pl.BlockSpec((ng, rows, POOL_EXT), lambda b, sw, s: (0, 0, 0)),
                              pl.BlockSpec((ng, 4, rows, LANES), lambda b, sw, s: (0, 0, 0, 0)),
                              pl.BlockSpec((None, ng, POOL_GROUP, POOL_GROUP), lambda b, sw, s: (l, 0, 0, 0)),
                              pl.BlockSpec((None, 1, bw), lambda b, sw, s: (l, 0, 0))]
    sconv_in = ([pl.BlockSpec((rows, bw), lambda b, sw, s: (once_row(b, sw, s), U_SCB // bw))]
                + halo(U_SCC) + halo(U_SCX) + [pl.BlockSpec((None, 3, bw), lambda b, sw, s: (l, 0, 0))])
    assert len(pool_in) == N_POOL_IN and len(sconv_in) == N_SCONV_IN
    scan_out = pl.BlockSpec((rows, bw), lambda b, sw, s: (out_row(b, sw, s), 0))
    once_out = pl.BlockSpec((rows, bw), lambda b, sw, s: (once_row(b, sw, s), 0))
    return pl.pallas_call(
        functools.partial(_scans_kernel, ncc=ncc, ncl=ncl, nk=nk, nb=nb, bc=bc, ctx=ctx, seq=seq),
        grid=(nb, 2, nk),
        in_specs=ssd_in + ret_in + pool_in + sconv_in,
        out_specs=[scan_out, scan_out, once_out, once_out],
        out_shape=[jax.ShapeDtypeStruct((t, bw), BF16)] * 4,
        scratch_shapes=ssd_scr + ret_scr,
        compiler_params=_params(("parallel", "arbitrary", "arbitrary")),
        name="mixer_branches",
    )(xbc, dt, u, dt_bias, a_log, d_exp, norm_w, *_ssd_spread_constants(),
      u, u, u, u, cos_t, sin_t, decay_logit,
      u, u, u, _pool_bands(), _pool_counts(), pool_w, pool_scale,
      u, u, u, u, u, u, u, sconv_w)


def _rope_tables(ctx, seq):
    quarter = RET_K_HEAD // 4
    t = jnp.arange(seq)
    rowp = (t // GRID_W).astype(F32)
    colp = (t % GRID_W).astype(F32)
    inv = ROPE_BASE ** (-jnp.arange(quarter, dtype=F32) / quarter)
    ang_r = rowp[:, None] * inv[None, :]
    ang_c = colp[:, None] * inv[None, :]

    def blocks(ang):
        c, s = jnp.cos(ang), jnp.sin(ang)
        return jnp.concatenate([c, c], axis=-1), jnp.concatenate([-s, s], axis=-1)

    cr, sr = blocks(ang_r)
    cc, sc = blocks(ang_c)
    cos_h = jnp.concatenate([cr, cc], axis=-1)
    sin_h = jnp.concatenate([sr, sc], axis=-1)
    cos_l = jnp.concatenate([cos_h, cos_h], axis=-1)
    sin_l = jnp.concatenate([sin_h, sin_h], axis=-1)
    cos_t = jnp.concatenate([jnp.ones((ctx, LANES), F32), cos_l], axis=0)
    sin_t = jnp.concatenate([jnp.zeros((ctx, LANES), F32), sin_l], axis=0)
    return cos_t, sin_t


def _pad_lanes(a):
    return jnp.pad(a, [(0, 0)] * (a.ndim - 1) + [(0, LANES - a.shape[-1])])


def kernel(x, c, ctx, c_ctx, w_mod, b_mod, norm1_w, w_in, ssd_conv_w, ssd_conv_b, ssd_a_log,
           ssd_dt_bias, ssd_d, ssd_norm_w, pool_w, pool_scale, sconv_w, ret_decay_logit,
           w_branch, w_gate, b_gate, w_o, norm2_w, ffn_up, ffn_conv_w, ffn_conv_b, ffn_down,
           final_norm_w):
    nb, seq, d = x.shape
    nctx = ctx.shape[1]
    depth = w_mod.shape[0]
    assert d == D_MODEL and seq % GRID_W == 0
    assert nctx % POOL_ROWS == 0 and seq % POOL_ROWS == 0
    bc = nb * nctx
    t = bc + nb * seq
    tm = _pick((1024, 512, 256), bc, seq)
    rt = _pick((512, 256), bc, seq)

    def row_of(i):
        r = i * tm
        return jnp.where(r < bc, 0, 1 + (r - bc) // seq)

    sizes = (SSD_INNER, SSD_CONV_CH, SSD_HEADS, BRANCH_WIDTH, BRANCH_WIDTH, BRANCH_WIDTH,
             BRANCH_WIDTH, RET_QK, RET_QK, BRANCH_WIDTH, BRANCH_WIDTH)
    offs = [0]
    for sz in sizes:
        offs.append(offs[-1] + sz)
    parts = [w_in[:, :, offs[i]:offs[i + 1]] for i in range(len(sizes))]
    w_main = jnp.concatenate(parts[:2] + parts[3:], axis=-1).astype(BF16)
    w_dt = _pad_lanes(parts[2]).astype(BF16)
    assert w_main.shape[-1] == U_COLS
    wg_b = w_gate.astype(BF16)
    wb_b = w_branch.astype(BF16)
    wo_b = w_o.astype(BF16)
    up_b = ffn_up.astype(BF16)
    down_b = ffn_down.astype(BF16)
    poolw_b = pool_w.astype(BF16)

    nrows = 8 * ((1 + nb + 7) // 8)
    cvec = jnp.zeros((nrows, d), F32).at[0].set(c_ctx).at[1:1 + nb].set(c)
    mod = _mod_call(cvec, w_mod, b_mod)
    mod5 = mod.reshape(depth, nrows, 6, 1, d)

    cos_t, sin_t = _rope_tables(nctx, seq)
    rep = lambda a: jnp.broadcast_to(a.reshape(depth, 2 * SSD_HEADS, 1), (depth, 2 * SSD_HEADS, CHUNK))
    dt_bias_p = rep(ssd_dt_bias)
    a_log_p = rep(ssd_a_log)
    decay_p = _pad_lanes(ret_decay_logit)
    d_exp = jnp.repeat(ssd_d, SSD_HEAD_DIM, axis=-1)[:, None, :]

    r3 = lambda a: a[:, None, :]
    xs = jnp.concatenate([ctx.reshape(bc, d), x.reshape(nb * seq, d)], axis=0)
    for l in range(depth):
        u, h, dt = _in_proj_call(xs, r3(norm1_w), mod5, w_main, w_dt, l, 0, 1, tm, row_of)
        xbc = _ssd_conv_call(u, ssd_conv_w, r3(ssd_conv_b), l, rt, bc, nctx, seq)
        y_ssd, y_ret, y_pool, y_sc = _mixers_call(
            xbc, dt, u, dt_bias_p, a_log_p, d_exp, r3(ssd_norm_w), cos_t, sin_t, decay_p,
            poolw_b, r3(pool_scale), sconv_w, l, nb, bc, nctx, seq)
        lat = l == depth - 1
        bc_l = 0 if lat else bc
        row_l = _mod_row(tm, bc_l, seq)
        merged = _merge_call(h, (y_ssd, y_pool, y_sc, y_ret), wg_b, b_gate[:, :, None, :], wb_b, l, tm,
                             row0=bc if lat else 0)
        xs, h2 = _outproj_norm_call(merged, wo_b, xs, mod5, r3(norm2_w), l, 2, 3, 4, rt, bc, seq)
        gact = _ffn_gate_call(h2, up_b, ffn_conv_w, r3(ffn_conv_b), l, tm, bc_l, nctx, seq)
        xs = _res_call(gact, down_b, xs, mod5, l, 5, tm, row_l, "ffn_down_residual")
    out = _final_norm_call(xs, final_norm_w[None, :], 0, nb * seq, tm)
    return out.reshape(nb, seq, d)
```

```python
import functools

import jax
import jax.numpy as jnp
from jax import lax
from jax.experimental import pallas as pl
from jax.experimental.pallas import tpu as pltpu

F32 = jnp.float32
BF16 = jnp.bfloat16

D_MODEL = 2048
GRID_W = 64
EPS = 1e-6
CHUNK = 128
BRANCH_WIDTH = D_MODEL // 2
SSD_INNER = BRANCH_WIDTH
SSD_HEAD_DIM = 64
SSD_HEADS = SSD_INNER // SSD_HEAD_DIM
SSD_GROUPS = 4
SSD_STATE = 128
SSD_CONV_CH = SSD_INNER + 2 * SSD_GROUPS * SSD_STATE
POOL_WINDOWS = (2, 4, 8, 16)
POOL_GROUP = BRANCH_WIDTH // len(POOL_WINDOWS)
RET_HEADS = 8
RET_V_HEAD = BRANCH_WIDTH // RET_HEADS
RET_K_HEAD = RET_V_HEAD // 2
RET_QK = RET_HEADS * RET_K_HEAD
ROPE_BASE = 10000.0
D_FF = 256 * ((8 * D_MODEL // 3 + 255) // 256)

LANES = 128
HALO = 16
VMEM_LIMIT = 56 * 1024 * 1024

U_Z = 0
U_XBC = 1024
U_POOL = 3072
U_SCB = 4096
U_SCC = 5120
U_SCX = 6144
U_RQ = 7168
U_RK = 7680
U_RV = 8192
U_RG = 9216
U_COLS = 10240


def _params(sem):
    return pltpu.CompilerParams(dimension_semantics=sem, vmem_limit_bytes=VMEM_LIMIT)


def _silu(v):
    return v * jax.nn.sigmoid(v)


def _pick(cands, *ns):
    for c in cands:
        if all(n % c == 0 for n in ns):
            return c
    raise ValueError(f"no tile in {cands} divides {ns}")


def _mod_const(v, n):
    if n & (n - 1) == 0:
        return v & (n - 1)
    return lax.rem(v, n)


def _seq_pos(r0, rows, bc, ctx, seq):
    in_ctx = r0 < bc
    g = r0 + lax.broadcasted_iota(jnp.int32, (rows, 1), 0)
    pos = jnp.where(in_ctx, _mod_const(g, ctx), _mod_const(g - bc, seq))
    n = jnp.where(in_ctx, ctx, seq)
    return pos, n


def _mod_kernel(c_ref, w_ref, b_ref, o_ref):
    a = _silu(c_ref[...]).astype(BF16)
    o_ref[...] = jnp.dot(a, w_ref[...].astype(BF16), preferred_element_type=F32) + b_ref[...]


def _mod_call(cvec, w_mod, b_mod):
    nl, d, n6 = w_mod.shape
    r = cvec.shape[0]
    tn = 1024
    return pl.pallas_call(
        _mod_kernel,
        grid=(nl, n6 // tn),
        in_specs=[pl.BlockSpec((r, d), lambda l, j: (0, 0)),
                  pl.BlockSpec((None, d, tn), lambda l, j: (l, 0, j)),
                  pl.BlockSpec((None, 1, tn), lambda l, j: (l, 0, j))],
        out_specs=pl.BlockSpec((None, r, tn), lambda l, j: (l, 0, j)),
        out_shape=jax.ShapeDtypeStruct((nl, r, n6), F32),
        compiler_params=_params(("parallel", "parallel")),
        name="mod_vectors",
    )(cvec, w_mod, b_mod.reshape(nl, 1, n6))


def _normed(x, nw, sh, sc):
    y = x * lax.rsqrt(jnp.mean(jnp.square(x), axis=-1, keepdims=True) + EPS)
    return (y * nw) * (1.0 + sc) + sh


def _in_proj_kernel(x_ref, nw_ref, sh_ref, sc_ref, w_ref, wdt_ref, u_ref, h_ref, dt_ref):
    @pl.when(pl.program_id(1) == 0)
    def _():
        h = _normed(x_ref[...], nw_ref[...], sh_ref[...], sc_ref[...]).astype(BF16)
        h_ref[...] = h
        dt_ref[...] = jnp.dot(h, wdt_ref[...], preferred_element_type=F32)

    u_ref[...] = jnp.dot(h_ref[...], w_ref[...], preferred_element_type=F32).astype(BF16)


def _mod_spec(l, k, tn, row_fn, col_fn):
    return pl.BlockSpec((None, None, None, 1, tn),
                        lambda *g: (l, row_fn(*g), k, 0, col_fn(*g)))


def _mod_row(tile, bc, seq):
    def row(i):
        r = i * tile
        return jnp.where(r < bc, 0, 1 + (r - bc) // seq)
    return row


def _in_proj_call(x, norm_w, mod5, w, wdt, l, k_sh, k_sc, tm, row_of):
    t, d = x.shape
    n = w.shape[-1]
    tn = 1024
    zero = lambda i, j: 0
    row_of = (lambda f: lambda i, j: f(i))(row_of)
    return pl.pallas_call(
        _in_proj_kernel,
        grid=(t // tm, n // tn),
        in_specs=[pl.BlockSpec((tm, d), lambda i, j: (i, 0)),
                  pl.BlockSpec((None, 1, d), lambda i, j: (l, 0, 0)),
                  _mod_spec(l, k_sh, d, row_of, zero),
                  _mod_spec(l, k_sc, d, row_of, zero),
                  pl.BlockSpec((None, d, tn), lambda i, j: (l, 0, j)),
                  pl.BlockSpec((None, d, LANES), lambda i, j: (l, 0, 0))],
        out_specs=[pl.BlockSpec((tm, tn), lambda i, j: (i, j)),
                   pl.BlockSpec((tm, d), lambda i, j: (i, 0)),
                   pl.BlockSpec((tm, LANES), lambda i, j: (i, 0))],
        out_shape=[jax.ShapeDtypeStruct((t, n), BF16),
                   jax.ShapeDtypeStruct((t, d), BF16),
                   jax.ShapeDtypeStruct((t, LANES), F32)],
        compiler_params=_params(("parallel", "arbitrary")),
        name="in_proj",
    )(x, norm_w, mod5, mod5, w, wdt)


def _merge_kernel(h_ref, y0_ref, y1_ref, y2_ref, y3_ref, wg_ref, bg_ref, wb_ref, o_ref):
    h = h_ref[...]
    acc = None
    for i, y_ref in enumerate((y0_ref, y1_ref, y2_ref, y3_ref)):
        gate = jax.nn.sigmoid(jnp.dot(h, wg_ref[i], preferred_element_type=F32) + bg_ref[i])
        term = gate * jnp.dot(y_ref[...], wb_ref[i], preferred_element_type=F32)
        acc = term if acc is None else acc + term
    o_ref[...] = acc.astype(BF16)


def _merge_call(h, ys, wg, bg, wb, l, tm, row0=0):
    t, d = h.shape
    bw = ys[0].shape[-1]
    tn = 256
    nb = len(ys)
    off = row0 // tm
    t_out = t - row0
    return pl.pallas_call(
        _merge_kernel,
        grid=(t_out // tm, d // tn),
        in_specs=[pl.BlockSpec((tm, d), lambda i, j: (i + off, 0))]
                 + [pl.BlockSpec((tm, bw), lambda i, j: (i + off, 0)) for _ in ys]
                 + [pl.BlockSpec((None, nb, d, tn), lambda i, j: (l, 0, 0, j)),
                    pl.BlockSpec((None, nb, 1, tn), lambda i, j: (l, 0, 0, j)),
                    pl.BlockSpec((None, nb, bw, tn), lambda i, j: (l, 0, 0, j))],
        out_specs=pl.BlockSpec((tm, tn), lambda i, j: (i, j)),
        out_shape=jax.ShapeDtypeStruct((t_out, d), BF16),
        compiler_params=_params(("parallel", "arbitrary")),
        name="branch_merge",
    )(h, *ys, wg, bg, wb)


def _res_kernel(a_ref, w_ref, x_ref, g_ref, o_ref):
    o_ref[...] = x_ref[...] + g_ref[...] * jnp.dot(a_ref[...], w_ref[...],
                                                   preferred_element_type=F32)


def _res_call(a, w, x, mod5, l, k_gate, tm, row_of, name):
    t, kdim = a.shape
    d = x.shape[-1]
    tn = 512
    return pl.pallas_call(
        _res_kernel,
        grid=(t // tm, d // tn),
        in_specs=[pl.BlockSpec((tm, kdim), lambda i, j: (i, 0)),
                  pl.BlockSpec((None, kdim, tn), lambda i, j: (l, 0, j)),
                  pl.BlockSpec((tm, tn), lambda i, j: (i, j)),
                  _mod_spec(l, k_gate, tn, lambda i, j: row_of(i), lambda i, j: j)],
        out_specs=pl.BlockSpec((tm, tn), lambda i, j: (i, j)),
        out_shape=jax.ShapeDtypeStruct((t, d), F32),
        compiler_params=_params(("parallel", "arbitrary")),
        name=name,
    )(a, w, x, mod5)


EPI_ROWS = 128


def _outproj_norm_kernel(a_ref, w_ref, x_ref, g_ref, nw_ref, sh_ref, sc_ref, xo_ref, ho_ref, y0, y1):
    s = pl.program_id(0)
    tm = a_ref.shape[0]

    @pl.when(s == 0)
    def _():
        y1[...] = jnp.zeros_like(y1)

    def step(y_cur, y_prev):
        nslab = tm // EPI_ROWS
        cw = w_ref.shape[1] // nslab
        for k in range(nslab):
            rs = slice(k * EPI_ROWS, (k + 1) * EPI_ROWS)
            cs = slice(k * cw, (k + 1) * cw)
            y_cur[:, cs] = jnp.dot(a_ref[...], w_ref[:, cs], preferred_element_type=F32)
            xn = x_ref[rs, :] + g_ref[...] * y_prev[rs, :]
            xo_ref[rs, :] = xn
            ho_ref[rs, :] = _normed(xn, nw_ref[...], sh_ref[...], sc_ref[...]).astype(BF16)

    @pl.when(s % 2 == 0)
    def _():
        step(y0, y1)

    @pl.when(s % 2 == 1)
    def _():
        step(y1, y0)


def _outproj_norm_call(a, w, x, mod5, norm_w, l, k_gate, k_sh, k_sc, tm, bc, seq):
    t, kdim = a.shape
    d = x.shape[-1]
    ni = t // tm
    row0 = x.shape[0] - t
    assert row0 in (0, bc)
    off = row0 // tm
    row = _mod_row(tm, bc - row0, seq)
    cur = lambda s: jnp.minimum(s, ni - 1)
    prev = lambda s: jnp.maximum(s - 1, 0)
    zero = lambda s: 0
    return pl.pallas_call(
        _outproj_norm_kernel,
        grid=(ni + 1,),
        in_specs=[pl.BlockSpec((tm, kdim), lambda s: (cur(s), 0)),
                  pl.BlockSpec((None, kdim, d), lambda s: (l, 0, 0), pipeline_mode=pl.Buffered(1)),
                  pl.BlockSpec((tm, d), lambda s: (prev(s) + off, 0)),
                  _mod_spec(l, k_gate, d, lambda s: row(prev(s)), zero),
                  pl.BlockSpec((None, 1, d), lambda s: (l, 0, 0)),
                  _mod_spec(l, k_sh, d, lambda s: row(prev(s)), zero),
                  _mod_spec(l, k_sc, d, lambda s: row(prev(s)), zero)],
        out_specs=[pl.BlockSpec((tm, d), lambda s: (prev(s), 0)),
                   pl.BlockSpec((tm, d), lambda s: (prev(s), 0))],
        out_shape=[jax.ShapeDtypeStruct((t, d), F32),
                   jax.ShapeDtypeStruct((t, d), BF16)],
        scratch_shapes=[pltpu.VMEM((tm, d), F32), pltpu.VMEM((tm, d), F32)],
        compiler_params=_params(("arbitrary",)),
        name="out_proj_residual_norm",
    )(a, w, x, mod5, norm_w, mod5, mod5)


FFN_PIECE_ROWS = 64
FFN_SLABS = 4
SEQ_ALIGN = 256


def _ffn_gate_kernel(h_ref, hp_ref, hn_ref, wa_ref, wb_ref, cwa_ref, cwb_ref, ba_ref, bb_ref, o_ref,
                     h_s, a_s, b_s, *, nj, bc, ctx, seq):
    s = pl.program_id(0)
    tm = h_ref.shape[0]

    @pl.when(s == 0)
    def _():
        a_s[1] = jnp.zeros(a_s.shape[1:], F32)
        b_s[1] = jnp.zeros(b_s.shape[1:], F32)

    @pl.when(s % nj == 0)
    def _():
        h_s[0:HALO, :] = hp_ref[...]
        h_s[HALO:HALO + tm, :] = h_ref[...]
        h_s[HALO + tm:, :] = hn_ref[...]

    r0 = (jnp.maximum(s - 1, 0) // nj) * tm
    in_ctx = r0 < bc
    tn = o_ref.shape[1]
    rows_ext = tm + 2 * HALO
    pr = FFN_PIECE_ROWS
    row_first = lax.broadcasted_iota(jnp.int32, (pr, 1), 0) == 0
    row_last = lax.broadcasted_iota(jnp.int32, (pr, 1), 0) == pr - 1

    def seq_start(g):
        return jnp.where(in_ctx, _mod_const(g, ctx), _mod_const(g - bc, seq)) == 0

    def taps(w_ref, bias_ref, cs):
        return [jnp.broadcast_to(w_ref[j:j + 1, cs], (pr, LANES)) for j in range(3)] + \
               [jnp.broadcast_to(bias_ref[:, cs], (pr, LANES))]

    col_slabs = [slice(c, c + LANES) for c in range(0, tn, LANES)]
    taps_a = [taps(cwa_ref, ba_ref, cs) for cs in col_slabs]
    taps_b = [taps(cwb_ref, bb_ref, cs) for cs in col_slabs]

    def conv_piece(src, r, cs, tp, kill_prev, kill_next):
        base = HALO + r
        ext = src[base - 8:base + pr + 8, cs]
        xm1 = pltpu.roll(ext, 1, 0)[8:8 + pr]
        x00 = ext[8:8 + pr]
        xp1 = pltpu.roll(ext, pr + 15, 0)[8:8 + pr]
        if kill_prev is not None:
            xm1 = jnp.where(kill_prev, 0.0, xm1)
        if kill_next is not None:
            xp1 = jnp.where(kill_next, 0.0, xp1)
        return xm1 * tp[0] + x00 * tp[1] + xp1 * tp[2] + tp[3]

    def epilogue(a_prev, b_prev, lo, hi):
        for r in range(lo, hi, pr):
            kill_prev = kill_next = None
            if r % SEQ_ALIGN == 0:
                kill_prev = jnp.logical_and(row_first, seq_start(r0 + r))
            if (r + pr) % SEQ_ALIGN == 0:
                kill_next = jnp.logical_and(row_last, seq_start(r0 + r + pr))
            for ci, cs in enumerate(col_slabs):
                a = conv_piece(a_prev, r, cs, taps_a[ci], kill_prev, kill_next)
                b = conv_piece(b_prev, r, cs, taps_b[ci], kill_prev, kill_next)
                o_ref[r:r + pr, cs] = (_silu(a) * b).astype(BF16)

    def step(a_cur, b_cur, a_prev, b_prev):
        nslab = FFN_SLABS
        for k in range(nslab):
            m_lo = (k * rows_ext // nslab) // HALO * HALO
            m_hi = rows_ext if k == nslab - 1 else ((k + 1) * rows_ext // nslab) // HALO * HALO
            h = h_s[m_lo:m_hi, :]
            a_cur[m_lo:m_hi, :] = jnp.dot(h, wa_ref[...], preferred_element_type=F32)
            b_cur[m_lo:m_hi, :] = jnp.dot(h, wb_ref[...], preferred_element_type=F32)
            epilogue(a_prev, b_prev, k * tm // nslab, (k + 1) * tm // nslab)

    cur = s % 2
    step(a_s.at[cur], b_s.at[cur], a_s.at[1 - cur], b_s.at[1 - cur])


def _ffn_gate_call(h, w, cw, cb, l, tm, bc, ctx, seq):
    t, d = h.shape
    tn = 512
    nj = D_FF // tn
    ni = t // tm
    per = tm // HALO
    last = t // HALO - 1
    ci = lambda s: jnp.minimum(s // nj, ni - 1)
    cj = lambda s: s % nj
    pi = lambda s: jnp.maximum(s - 1, 0) // nj
    pj = lambda s: jnp.maximum(s - 1, 0) % nj
    return pl.pallas_call(
        functools.partial(_ffn_gate_kernel, nj=nj, bc=bc, ctx=ctx, seq=seq),
        grid=(ni * nj + 1,),
        in_specs=[pl.BlockSpec((tm, d), lambda s: (ci(s), 0)),
                  pl.BlockSpec((HALO, d), lambda s: (jnp.maximum(ci(s) * per - 1, 0), 0)),
                  pl.BlockSpec((HALO, d), lambda s: (jnp.minimum((ci(s) + 1) * per, last), 0)),
                  pl.BlockSpec((None, d, tn), lambda s: (l, 0, cj(s))),
                  pl.BlockSpec((None, d, tn), lambda s: (l, 0, nj + cj(s))),
                  pl.BlockSpec((None, 3, tn), lambda s: (l, 0, pj(s))),
                  pl.BlockSpec((None, 3, tn), lambda s: (l, 0, nj + pj(s))),
                  pl.BlockSpec((None, 1, tn), lambda s: (l, 0, pj(s))),
                  pl.BlockSpec((None, 1, tn), lambda s: (l, 0, nj + pj(s)))],
        out_specs=pl.BlockSpec((tm, tn), lambda s: (pi(s), pj(s))),
        out_shape=jax.ShapeDtypeStruct((t, D_FF), BF16),
        scratch_shapes=[pltpu.VMEM((tm + 2 * HALO, d), BF16)]
                       + [pltpu.VMEM((2, tm + 2 * HALO, tn), F32) for _ in range(2)],
        compiler_params=_params(("arbitrary",)),
        name="ffn_up_conv_gate",
    )(h, h, h, w, w, cw, cw, cb, cb)


def _final_norm_kernel(x_ref, w_ref, o_ref):
    x = x_ref[...]
    o_ref[...] = (x * lax.rsqrt(jnp.mean(jnp.square(x), axis=-1, keepdims=True) + EPS)) * w_ref[...]


def _final_norm_call(x, w, row0, rows, tm):
    d = x.shape[-1]
    off = row0 // tm
    return pl.pallas_call(
        _final_norm_kernel,
        grid=(rows // tm,),
        in_specs=[pl.BlockSpec((tm, d), lambda i: (i + off, 0)),
                  pl.BlockSpec((1, d), lambda i: (0, 0))],
        out_specs=pl.BlockSpec((tm, d), lambda i: (i, 0)),
        out_shape=jax.ShapeDtypeStruct((rows, d), F32),
        compiler_params=_params(("parallel",)),
        name="final_norm",
    )(x, w)


def _halo_specs(rt, cols, col_idx, t):
    per = rt // HALO
    last = t // HALO - 1
    return [pl.BlockSpec((rt, cols), lambda i, j: (i, col_idx(j))),
            pl.BlockSpec((HALO, cols), lambda i, j: (jnp.maximum(i * per - 1, 0), col_idx(j))),
            pl.BlockSpec((HALO, cols), lambda i, j: (jnp.minimum((i + 1) * per, last), col_idx(j)))]


def _shift_pm1(x, prev_row, next_row, pos, n):
    rt = x.shape[0]
    g = 8
    first = lax.broadcasted_iota(jnp.int32, (g, 1), 0) == 0
    last = lax.broadcasted_iota(jnp.int32, (g, 1), 0) == g - 1
    xm1 = pltpu.roll(x, 1, 0)
    xp1 = pltpu.roll(x, rt - 1, 0)
    m_parts, p_parts = [], []
    for s in range(0, rt, SEQ_ALIGN):
        e = s + SEQ_ALIGN
        head = xm1[s:s + g]
        row0 = prev_row if s == 0 else head[0:1]
        row0 = jnp.where(pos[s:s + 1] == 0, 0.0, row0)
        m_parts += [jnp.where(first, row0, head), xm1[s + g:e]]
        tail = xp1[e - g:e]
        row7 = next_row if e == rt else tail[g - 1:g]
        row7 = jnp.where(pos[e - 1:e] == n - 1, 0.0, row7)
        p_parts += [xp1[s:e - g], jnp.where(last, row7, tail)]
    return jnp.concatenate(m_parts, axis=0), jnp.concatenate(p_parts, axis=0)


def _conv3(x, prev_row, next_row, w, pos, n):
    xm1, xp1 = _shift_pm1(x, prev_row, next_row, pos, n)
    return xm1 * w[0:1] + x * w[1:2] + xp1 * w[2:3]


def _ssd_conv_kernel(x_ref, p_ref, n_ref, w_ref, b_ref, o_ref, *, bc, ctx, seq):
    rt = x_ref.shape[0]
    pos, n = _seq_pos(pl.program_id(0) * rt, rt, bc, ctx, seq)
    x = x_ref[...].astype(F32)
    prev_row = p_ref[...].astype(F32)[HALO - 1:HALO]
    next_row = n_ref[...].astype(F32)[0:1]
    y = _conv3(x, prev_row, next_row, w_ref[...], pos, n) + b_ref[...]
    o_ref[...] = _silu(y).astype(BF16)


def _ssd_conv_call(u, w, b, l, rt, bc, ctx, seq):
    t = u.shape[0]
    cols = 1024
    nj = SSD_CONV_CH // cols
    base = U_XBC // cols
    return pl.pallas_call(
        functools.partial(_ssd_conv_kernel, bc=bc, ctx=ctx, seq=seq),
        grid=(t // rt, nj),
        in_specs=_halo_specs(rt, cols, lambda j: base + j, t)
                 + [pl.BlockSpec((None, 3, cols), lambda i, j: (l, 0, j)),
                    pl.BlockSpec((None, 1, cols), lambda i, j: (l, 0, j))],
        out_specs=pl.BlockSpec((rt, cols), lambda i, j: (i, j)),
        out_shape=jax.ShapeDtypeStruct((t, SSD_CONV_CH), BF16),
        compiler_params=_params(("parallel", "parallel")),
        name="ssd_conv",
    )(u, u, u, w, b)


def _sconv_body(r0, b_ref, c_ref, cp_ref, cn_ref, x_ref, xp_ref, xn_ref, w_ref, o_ref, *, bc, ctx, seq):
    rt = b_ref.shape[0]
    pos, n = _seq_pos(r0, rt, bc, ctx, seq)
    cx = c_ref[...].astype(F32) * x_ref[...].astype(F32)
    prev_row = (cp_ref[...].astype(F32) * xp_ref[...].astype(F32))[HALO - 1:HALO]
    next_row = (cn_ref[...].astype(F32) * xn_ref[...].astype(F32))[0:1]
    y = b_ref[...].astype(F32) * _conv3(cx, prev_row, next_row, w_ref[...], pos, n)
    o_ref[...] = y.astype(BF16)


POOL_ROWS = 256
POOL_EXT = 384


def _pool_body(r0, x_ref, p_ref, n_ref, band_ref, cnt_ref, w_ref, s_ref, o_ref, *, bc, ctx, seq):
    rt = POOL_ROWS
    in_ctx = r0 < bc
    p0 = jnp.where(in_ctx, _mod_const(r0, ctx), _mod_const(r0 - bc, seq))
    n = jnp.where(in_ctx, ctx, seq)
    has_prev = p0 != 0
    has_next = p0 + rt != n
    zero = jnp.zeros((HALO, POOL_GROUP), BF16)
    pad = jnp.zeros((POOL_EXT - rt - 2 * HALO, POOL_GROUP), BF16)
    case = jnp.where(has_prev, 0, 2) + jnp.where(has_next, 0, 1)
    groups = [slice(gi * POOL_GROUP, (gi + 1) * POOL_GROUP) for gi in range(len(POOL_WINDOWS))]
    sums = []
    for gi, cs in enumerate(groups):
        ext = jnp.concatenate([jnp.where(has_prev, p_ref[:, cs], zero), x_ref[:, cs],
                               jnp.where(has_next, n_ref[:, cs], zero), pad], axis=0)
        sums.append(jnp.dot(band_ref[gi], ext, preferred_element_type=F32))
    for gi, cs in enumerate(groups):
        cnt = cnt_ref[gi, case]
        mean = sums[gi] / jnp.concatenate([cnt] * (POOL_GROUP // LANES), axis=1)
        pooled = (mean - x_ref[:, cs].astype(F32)).astype(BF16)
        y = jnp.dot(pooled, w_ref[gi], preferred_element_type=F32) * s_ref[:, cs]
        o_ref[:, cs] = y.astype(BF16)


def _pool_bands():
    t = jnp.arange(POOL_ROWS)[:, None]
    s = jnp.arange(POOL_EXT)[None, :] - HALO
    return jnp.stack([((s >= t - w // 2) & (s < t - w // 2 + w)) for w in POOL_WINDOWS]).astype(BF16)


def _pool_counts():
    t = jnp.arange(POOL_ROWS)
    tabs = []
    for w in POOL_WINDOWS:
        before = jnp.maximum(w // 2 - t, 0)
        after = jnp.maximum(t - w // 2 + w - POOL_ROWS, 0)
        cases = [w - (before if c >= 2 else 0) - (after if c % 2 else 0) for c in range(4)]
        tabs.append(jnp.stack([jnp.broadcast_to(c, (POOL_ROWS,)) for c in cases]))
    tab = jnp.stack(tabs).astype(F32)
    return jnp.broadcast_to(tab[..., None], tab.shape + (LANES,))


def _scan_positions(ncc, nk):
    def kpos(sw, s):
        back = jnp.where(s < ncc, ncc - 1 - s, nk - 1 - (s - ncc))
        return jnp.where(sw == 0, back, s)
    return kpos


def _chunk_row(b, k, ncc, ncl, nb):
    return jnp.where(k < ncc, b * ncc + k, nb * ncc + b * ncl + (k - ncc))


def _tri_masks():
    row = lax.broadcasted_iota(jnp.int32, (CHUNK, CHUNK), 0)
    col = lax.broadcasted_iota(jnp.int32, (CHUNK, CHUNK), 1)
    return row, col, col <= row, col >= row


MASKED = -1e30
LOG2E = 1.4426950408889634


SCAN_CHUNKS = 2


def _ssd_scan_parts(xbc_ref, dt_ref, z_ref, bias_ref, alog_ref, dexp_ref, nw_ref, sp_ref, sg_ref, o_ref,
                    hb_all, hf_s, hb_s, y_s, vw_s, pre_s, *, kpos):
    row, col, lower, upper = _tri_masks()
    lane_lo = col < SSD_HEAD_DIM
    row_lo1 = lane_lo[0:1]
    nh = SSD_HEADS
    n_pairs = nh // 2
    pairs_per_group = n_pairs // SSD_GROUPS
    gw = SSD_INNER // SSD_GROUPS

    def chunk_rows(c):
        return slice(c * CHUNK, (c + 1) * CHUNK)

    def split2(x):
        hi = x.astype(BF16).astype(F32)
        return hi, x - hi

    def split3(x):
        hi, r = split2(x)
        mid, lo = split2(r)
        return hi, mid, lo

    def to_tokens(parts):
        used = sum(p.shape[0] for p in parts)
        return jnp.concatenate(list(parts) + [jnp.zeros((CHUNK - used, CHUNK), F32)], axis=0).T.astype(BF16)

    def store_state_operand(wt, rc):
        lhs = to_tokens((jnp.zeros((4 * nh, CHUNK), F32),) + split2(wt))
        w_full = jnp.dot(lhs, sp_ref[:, 2 * SSD_INNER:3 * SSD_INNER], preferred_element_type=F32)
        vw_s[...] = (xbc_ref[rc, 0:SSD_INNER].astype(F32) * w_full).astype(BF16)

    def seg_all(cum):
        parts = split3(cum)
        ones = jnp.ones((nh, CHUNK), F32)
        a = to_tokens(parts + (ones, ones, ones))
        neg = jnp.concatenate([-p for p in parts], axis=0)
        tiled = jnp.concatenate([neg] * nh, axis=1)
        rows_b = jnp.where(sg_ref[3 * nh:6 * nh, :] != 0, tiled, 0.0).astype(BF16)
        b = jnp.concatenate([sg_ref[0:3 * nh, :], rows_b,
                             jnp.zeros((CHUNK - 6 * nh, nh * CHUNK), BF16)], axis=0)
        return jnp.dot(a, b, preferred_element_type=F32)

    def pair_scalar(x, p, j):
        return jnp.where(row_lo1, jnp.broadcast_to(x[2 * p:2 * p + 1, j:j + 1], (1, LANES)),
                         jnp.broadcast_to(x[2 * p + 1:2 * p + 2, j:j + 1], (1, LANES)))

    def state_update(h_s, g, dec_rows, rc):
        gs = slice(g * gw, (g + 1) * gw)
        bm = xbc_ref[rc, SSD_INNER + g * SSD_STATE:SSD_INNER + (g + 1) * SSD_STATE]
        st = lax.dot_general(bm, vw_s[:, gs], (((0,), (0,)), ((), ())), preferred_element_type=F32)
        h_s[:, gs] = h_s[:, gs] * jnp.concatenate(dec_rows, axis=1) + st

    def init_backward():
        hb_s[...] = jnp.zeros_like(hb_s)

    def init_forward():
        hf_s[...] = jnp.zeros_like(hf_s)

    def backward_step():
        for c in reversed(range(SCAN_CHUNKS)):
            rc = chunk_rows(c)
            ck = kpos * SCAN_CHUNKS + c
            dtt = dt_ref[rc, :].T[0:nh]
            dt2 = jax.nn.softplus(jnp.concatenate([dtt, dtt], axis=0) + bias_ref[...])
            la2 = dt2 * (-jnp.exp(alog_ref[...]))
            tri2 = jnp.concatenate([upper.astype(F32), lower.astype(F32)], axis=1)
            cum2 = jnp.dot(la2, tri2, precision=lax.Precision.HIGHEST, preferred_element_type=F32)
            cum_f = cum2[0:nh, 0:CHUNK]
            cum_b = cum2[nh:2 * nh, CHUNK:2 * CHUNK]
            pre_s[ck, 0:2 * nh] = dt2
            pre_s[ck, 2 * nh:3 * nh] = cum_f
            pre_s[ck, 3 * nh:4 * nh] = cum_b

            hb_all[ck] = hb_s[...].astype(BF16)
            store_state_operand(jnp.exp(cum_b[:, 0:1] - cum_b) * dt2[nh:2 * nh], rc)
            for g in range(SSD_GROUPS):
                dec_rows = [jnp.exp(pair_scalar(cum_b, g * pairs_per_group + pp, 0))
                            for pp in range(pairs_per_group)]
                state_update(hb_s, g, dec_rows, rc)

    def forward_step():
        for c in range(SCAN_CHUNKS):
            rc = chunk_rows(c)
            ck = kpos * SCAN_CHUNKS + c
            dt2 = pre_s[ck, 0:2 * nh]
            cum_f = pre_s[ck, 2 * nh:3 * nh]
            cum_b = pre_s[ck, 3 * nh:4 * nh]
            hb_in = hb_all[ck]
            wt = jnp.exp(cum_f[:, CHUNK - 1:CHUNK] - cum_f) * dt2[0:nh]
            ef, eb = split2(jnp.exp(cum_f)), split2(jnp.exp(cum_b))
            lhs = to_tokens((ef[0], eb[0], ef[1], eb[1]) + split2(wt))
            spread = jnp.dot(lhs, sp_ref[...], preferred_element_type=F32)
            vw_s[...] = (xbc_ref[rc, 0:SSD_INNER].astype(F32)
                         * spread[:, 2 * SSD_INNER:3 * SSD_INNER]).astype(BF16)
            seg_f = seg_all(cum_f * LOG2E)
            seg_b = seg_all(cum_b * LOG2E)
            for g in range(SSD_GROUPS):
                gs = slice(g * gw, (g + 1) * gw)
                bm = xbc_ref[rc, SSD_INNER + g * SSD_STATE:SSD_INNER + (g + 1) * SSD_STATE]
                cm = xbc_ref[rc, SSD_INNER + SSD_GROUPS * SSD_STATE + g * SSD_STATE:
                             SSD_INNER + SSD_GROUPS * SSD_STATE + (g + 1) * SSD_STATE]
                sc = lax.dot_general(cm, bm, (((1,), (1,)), ((), ())), preferred_element_type=F32)
                hcat = jnp.concatenate([hf_s[:, gs].astype(BF16), hb_in[:, gs]], axis=1)
                gfb = jnp.dot(cm, hcat, preferred_element_type=F32)
                dec_rows = []
                for pp in range(pairs_per_group):
                    p = g * pairs_per_group + pp
                    ps = slice(p * LANES, (p + 1) * LANES)
                    xs_pair = xbc_ref[rc, ps]
                    ms = []
                    for h in (2 * p, 2 * p + 1):
                        hs = slice(h * CHUNK, (h + 1) * CHUNK)
                        dec_f = jnp.exp2(jnp.where(lower, seg_f[:, hs], MASKED))
                        dec_b = jnp.exp2(jnp.where(upper, seg_b[:, hs], MASKED))
                        m = sc * (dec_f * dt2[h:h + 1] + dec_b * dt2[nh + h:nh + h + 1])
                        ms.append(m.astype(BF16))
                    zero = jnp.zeros_like(xs_pair)
                    x_split = jnp.concatenate([jnp.where(lane_lo, xs_pair, zero),
                                               jnp.where(lane_lo, zero, xs_pair)], axis=0)
                    y_intra = jnp.dot(jnp.concatenate(ms, axis=1), x_split, preferred_element_type=F32)
                    lo = pp * LANES
                    y_inter = (gfb[:, lo:lo + LANES] * spread[:, ps]
                               + gfb[:, gw + lo:gw + lo + LANES]
                               * spread[:, SSD_INNER + p * LANES:SSD_INNER + (p + 1) * LANES])
                    y_s[:, ps] = y_intra + y_inter
                    dec_rows.append(jnp.exp(pair_scalar(cum_f, p, CHUNK - 1)))
                state_update(hf_s, g, dec_rows, rc)

            xs32 = xbc_ref[rc, 0:SSD_INNER].astype(F32)
            y = (y_s[...] + xs32 * dexp_ref[...]) * _silu(z_ref[rc, :].astype(F32))
            y = y * lax.rsqrt(jnp.mean(jnp.square(y), axis=-1, keepdims=True) + EPS)
            o_ref[rc, :] = (y * nw_ref[...]).astype(BF16)

    return init_backward, backward_step, init_forward, forward_step


def _ssd_spread_constants():
    nh = SSD_HEADS
    r = jnp.arange(CHUNK)[:, None]
    c = jnp.arange(3 * SSD_INNER)[None, :]
    block_of_row = jnp.array([0, 1, 0, 1, 2, 2, -1, -1])[r // nh]
    sp = (block_of_row == c // SSD_INNER) & (r % nh == (c % SSD_INNER) // SSD_HEAD_DIM)
    c2 = jnp.arange(nh * CHUNK)[None, :]
    sg = (r < 6 * nh) & (r % nh == c2 // CHUNK)
    return sp.astype(BF16), sg.astype(BF16)


N_SSD_IN, N_SSD_SCRATCH = 9, 6
N_RET_IN, N_RET_SCRATCH = 7, 5


def _ssd_scan_specs(l, rows, in_row, nk):
    in_specs = [pl.BlockSpec((rows, SSD_CONV_CH), lambda b, sw, s: (in_row(b, sw, s), 0)),
                pl.BlockSpec((rows, LANES), lambda b, sw, s: (in_row(b, sw, s), 0)),
                pl.BlockSpec((rows, SSD_INNER), lambda b, sw, s: (in_row(b, sw, s), U_Z // SSD_INNER)),
                pl.BlockSpec((None, 2 * SSD_HEADS, CHUNK), lambda b, sw, s: (l, 0, 0)),
                pl.BlockSpec((None, 2 * SSD_HEADS, CHUNK), lambda b, sw, s: (l, 0, 0)),
                pl.BlockSpec((None, 1, SSD_INNER), lambda b, sw, s: (l, 0, 0)),
                pl.BlockSpec((None, 1, SSD_INNER), lambda b, sw, s: (l, 0, 0)),
                pl.BlockSpec((CHUNK, 3 * SSD_INNER), lambda b, sw, s: (0, 0)),
                pl.BlockSpec((CHUNK, SSD_HEADS * CHUNK), lambda b, sw, s: (0, 0))]
    scratch = [pltpu.VMEM((nk * SCAN_CHUNKS, SSD_STATE, SSD_INNER), BF16),
               pltpu.VMEM((SSD_STATE, SSD_INNER), F32),
               pltpu.VMEM((SSD_STATE, SSD_INNER), F32),
               pltpu.VMEM((CHUNK, SSD_INNER), F32),
               pltpu.VMEM((CHUNK, SSD_INNER), BF16),
               pltpu.VMEM((nk * SCAN_CHUNKS, 4 * SSD_HEADS, CHUNK), F32)]
    assert len(in_specs) == N_SSD_IN and len(scratch) == N_SSD_SCRATCH
    return in_specs, scratch


def _ret_scan_parts(q_ref, k_ref, v_ref, g_ref, cos_ref, sin_ref, dl_ref, o_ref,
                    hb_all, hf_s, hb_s, dec_s, tab_s, *, kpos):
    row, col, lower, upper = _tri_masks()
    lane_lo = col < RET_K_HEAD
    la = jax.nn.log_sigmoid(dl_ref[...])
    rowf = row.astype(F32)
    diff = (row - col).astype(F32)
    n_pairs = RET_HEADS // 2

    first16 = (col & 31) < 16

    def chunk_rows(c):
        return slice(c * CHUNK, (c + 1) * CHUNK)

    def rope(x, rc):
        swapped = jnp.where(first16, pltpu.roll(x, LANES - 16, 1), pltpu.roll(x, 16, 1))
        return x * cos_ref[rc, :] + swapped * sin_ref[rc, :]

    def head_scalar(d, h):
        return la[d:d + 1, h:h + 1]

    def pair_k(p, rc):
        return rope(k_ref[rc, p * LANES:(p + 1) * LANES].astype(F32), rc).astype(BF16)

    def state_update(h_s, d, p, kp, w_tab, rc):
        h0, h1 = 2 * p, 2 * p + 1
        v_of = lambda h: v_ref[rc, h * RET_V_HEAD:(h + 1) * RET_V_HEAD]
        vw = jnp.concatenate([(v_of(h0).astype(F32) * tab_s[w_tab, h0]).astype(BF16),
                              (v_of(h1).astype(F32) * tab_s[w_tab, h1]).astype(BF16)], axis=1)
        st = lax.dot_general(kp, vw, (((0,), (0,)), ((), ())), preferred_element_type=F32)
        top = row < RET_K_HEAD
        acc = jnp.where(top, st[:, 0:RET_V_HEAD], st[:, RET_V_HEAD:2 * RET_V_HEAD])
        dec = jnp.where(top, jnp.exp(float(CHUNK) * head_scalar(d, h0)),
                        jnp.exp(float(CHUNK) * head_scalar(d, h1)))
        rs = slice(p * LANES, (p + 1) * LANES)
        h_s[rs, :] = h_s[rs, :] * dec + acc

    def init_backward():
        hb_s[...] = jnp.zeros_like(hb_s)
        for h in range(RET_HEADS):
            laf = head_scalar(0, h)
            lab = head_scalar(1, h)
            dec_s[h] = (jnp.exp(jnp.where(lower, diff * laf, MASKED))
                        + jnp.exp(jnp.where(upper, -diff * lab, MASKED)))
            tab_s[0, h] = jnp.exp((rowf + 1.0) * laf)
            tab_s[1, h] = jnp.exp((float(CHUNK) - rowf) * lab)
            tab_s[2, h] = jnp.exp((float(CHUNK - 1) - rowf) * laf)
            tab_s[3, h] = jnp.exp(rowf * lab)

    def init_forward():
        hf_s[...] = jnp.zeros_like(hf_s)

    def backward_step():
        for c in reversed(range(SCAN_CHUNKS)):
            rc = chunk_rows(c)
            hb_all[kpos * SCAN_CHUNKS + c] = hb_s[...].astype(BF16)
            for p in range(n_pairs):
                state_update(hb_s, 1, p, pair_k(p, rc), 3, rc)

    def forward_step():
        for c in range(SCAN_CHUNKS):
            rc = chunk_rows(c)
            hb_in = hb_all[kpos * SCAN_CHUNKS + c]
            for p in range(n_pairs):
                h0, h1 = 2 * p, 2 * p + 1
                v_of = lambda h: v_ref[rc, h * RET_V_HEAD:(h + 1) * RET_V_HEAD]
                qp = rope(q_ref[rc, p * LANES:(p + 1) * LANES].astype(F32) * (RET_K_HEAD ** -0.5), rc)
                kp = pair_k(p, rc)
                rs = slice(p * LANES, (p + 1) * LANES)
                qcat = jnp.concatenate([jnp.where(lane_lo, qp, 0.0), jnp.where(lane_lo, 0.0, qp)],
                                       axis=0).astype(BF16)
                sc = lax.dot_general(qcat, kp, (((1,), (1,)), ((), ())), preferred_element_type=F32)
                hcat = jnp.concatenate([hf_s[rs, :].astype(BF16), hb_in[rs, :]], axis=1)
                gfb = jnp.dot(qcat, hcat, preferred_element_type=F32)
                m = jnp.concatenate([(sc[0:CHUNK] * dec_s[h0]).astype(BF16),
                                     (sc[CHUNK:2 * CHUNK] * dec_s[h1]).astype(BF16)], axis=1)
                zero = jnp.zeros((CHUNK, RET_V_HEAD), BF16)
                v_blk = jnp.concatenate([jnp.concatenate([v_of(h0), zero], axis=1),
                                         jnp.concatenate([zero, v_of(h1)], axis=1)], axis=0)
                y_intra = jnp.dot(m, v_blk, preferred_element_type=F32)
                for i, h in enumerate((h0, h1)):
                    vs = slice(h * RET_V_HEAD, (h + 1) * RET_V_HEAD)
                    g_h = gfb[i * CHUNK:(i + 1) * CHUNK]
                    y = (y_intra[:, i * RET_V_HEAD:(i + 1) * RET_V_HEAD]
                         + g_h[:, 0:RET_V_HEAD] * tab_s[0, h]
                         + g_h[:, RET_V_HEAD:2 * RET_V_HEAD] * tab_s[1, h])
                    mu = jnp.mean(y, axis=-1, keepdims=True)
                    yc = y - mu
                    var = jnp.mean(jnp.square(yc), axis=-1, keepdims=True)
                    yn = yc * lax.rsqrt(var + EPS)
                    o_ref[rc, vs] = (_silu(g_ref[rc, vs].astype(F32)) * yn).astype(BF16)
                state_update(hf_s, 0, p, kp, 2, rc)

    return init_backward, backward_step, init_forward, forward_step


def _ret_scan_specs(l, rows, in_row, kpos, nk):
    bw = BRANCH_WIDTH
    in_specs = [pl.BlockSpec((rows, RET_QK), lambda b, sw, s: (in_row(b, sw, s), U_RQ // RET_QK)),
                pl.BlockSpec((rows, RET_QK), lambda b, sw, s: (in_row(b, sw, s), U_RK // RET_QK)),
                pl.BlockSpec((rows, bw), lambda b, sw, s: (in_row(b, sw, s), U_RV // bw)),
                pl.BlockSpec((rows, bw), lambda b, sw, s: (in_row(b, sw, s), U_RG // bw)),
                pl.BlockSpec((rows, LANES), lambda b, sw, s: (kpos(sw, s), 0)),
                pl.BlockSpec((rows, LANES), lambda b, sw, s: (kpos(sw, s), 0)),
                pl.BlockSpec((None, 2, LANES), lambda b, sw, s: (l, 0, 0))]
    scratch = [pltpu.VMEM((nk * SCAN_CHUNKS, RET_QK, RET_V_HEAD), BF16),
               pltpu.VMEM((RET_QK, RET_V_HEAD), F32),
               pltpu.VMEM((RET_QK, RET_V_HEAD), F32),
               pltpu.VMEM((RET_HEADS, CHUNK, CHUNK), F32),
               pltpu.VMEM((4, RET_HEADS, CHUNK, LANES), F32)]
    assert len(in_specs) == N_RET_IN and len(scratch) == N_RET_SCRATCH
    return in_specs, scratch


N_POOL_IN, N_SCONV_IN = 7, 8


def _scans_kernel(*refs, ncc, ncl, nk, nb, bc, ctx, seq):
    bounds = [0]
    for n in (N_SSD_IN, N_RET_IN, N_POOL_IN, N_SCONV_IN):
        bounds.append(bounds[-1] + n)
    ssd_in, ret_in, pool_in, sconv_in = (refs[bounds[i]:bounds[i + 1]] for i in range(4))
    n_in = bounds[-1]
    ssd_out, ret_out, pool_out, sconv_out = refs[n_in:n_in + 4]
    scratch = refs[n_in + 4:]
    ssd_scr, ret_scr = scratch[:N_SSD_SCRATCH], scratch[N_SSD_SCRATCH:]
    sw = pl.program_id(1)
    s = pl.program_id(2)
    kpos = _scan_positions(ncc, nk)(sw, s)
    ssd = _ssd_scan_parts(*ssd_in, ssd_out, *ssd_scr, kpos=kpos)
    ret = _ret_scan_parts(*ret_in, ret_out, *ret_scr, kpos=kpos)
    r0 = _chunk_row(pl.program_id(0), kpos, ncc, ncl, nb) * (SCAN_CHUNKS * CHUNK)

    @pl.when(jnp.logical_and(sw == 0, s == 0))
    def _():
        ssd[0]()
        ret[0]()

    @pl.when(sw == 0)
    def _():
        _sconv_body(r0, *sconv_in, sconv_out, bc=bc, ctx=ctx, seq=seq)
        ssd[1]()
        _pool_body(r0, *pool_in, pool_out, bc=bc, ctx=ctx, seq=seq)
        ret[1]()

    @pl.when(jnp.logical_and(sw == 1, s == 0))
    def _():
        ssd[2]()
        ret[2]()

    @pl.when(sw == 1)
    def _():
        ssd[3]()
        ret[3]()


def _mixers_call(xbc, dt, u, dt_bias, a_log, d_exp, norm_w, cos_t, sin_t, decay_logit,
                 pool_w, pool_scale, sconv_w, l, nb, bc, ctx, seq):
    t = u.shape[0]
    rows = SCAN_CHUNKS * CHUNK
    assert rows == POOL_ROWS and SSD_INNER == BRANCH_WIDTH
    ncc, ncl = ctx // rows, seq // rows
    nk = ncc + ncl
    kpos = _scan_positions(ncc, nk)
    bw = BRANCH_WIDTH
    per = rows // HALO
    last = t // HALO - 1

    def in_row(b, sw, s):
        return _chunk_row(b, kpos(sw, s), ncc, ncl, nb)

    def out_row(b, sw, s):
        return _chunk_row(b, jnp.where(sw == 0, 0, s), ncc, ncl, nb)

    def once_row(b, sw, s):
        return in_row(b, 0, jnp.where(sw == 0, s, nk - 1))

    def halo(col):
        c = col // bw
        return [pl.BlockSpec((rows, bw), lambda b, sw, s: (once_row(b, sw, s), c)),
                pl.BlockSpec((HALO, bw), lambda b, sw, s: (jnp.maximum(once_row(b, sw, s) * per - 1, 0), c)),
                pl.BlockSpec((HALO, bw), lambda b, sw, s: (jnp.minimum((once_row(b, sw, s) + 1) * per, last), c))]

    ng = len(POOL_WINDOWS)
    ssd_in, ssd_scr = _ssd_scan_specs(l, rows, in_row, nk)
    ret_in, ret_scr = _ret_scan_specs(l, rows, in_row, kpos, nk)
    pool_in = halo(U_POOL) + [pl.BlockSpec((ng, rows, POOL_EXT), lambda b, sw, s: (0, 0, 0)),
                              pl.BlockSpec((ng, 4, rows, LANES), lambda b, sw, s: (0, 0, 0, 0)),
                              pl.BlockSpec((None, ng, POOL_GROUP, POOL_GROUP), lambda b, sw, s: (l, 0, 0, 0)),
                              pl.BlockSpec((None, 1, bw), lambda b, sw, s: (l, 0, 0))]
    sconv_in = ([pl.BlockSpec((rows, bw), lambda b, sw, s: (once_row(b, sw, s), U_SCB // bw))]
                + halo(U_SCC) + halo(U_SCX) + [pl.BlockSpec((None, 3, bw), lambda b, sw, s: (l, 0, 0))])
    assert len(pool_in) == N_POOL_IN and len(sconv_in) == N_SCONV_IN
    scan_out = pl.BlockSpec((rows, bw), lambda b, sw, s: (out_row(b, sw, s), 0))
    once_out = pl.BlockSpec((rows, bw), lambda b, sw, s: (once_row(b, sw, s), 0))
    return pl.pallas_call(
        functools.partial(_scans_kernel, ncc=ncc, ncl=ncl, nk=nk, nb=nb, bc=bc, ctx=ctx, seq=seq),
        grid=(nb, 2, nk),
        in_specs=ssd_in + ret_in + pool_in + sconv_in,
        out_specs=[scan_out, scan_out, once_out, once_out],
        out_shape=[jax.ShapeDtypeStruct((t, bw), BF16)] * 4,
        scratch_shapes=ssd_scr + ret_scr,
        compiler_params=_params(("parallel", "arbitrary", "arbitrary")),
        name="mixer_branches",
    )(xbc, dt, u, dt_bias, a_log, d_exp, norm_w, *_ssd_spread_constants(),
      u, u, u, u, cos_t, sin_t, decay_logit,
      u, u, u, _pool_bands(), _pool_counts(), pool_w, pool_scale,
      u, u, u, u, u, u, u, sconv_w)


def _rope_tables(ctx, seq):
    quarter = RET_K_HEAD // 4
    t = jnp.arange(seq)
    rowp = (t // GRID_W).astype(F32)
    colp = (t % GRID_W).astype(F32)
    inv = ROPE_BASE ** (-jnp.arange(quarter, dtype=F32) / quarter)
    ang_r = rowp[:, None] * inv[None, :]
    ang_c = colp[:, None] * inv[None, :]

    def blocks(ang):
        c, s = jnp.cos(ang), jnp.sin(ang)
        return jnp.concatenate([c, c], axis=-1), jnp.concatenate([-s, s], axis=-1)

    cr, sr = blocks(ang_r)
    cc, sc = blocks(ang_c)
    cos_h = jnp.concatenate([cr, cc], axis=-1)
    sin_h = jnp.concatenate([sr, sc], axis=-1)
    cos_l = jnp.concatenate([cos_h, cos_h], axis=-1)
    sin_l = jnp.concatenate([sin_h, sin_h], axis=-1)
    cos_t = jnp.concatenate([jnp.ones((ctx, LANES), F32), cos_l], axis=0)
    sin_t = jnp.concatenate([jnp.zeros((ctx, LANES), F32), sin_l], axis=0)
    return cos_t, sin_t


def _pad_lanes(a):
    return jnp.pad(a, [(0, 0)] * (a.ndim - 1) + [(0, LANES - a.shape[-1])])


def kernel(x, c, ctx, c_ctx, w_mod, b_mod, norm1_w, w_in, ssd_conv_w, ssd_conv_b, ssd_a_log,
           ssd_dt_bias, ssd_d, ssd_norm_w, pool_w, pool_scale, sconv_w, ret_decay_logit,
           w_branch, w_gate, b_gate, w_o, norm2_w, ffn_up, ffn_conv_w, ffn_conv_b, ffn_down,
           final_norm_w):
    nb, seq, d = x.shape
    nctx = ctx.shape[1]
    depth = w_mod.shape[0]
    assert d == D_MODEL and seq % GRID_W == 0
    assert nctx % POOL_ROWS == 0 and seq % POOL_ROWS == 0
    bc = nb * nctx
    t = bc + nb * seq
    tm = _pick((1024, 512, 256), bc, seq)
    rt = _pick((512, 256), bc, seq)

    def row_of(i):
        r = i * tm
        return jnp.where(r < bc, 0, 1 + (r - bc) // seq)

    sizes = (SSD_INNER, SSD_CONV_CH, SSD_HEADS, BRANCH_WIDTH, BRANCH_WIDTH, BRANCH_WIDTH,
             BRANCH_WIDTH, RET_QK, RET_QK, BRANCH_WIDTH, BRANCH_WIDTH)
    offs = [0]
    for sz in sizes:
        offs.append(offs[-1] + sz)
    parts = [w_in[:, :, offs[i]:offs[i + 1]] for i in range(len(sizes))]
    w_main = jnp.concatenate(parts[:2] + parts[3:], axis=-1).astype(BF16)
    w_dt = _pad_lanes(parts[2]).astype(BF16)
    assert w_main.shape[-1] == U_COLS
    wg_b = w_gate.astype(BF16)
    wb_b = w_branch.astype(BF16)
    wo_b = w_o.astype(BF16)
    up_b = ffn_up.astype(BF16)
    down_b = ffn_down.astype(BF16)
    poolw_b = pool_w.astype(BF16)

    nrows = 8 * ((1 + nb + 7) // 8)
    cvec = jnp.zeros((nrows, d), F32).at[0].set(c_ctx).at[1:1 + nb].set(c)
    mod = _mod_call(cvec, w_mod, b_mod)
    mod5 = mod.reshape(depth, nrows, 6, 1, d)

    cos_t, sin_t = _rope_tables(nctx, seq)
    rep = lambda a: jnp.broadcast_to(a.reshape(depth, 2 * SSD_HEADS, 1), (depth, 2 * SSD_HEADS, CHUNK))
    dt_bias_p = rep(ssd_dt_bias)
    a_log_p = rep(ssd_a_log)
    decay_p = _pad_lanes(ret_decay_logit)
    d_exp = jnp.repeat(ssd_d, SSD_HEAD_DIM, axis=-1)[:, None, :]

    r3 = lambda a: a[:, None, :]
    xs = jnp.concatenate([ctx.reshape(bc, d), x.reshape(nb * seq, d)], axis=0)
    for l in range(depth):
        u, h, dt = _in_proj_call(xs, r3(norm1_w), mod5, w_main, w_dt, l, 0, 1, tm, row_of)
        xbc = _ssd_conv_call(u, ssd_conv_w, r3(ssd_conv_b), l, rt, bc, nctx, seq)
        y_ssd, y_ret, y_pool, y_sc = _mixers_call(
            xbc, dt, u, dt_bias_p, a_log_p, d_exp, r3(ssd_norm_w), cos_t, sin_t, decay_p,
            poolw_b, r3(pool_scale), sconv_w, l, nb, bc, nctx, seq)
        lat = l == depth - 1
        bc_l = 0 if lat else bc
        row_l = _mod_row(tm, bc_l, seq)
        merged = _merge_call(h, (y_ssd, y_pool, y_sc, y_ret), wg_b, b_gate[:, :, None, :], wb_b, l, tm,
                             row0=bc if lat else 0)
        xs, h2 = _outproj_norm_call(merged, wo_b, xs, mod5, r3(norm2_w), l, 2, 3, 4, rt, bc, seq)
        gact = _ffn_gate_call(h2, up_b, ffn_conv_w, r3(ffn_conv_b), l, tm, bc_l, nctx, seq)
        xs = _res_call(gact, down_b, xs, mod5, l, 5, tm, row_l, "ffn_down_residual")
    out = _final_norm_call(xs, final_norm_w[None, :], 0, nb * seq, tm)
    return out.reshape(nb, seq, d)
```

```python
import functools

import jax
import jax.numpy as jnp
from jax import lax
from jax.experimental import pallas as pl
from jax.experimental.pallas import tpu as pltpu

F32 = jnp.float32
BF16 = jnp.bfloat16

D_MODEL = 2048
GRID_W = 64
EPS = 1e-6
CHUNK = 128
BRANCH_WIDTH = D_MODEL // 2
SSD_INNER = BRANCH_WIDTH
SSD_HEAD_DIM = 64
SSD_HEADS = SSD_INNER // SSD_HEAD_DIM
SSD_GROUPS = 4
SSD_STATE = 128
SSD_CONV_CH = SSD_INNER + 2 * SSD_GROUPS * SSD_STATE
POOL_WINDOWS = (2, 4, 8, 16)
POOL_GROUP = BRANCH_WIDTH // len(POOL_WINDOWS)
RET_HEADS = 8
RET_V_HEAD = BRANCH_WIDTH // RET_HEADS
RET_K_HEAD = RET_V_HEAD // 2
RET_QK = RET_HEADS * RET_K_HEAD
ROPE_BASE = 10000.0
D_FF = 256 * ((8 * D_MODEL // 3 + 255) // 256)

LANES = 128
HALO = 16
VMEM_LIMIT = 56 * 1024 * 1024

U_Z = 0
U_XBC = 1024
U_POOL = 3072
U_SCB = 4096
U_SCC = 5120
U_SCX = 6144
U_RQ = 7168
U_RK = 7680
U_RV = 8192
U_RG = 9216
U_COLS = 10240


def _params(sem):
    return pltpu.CompilerParams(dimension_semantics=sem, vmem_limit_bytes=VMEM_LIMIT)


def _silu(v):
    return v * jax.nn.sigmoid(v)


def _pick(cands, *ns):
    for c in cands:
        if all(n % c == 0 for n in ns):
            return c
    raise ValueError(f"no tile in {cands} divides {ns}")


def _mod_const(v, n):
    if n & (n - 1) == 0:
        return v & (n - 1)
    return lax.rem(v, n)


def _seq_pos(r0, rows, bc, ctx, seq):
    in_ctx = r0 < bc
    g = r0 + lax.broadcasted_iota(jnp.int32, (rows, 1), 0)
    pos = jnp.where(in_ctx, _mod_const(g, ctx), _mod_const(g - bc, seq))
    n = jnp.where(in_ctx, ctx, seq)
    return pos, n


def _mod_kernel(c_ref, w_ref, b_ref, o_ref):
    a = _silu(c_ref[...]).astype(BF16)
    o_ref[...] = jnp.dot(a, w_ref[...].astype(BF16), preferred_element_type=F32) + b_ref[...]


def _mod_call(cvec, w_mod, b_mod):
    nl, d, n6 = w_mod.shape
    r = cvec.shape[0]
    tn = 1024
    return pl.pallas_call(
        _mod_kernel,
        grid=(nl, n6 // tn),
        in_specs=[pl.BlockSpec((r, d), lambda l, j: (0, 0)),
                  pl.BlockSpec((None, d, tn), lambda l, j: (l, 0, j)),
                  pl.BlockSpec((None, 1, tn), lambda l, j: (l, 0, j))],
        out_specs=pl.BlockSpec((None, r, tn), lambda l, j: (l, 0, j)),
        out_shape=jax.ShapeDtypeStruct((nl, r, n6), F32),
        compiler_params=_params(("parallel", "parallel")),
        name="mod_vectors",
    )(cvec, w_mod, b_mod.reshape(nl, 1, n6))


def _normed(x, nw, sh, sc):
    y = x * lax.rsqrt(jnp.mean(jnp.square(x), axis=-1, keepdims=True) + EPS)
    return y * (nw * (1.0 + sc)) + sh


def _in_proj_kernel(x_ref, nw_ref, sh_ref, sc_ref, w_ref, wdt_ref, u_ref, h_ref, dt_ref):
    @pl.when(pl.program_id(1) == 0)
    def _():
        h = _normed(x_ref[...], nw_ref[...], sh_ref[...], sc_ref[...]).astype(BF16)
        h_ref[...] = h
        dt_ref[...] = jnp.dot(h, wdt_ref[...], preferred_element_type=F32)

    u_ref[...] = jnp.dot(h_ref[...], w_ref[...], preferred_element_type=F32).astype(BF16)


def _mod_spec(l, k, tn, row_fn, col_fn):
    return pl.BlockSpec((None, None, None, 1, tn),
                        lambda *g: (l, row_fn(*g), k, 0, col_fn(*g)))


def _mod_row(tile, bc, seq):
    def row(i):
        r = i * tile
        return jnp.where(r < bc, 0, 1 + (r - bc) // seq)
    return row


def _in_proj_call(x, norm_w, mod5, w, wdt, l, k_sh, k_sc, tm, row_of):
    t, d = x.shape
    n = w.shape[-1]
    tn = 1024
    zero = lambda i, j: 0
    row_of = (lambda f: lambda i, j: f(i))(row_of)
    return pl.pallas_call(
        _in_proj_kernel,
        grid=(t // tm, n // tn),
        in_specs=[pl.BlockSpec((tm, d), lambda i, j: (i, 0)),
                  pl.BlockSpec((None, 1, d), lambda i, j: (l, 0, 0)),
                  _mod_spec(l, k_sh, d, row_of, zero),
                  _mod_spec(l, k_sc, d, row_of, zero),
                  pl.BlockSpec((None, d, tn), lambda i, j: (l, 0, j)),
                  pl.BlockSpec((None, d, LANES), lambda i, j: (l, 0, 0))],
        out_specs=[pl.BlockSpec((tm, tn), lambda i, j: (i, j)),
                   pl.BlockSpec((tm, d), lambda i, j: (i, 0)),
                   pl.BlockSpec((tm, LANES), lambda i, j: (i, 0))],
        out_shape=[jax.ShapeDtypeStruct((t, n), BF16),
                   jax.ShapeDtypeStruct((t, d), BF16),
                   jax.ShapeDtypeStruct((t, LANES), F32)],
        compiler_params=_params(("parallel", "arbitrary")),
        name="in_proj",
    )(x, norm_w, mod5, mod5, w, wdt)


def _merge_kernel(h_ref, y0_ref, y1_ref, y2_ref, y3_ref, wg_ref, bg_ref, wb_ref, o_ref):
    h = h_ref[...]
    acc = None
    for i, y_ref in enumerate((y0_ref, y1_ref, y2_ref, y3_ref)):
        gate = jax.nn.sigmoid(jnp.dot(h, wg_ref[i], preferred_element_type=F32) + bg_ref[i])
        term = gate * jnp.dot(y_ref[...], wb_ref[i], preferred_element_type=F32)
        acc = term if acc is None else acc + term
    o_ref[...] = acc.astype(BF16)


def _merge_call(h, ys, wg, bg, wb, l, tm, row0=0):
    t, d = h.shape
    bw = ys[0].shape[-1]
    tn = 256
    nb = len(ys)
    off = row0 // tm
    t_out = t - row0
    return pl.pallas_call(
        _merge_kernel,
        grid=(t_out // tm, d // tn),
        in_specs=[pl.BlockSpec((tm, d), lambda i, j: (i + off, 0))]
                 + [pl.BlockSpec((tm, bw), lambda i, j: (i + off, 0)) for _ in ys]
                 + [pl.BlockSpec((None, nb, d, tn), lambda i, j: (l, 0, 0, j)),
                    pl.BlockSpec((None, nb, 1, tn), lambda i, j: (l, 0, 0, j)),
                    pl.BlockSpec((None, nb, bw, tn), lambda i, j: (l, 0, 0, j))],
        out_specs=pl.BlockSpec((tm, tn), lambda i, j: (i, j)),
        out_shape=jax.ShapeDtypeStruct((t_out, d), BF16),
        compiler_params=_params(("parallel", "arbitrary")),
        name="branch_merge",
    )(h, *ys, wg, bg, wb)


def _res_kernel(a_ref, w_ref, x_ref, g_ref, o_ref):
    o_ref[...] = x_ref[...] + g_ref[...] * jnp.dot(a_ref[...], w_ref[...],
                                                   preferred_element_type=F32)


def _res_call(a, w, x, mod5, l, k_gate, tm, row_of, name):
    t, kdim = a.shape
    d = x.shape[-1]
    tn = 512
    return pl.pallas_call(
        _res_kernel,
        grid=(t // tm, d // tn),
        in_specs=[pl.BlockSpec((tm, kdim), lambda i, j: (i, 0)),
                  pl.BlockSpec((None, kdim, tn), lambda i, j: (l, 0, j)),
                  pl.BlockSpec((tm, tn), lambda i, j: (i, j)),
                  _mod_spec(l, k_gate, tn, lambda i, j: row_of(i), lambda i, j: j)],
        out_specs=pl.BlockSpec((tm, tn), lambda i, j: (i, j)),
        out_shape=jax.ShapeDtypeStruct((t, d), F32),
        compiler_params=_params(("parallel", "arbitrary")),
        name=name,
    )(a, w, x, mod5)


EPI_ROWS = 128


def _outproj_norm_kernel(a_ref, w_ref, x_ref, g_ref, nw_ref, sh_ref, sc_ref, xo_ref, ho_ref, y0, y1):
    s = pl.program_id(0)
    tm = a_ref.shape[0]

    @pl.when(s == 0)
    def _():
        y1[...] = jnp.zeros_like(y1)

    def step(y_cur, y_prev):
        nslab = tm // EPI_ROWS
        cw = w_ref.shape[1] // nslab
        for k in range(nslab):
            rs = slice(k * EPI_ROWS, (k + 1) * EPI_ROWS)
            cs = slice(k * cw, (k + 1) * cw)
            y_cur[:, cs] = jnp.dot(a_ref[...], w_ref[:, cs], preferred_element_type=F32)
            xn = x_ref[rs, :] + g_ref[...] * y_prev[rs, :]
            xo_ref[rs, :] = xn
            ho_ref[rs, :] = _normed(xn, nw_ref[...], sh_ref[...], sc_ref[...]).astype(BF16)

    @pl.when(s % 2 == 0)
    def _():
        step(y0, y1)

    @pl.when(s % 2 == 1)
    def _():
        step(y1, y0)


def _outproj_norm_call(a, w, x, mod5, norm_w, l, k_gate, k_sh, k_sc, tm, bc, seq):
    t, kdim = a.shape
    d = x.shape[-1]
    ni = t // tm
    row0 = x.shape[0] - t
    assert row0 in (0, bc)
    off = row0 // tm
    row = _mod_row(tm, bc - row0, seq)
    cur = lambda s: jnp.minimum(s, ni - 1)
    prev = lambda s: jnp.maximum(s - 1, 0)
    zero = lambda s: 0
    return pl.pallas_call(
        _outproj_norm_kernel,
        grid=(ni + 1,),
        in_specs=[pl.BlockSpec((tm, kdim), lambda s: (cur(s), 0)),
                  pl.BlockSpec((None, kdim, d), lambda s: (l, 0, 0), pipeline_mode=pl.Buffered(1)),
                  pl.BlockSpec((tm, d), lambda s: (prev(s) + off, 0)),
                  _mod_spec(l, k_gate, d, lambda s: row(prev(s)), zero),
                  pl.BlockSpec((None, 1, d), lambda s: (l, 0, 0)),
                  _mod_spec(l, k_sh, d, lambda s: row(prev(s)), zero),
                  _mod_spec(l, k_sc, d, lambda s: row(prev(s)), zero)],
        out_specs=[pl.BlockSpec((tm, d), lambda s: (prev(s), 0)),
                   pl.BlockSpec((tm, d), lambda s: (prev(s), 0))],
        out_shape=[jax.ShapeDtypeStruct((t, d), F32),
                   jax.ShapeDtypeStruct((t, d), BF16)],
        scratch_shapes=[pltpu.VMEM((tm, d), F32), pltpu.VMEM((tm, d), F32)],
        compiler_params=_params(("arbitrary",)),
        name="out_proj_residual_norm",
    )(a, w, x, mod5, norm_w, mod5, mod5)


FFN_PIECE_ROWS = 64
FFN_SLABS = 4
SEQ_ALIGN = 256


def _ffn_gate_kernel(h_ref, hp_ref, hn_ref, wa_ref, wb_ref, cwa_ref, cwb_ref, ba_ref, bb_ref, o_ref,
                     h_s, a_s, b_s, *, nj, bc, ctx, seq):
    s = pl.program_id(0)
    tm = h_ref.shape[0]

    @pl.when(s == 0)
    def _():
        a_s[1] = jnp.zeros(a_s.shape[1:], F32)
        b_s[1] = jnp.zeros(b_s.shape[1:], F32)

    @pl.when(s % nj == 0)
    def _():
        h_s[0:HALO, :] = hp_ref[...]
        h_s[HALO:HALO + tm, :] = h_ref[...]
        h_s[HALO + tm:, :] = hn_ref[...]

    r0 = (jnp.maximum(s - 1, 0) // nj) * tm
    in_ctx = r0 < bc
    tn = o_ref.shape[1]
    rows_ext = tm + 2 * HALO
    pr = FFN_PIECE_ROWS
    row_first = lax.broadcasted_iota(jnp.int32, (pr, 1), 0) == 0
    row_last = lax.broadcasted_iota(jnp.int32, (pr, 1), 0) == pr - 1

    def seq_start(g):
        return jnp.where(in_ctx, _mod_const(g, ctx), _mod_const(g - bc, seq)) == 0

    def taps(w_ref, bias_ref, cs):
        return [jnp.broadcast_to(w_ref[j:j + 1, cs], (pr, LANES)) for j in range(3)] + \
               [jnp.broadcast_to(bias_ref[:, cs], (pr, LANES))]

    col_slabs = [slice(c, c + LANES) for c in range(0, tn, LANES)]
    taps_a = [taps(cwa_ref, ba_ref, cs) for cs in col_slabs]
    taps_b = [taps(cwb_ref, bb_ref, cs) for cs in col_slabs]

    def conv_piece(src, r, cs, tp, kill_prev, kill_next):
        base = HALO + r
        ext = src[base - 8:base + pr + 8, cs]
        xm1 = pltpu.roll(ext, 1, 0)[8:8 + pr]
        x00 = ext[8:8 + pr]
        xp1 = pltpu.roll(ext, pr + 15, 0)[8:8 + pr]
        if kill_prev is not None:
            xm1 = jnp.where(kill_prev, 0.0, xm1)
        if kill_next is not None:
            xp1 = jnp.where(kill_next, 0.0, xp1)
        return xm1 * tp[0] + x00 * tp[1] + xp1 * tp[2] + tp[3]

    def epilogue(a_prev, b_prev, lo, hi):
        for r in range(lo, hi, pr):
            kill_prev = kill_next = None
            if r % SEQ_ALIGN == 0:
                kill_prev = jnp.logical_and(row_first, seq_start(r0 + r))
            if (r + pr) % SEQ_ALIGN == 0:
                kill_next = jnp.logical_and(row_last, seq_start(r0 + r + pr))
            for ci, cs in enumerate(col_slabs):
                a = conv_piece(a_prev, r, cs, taps_a[ci], kill_prev, kill_next)
                b = conv_piece(b_prev, r, cs, taps_b[ci], kill_prev, kill_next)
                o_ref[r:r + pr, cs] = (_silu(a) * b).astype(BF16)

    def step(a_cur, b_cur, a_prev, b_prev):
        nslab = FFN_SLABS
        for k in range(nslab):
            m_lo = (k * rows_ext // nslab) // HALO * HALO
            m_hi = rows_ext if k == nslab - 1 else ((k + 1) * rows_ext // nslab) // HALO * HALO
            h = h_s[m_lo:m_hi, :]
            a_cur[m_lo:m_hi, :] = jnp.dot(h, wa_ref[...], preferred_element_type=F32)
            b_cur[m_lo:m_hi, :] = jnp.dot(h, wb_ref[...], preferred_element_type=F32)
            epilogue(a_prev, b_prev, k * tm // nslab, (k + 1) * tm // nslab)

    cur = s % 2
    step(a_s.at[cur], b_s.at[cur], a_s.at[1 - cur], b_s.at[1 - cur])


def _ffn_gate_call(h, w, cw, cb, l, tm, bc, ctx, seq):
    t, d = h.shape
    tn = 512
    nj = D_FF // tn
    ni = t // tm
    per = tm // HALO
    last = t // HALO - 1
    ci = lambda s: jnp.minimum(s // nj, ni - 1)
    cj = lambda s: s % nj
    pi = lambda s: jnp.maximum(s - 1, 0) // nj
    pj = lambda s: jnp.maximum(s - 1, 0) % nj
    return pl.pallas_call(
        functools.partial(_ffn_gate_kernel, nj=nj, bc=bc, ctx=ctx, seq=seq),
        grid=(ni * nj + 1,),
        in_specs=[pl.BlockSpec((tm, d), lambda s: (ci(s), 0)),
                  pl.BlockSpec((HALO, d), lambda s: (jnp.maximum(ci(s) * per - 1, 0), 0)),
                  pl.BlockSpec((HALO, d), lambda s: (jnp.minimum((ci(s) + 1) * per, last), 0)),
                  pl.BlockSpec((None, d, tn), lambda s: (l, 0, cj(s))),
                  pl.BlockSpec((None, d, tn), lambda s: (l, 0, nj + cj(s))),
                  pl.BlockSpec((None, 3, tn), lambda s: (l, 0, pj(s))),
                  pl.BlockSpec((None, 3, tn), lambda s: (l, 0, nj + pj(s))),
                  pl.BlockSpec((None, 1, tn), lambda s: (l, 0, pj(s))),
                  pl.BlockSpec((None, 1, tn), lambda s: (l, 0, nj + pj(s)))],
        out_specs=pl.BlockSpec((tm, tn), lambda s: (pi(s), pj(s))),
        out_shape=jax.ShapeDtypeStruct((t, D_FF), BF16),
        scratch_shapes=[pltpu.VMEM((tm + 2 * HALO, d), BF16)]
                       + [pltpu.VMEM((2, tm + 2 * HALO, tn), F32) for _ in range(2)],
        compiler_params=_params(("arbitrary",)),
        name="ffn_up_conv_gate",
    )(h, h, h, w, w, cw, cw, cb, cb)


def _final_norm_kernel(x_ref, w_ref, o_ref):
    x = x_ref[...]
    o_ref[...] = (x * lax.rsqrt(jnp.mean(jnp.square(x), axis=-1, keepdims=True) + EPS)) * w_ref[...]


def _final_norm_call(x, w, row0, rows, tm):
    d = x.shape[-1]
    off = row0 // tm
    return pl.pallas_call(
        _final_norm_kernel,
        grid=(rows // tm,),
        in_specs=[pl.BlockSpec((tm, d), lambda i: (i + off, 0)),
                  pl.BlockSpec((1, d), lambda i: (0, 0))],
        out_specs=pl.BlockSpec((tm, d), lambda i: (i, 0)),
        out_shape=jax.ShapeDtypeStruct((rows, d), F32),
        compiler_params=_params(("parallel",)),
        name="final_norm",
    )(x, w)


def _halo_specs(rt, cols, col_idx, t):
    per = rt // HALO
    last = t // HALO - 1
    return [pl.BlockSpec((rt, cols), lambda i, j: (i, col_idx(j))),
            pl.BlockSpec((HALO, cols), lambda i, j: (jnp.maximum(i * per - 1, 0), col_idx(j))),
            pl.BlockSpec((HALO, cols), lambda i, j: (jnp.minimum((i + 1) * per, last), col_idx(j)))]


def _shift_pm1(x, prev_row, next_row, pos, n):
    rt = x.shape[0]
    g = 8
    first = lax.broadcasted_iota(jnp.int32, (g, 1), 0) == 0
    last = lax.broadcasted_iota(jnp.int32, (g, 1), 0) == g - 1
    xm1 = pltpu.roll(x, 1, 0)
    xp1 = pltpu.roll(x, rt - 1, 0)
    m_parts, p_parts = [], []
    for s in range(0, rt, SEQ_ALIGN):
        e = s + SEQ_ALIGN
        head = xm1[s:s + g]
        row0 = prev_row if s == 0 else head[0:1]
        row0 = jnp.where(pos[s:s + 1] == 0, 0.0, row0)
        m_parts += [jnp.where(first, row0, head), xm1[s + g:e]]
        tail = xp1[e - g:e]
        row7 = next_row if e == rt else tail[g - 1:g]
        row7 = jnp.where(pos[e - 1:e] == n - 1, 0.0, row7)
        p_parts += [xp1[s:e - g], jnp.where(last, row7, tail)]
    return jnp.concatenate(m_parts, axis=0), jnp.concatenate(p_parts, axis=0)


def _conv3(x, prev_row, next_row, w, pos, n):
    xm1, xp1 = _shift_pm1(x, prev_row, next_row, pos, n)
    return xm1 * w[0:1] + x * w[1:2] + xp1 * w[2:3]


def _ssd_conv_kernel(x_ref, p_ref, n_ref, w_ref, b_ref, o_ref, *, bc, ctx, seq):
    rt = x_ref.shape[0]
    pos, n = _seq_pos(pl.program_id(0) * rt, rt, bc, ctx, seq)
    x = x_ref[...].astype(F32)
    prev_row = p_ref[...].astype(F32)[HALO - 1:HALO]
    next_row = n_ref[...].astype(F32)[0:1]
    y = _conv3(x, prev_row, next_row, w_ref[...], pos, n) + b_ref[...]
    o_ref[...] = _silu(y).astype(BF16)


def _ssd_conv_call(u, w, b, l, rt, bc, ctx, seq):
    t = u.shape[0]
    cols = 1024
    nj = SSD_CONV_CH // cols
    base = U_XBC // cols
    return pl.pallas_call(
        functools.partial(_ssd_conv_kernel, bc=bc, ctx=ctx, seq=seq),
        grid=(t // rt, nj),
        in_specs=_halo_specs(rt, cols, lambda j: base + j, t)
                 + [pl.BlockSpec((None, 3, cols), lambda i, j: (l, 0, j)),
                    pl.BlockSpec((None, 1, cols), lambda i, j: (l, 0, j))],
        out_specs=pl.BlockSpec((rt, cols), lambda i, j: (i, j)),
        out_shape=jax.ShapeDtypeStruct((t, SSD_CONV_CH), BF16),
        compiler_params=_params(("parallel", "parallel")),
        name="ssd_conv",
    )(u, u, u, w, b)


def _sconv_body(r0, b_ref, c_ref, cp_ref, cn_ref, x_ref, xp_ref, xn_ref, w_ref, o_ref, *, bc, ctx, seq):
    rt = b_ref.shape[0]
    pos, n = _seq_pos(r0, rt, bc, ctx, seq)
    cx = c_ref[...].astype(F32) * x_ref[...].astype(F32)
    prev_row = (cp_ref[...].astype(F32) * xp_ref[...].astype(F32))[HALO - 1:HALO]
    next_row = (cn_ref[...].astype(F32) * xn_ref[...].astype(F32))[0:1]
    y = b_ref[...].astype(F32) * _conv3(cx, prev_row, next_row, w_ref[...], pos, n)
    o_ref[...] = y.astype(BF16)


POOL_ROWS = 256
POOL_EXT = 384


def _pool_body(r0, x_ref, p_ref, n_ref, band_ref, cnt_ref, w_ref, s_ref, o_ref, *, bc, ctx, seq):
    rt = POOL_ROWS
    in_ctx = r0 < bc
    p0 = jnp.where(in_ctx, _mod_const(r0, ctx), _mod_const(r0 - bc, seq))
    n = jnp.where(in_ctx, ctx, seq)
    has_prev = p0 != 0
    has_next = p0 + rt != n
    zero = jnp.zeros((HALO, POOL_GROUP), BF16)
    pad = jnp.zeros((POOL_EXT - rt - 2 * HALO, POOL_GROUP), BF16)
    case = jnp.where(has_prev, 0, 2) + jnp.where(has_next, 0, 1)
    groups = [slice(gi * POOL_GROUP, (gi + 1) * POOL_GROUP) for gi in range(len(POOL_WINDOWS))]
    sums = []
    for gi, cs in enumerate(groups):
        ext = jnp.concatenate([jnp.where(has_prev, p_ref[:, cs], zero), x_ref[:, cs],
                               jnp.where(has_next, n_ref[:, cs], zero), pad], axis=0)
        sums.append(jnp.dot(band_ref[gi], ext, preferred_element_type=F32))
    for gi, cs in enumerate(groups):
        cnt = cnt_ref[gi, case]
        mean = sums[gi] / jnp.concatenate([cnt] * (POOL_GROUP // LANES), axis=1)
        pooled = (mean - x_ref[:, cs].astype(F32)).astype(BF16)
        y = jnp.dot(pooled, w_ref[gi], preferred_element_type=F32) * s_ref[:, cs]
        o_ref[:, cs] = y.astype(BF16)


def _pool_bands():
    t = jnp.arange(POOL_ROWS)[:, None]
    s = jnp.arange(POOL_EXT)[None, :] - HALO
    return jnp.stack([((s >= t - w // 2) & (s < t - w // 2 + w)) for w in POOL_WINDOWS]).astype(BF16)


def _pool_counts():
    t = jnp.arange(POOL_ROWS)
    tabs = []
    for w in POOL_WINDOWS:
        before = jnp.maximum(w // 2 - t, 0)
        after = jnp.maximum(t - w // 2 + w - POOL_ROWS, 0)
        cases = [w - (before if c >= 2 else 0) - (after if c % 2 else 0) for c in range(4)]
        tabs.append(jnp.stack([jnp.broadcast_to(c, (POOL_ROWS,)) for c in cases]))
    tab = jnp.stack(tabs).astype(F32)
    return jnp.broadcast_to(tab[..., None], tab.shape + (LANES,))


def _scan_positions(ncc, nk):
    def kpos(sw, s):
        back = jnp.where(s < ncc, ncc - 1 - s, nk - 1 - (s - ncc))
        return jnp.where(sw == 0, back, s)
    return kpos


def _chunk_row(b, k, ncc, ncl, nb):
    return jnp.where(k < ncc, b * ncc + k, nb * ncc + b * ncl + (k - ncc))


def _tri_masks():
    row = lax.broadcasted_iota(jnp.int32, (CHUNK, CHUNK), 0)
    col = lax.broadcasted_iota(jnp.int32, (CHUNK, CHUNK), 1)
    return row, col, col <= row, col >= row


MASKED = -1e30
LOG2E = 1.4426950408889634


SCAN_CHUNKS = 2


def _ssd_scan_parts(xbc_ref, dt_ref, z_ref, bias_ref, alog_ref, dexp_ref, nw_ref, sp_ref, sg_ref, o_ref,
                    hb_all, hf_s, hb_s, y_s, vw_s, pre_s, *, kpos):
    row, col, lower, upper = _tri_masks()
    lane_lo = col < SSD_HEAD_DIM
    row_lo1 = lane_lo[0:1]
    nh = SSD_HEADS
    n_pairs = nh // 2
    pairs_per_group = n_pairs // SSD_GROUPS
    gw = SSD_INNER // SSD_GROUPS

    def chunk_rows(c):
        return slice(c * CHUNK, (c + 1) * CHUNK)

    def split2(x):
        hi = x.astype(BF16).astype(F32)
        return hi, x - hi

    def split3(x):
        hi, r = split2(x)
        mid, lo = split2(r)
        return hi, mid, lo

    def to_tokens(parts):
        used = sum(p.shape[0] for p in parts)
        return jnp.concatenate(list(parts) + [jnp.zeros((CHUNK - used, CHUNK), F32)], axis=0).T.astype(BF16)

    def store_state_operand(wt, rc):
        lhs = to_tokens((jnp.zeros((4 * nh, CHUNK), F32),) + split2(wt))
        w_full = jnp.dot(lhs, sp_ref[:, 2 * SSD_INNER:3 * SSD_INNER], preferred_element_type=F32)
        vw_s[...] = (xbc_ref[rc, 0:SSD_INNER].astype(F32) * w_full).astype(BF16)

    def seg_all(cum):
        parts = split3(cum)
        ones = jnp.ones((nh, CHUNK), F32)
        a = to_tokens(parts + (ones, ones, ones))
        neg = jnp.concatenate([-p for p in parts], axis=0)
        tiled = jnp.concatenate([neg] * nh, axis=1)
        rows_b = jnp.where(sg_ref[3 * nh:6 * nh, :] != 0, tiled, 0.0).astype(BF16)
        b = jnp.concatenate([sg_ref[0:3 * nh, :], rows_b,
                             jnp.zeros((CHUNK - 6 * nh, nh * CHUNK), BF16)], axis=0)
        return jnp.dot(a, b, preferred_element_type=F32)

    def pair_scalar(x, p, j):
        return jnp.where(row_lo1, jnp.broadcast_to(x[2 * p:2 * p + 1, j:j + 1], (1, LANES)),
                         jnp.broadcast_to(x[2 * p + 1:2 * p + 2, j:j + 1], (1, LANES)))

    def state_update(h_s, g, dec_rows, rc):
        gs = slice(g * gw, (g + 1) * gw)
        bm = xbc_ref[rc, SSD_INNER + g * SSD_STATE:SSD_INNER + (g + 1) * SSD_STATE]
        st = lax.dot_general(bm, vw_s[:, gs], (((0,), (0,)), ((), ())), preferred_element_type=F32)
        h_s[:, gs] = h_s[:, gs] * jnp.concatenate(dec_rows, axis=1) + st

    def init_backward():
        hb_s[...] = jnp.zeros_like(hb_s)

    def init_forward():
        hf_s[...] = jnp.zeros_like(hf_s)

    def backward_step():
        for c in reversed(range(SCAN_CHUNKS)):
            rc = chunk_rows(c)
            ck = kpos * SCAN_CHUNKS + c
            dtt = dt_ref[rc, :].T[0:nh]
            dt2 = jax.nn.softplus(jnp.concatenate([dtt, dtt], axis=0) + bias_ref[...])
            la2 = dt2 * (-jnp.exp(alog_ref[...]))
            tri2 = jnp.concatenate([upper.astype(F32), lower.astype(F32)], axis=1)
            cum2 = jnp.dot(la2, tri2, precision=lax.Precision.HIGHEST, preferred_element_type=F32)
            cum_f = cum2[0:nh, 0:CHUNK]
            cum_b = cum2[nh:2 * nh, CHUNK:2 * CHUNK]
            pre_s[ck, 0:2 * nh] = dt2
            pre_s[ck, 2 * nh:3 * nh] = cum_f
            pre_s[ck, 3 * nh:4 * nh] = cum_b

            hb_all[ck] = hb_s[...].astype(BF16)
            store_state_operand(jnp.exp(cum_b[:, 0:1] - cum_b) * dt2[nh:2 * nh], rc)
            for g in range(SSD_GROUPS):
                dec_rows = [jnp.exp(pair_scalar(cum_b, g * pairs_per_group + pp, 0))
                            for pp in range(pairs_per_group)]
                state_update(hb_s, g, dec_rows, rc)

    def forward_step():
        for c in range(SCAN_CHUNKS):
            rc = chunk_rows(c)
            ck = kpos * SCAN_CHUNKS + c
            dt2 = pre_s[ck, 0:2 * nh]
            cum_f = pre_s[ck, 2 * nh:3 * nh]
            cum_b = pre_s[ck, 3 * nh:4 * nh]
            hb_in = hb_all[ck]
            wt = jnp.exp(cum_f[:, CHUNK - 1:CHUNK] - cum_f) * dt2[0:nh]
            ef, eb = split2(jnp.exp(cum_f)), split2(jnp.exp(cum_b))
            lhs = to_tokens((ef[0], eb[0], ef[1], eb[1]) + split2(wt))
            spread = jnp.dot(lhs, sp_ref[...], preferred_element_type=F32)
            vw_s[...] = (xbc_ref[rc, 0:SSD_INNER].astype(F32)
                         * spread[:, 2 * SSD_INNER:3 * SSD_INNER]).astype(BF16)
            seg_f = seg_all(cum_f * LOG2E)
            seg_b = seg_all(cum_b * LOG2E)
            for g in range(SSD_GROUPS):
                gs = slice(g * gw, (g + 1) * gw)
                bm = xbc_ref[rc, SSD_INNER + g * SSD_STATE:SSD_INNER + (g + 1) * SSD_STATE]
                cm = xbc_ref[rc, SSD_INNER + SSD_GROUPS * SSD_STATE + g * SSD_STATE:
                             SSD_INNER + SSD_GROUPS * SSD_STATE + (g + 1) * SSD_STATE]
                sc = lax.dot_general(cm, bm, (((1,), (1,)), ((), ())), preferred_element_type=F32)
                hcat = jnp.concatenate([hf_s[:, gs].astype(BF16), hb_in[:, gs]], axis=1)
                gfb = jnp.dot(cm, hcat, preferred_element_type=F32)
                dec_rows = []
                for pp in range(pairs_per_group):
                    p = g * pairs_per_group + pp
                    ps = slice(p * LANES, (p + 1) * LANES)
                    xs_pair = xbc_ref[rc, ps]
                    ms = []
                    for h in (2 * p, 2 * p + 1):
                        hs = slice(h * CHUNK, (h + 1) * CHUNK)
                        dec_f = jnp.exp2(jnp.where(lower, seg_f[:, hs], MASKED))
                        dec_b = jnp.exp2(jnp.where(upper, seg_b[:, hs], MASKED))
                        m = sc * (dec_f * dt2[h:h + 1] + dec_b * dt2[nh + h:nh + h + 1])
                        ms.append(m.astype(BF16))
                    zero = jnp.zeros_like(xs_pair)
                    x_split = jnp.concatenate([jnp.where(lane_lo, xs_pair, zero),
                                               jnp.where(lane_lo, zero, xs_pair)], axis=0)
                    y_intra = jnp.dot(jnp.concatenate(ms, axis=1), x_split, preferred_element_type=F32)
                    lo = pp * LANES
                    y_inter = (gfb[:, lo:lo + LANES] * spread[:, ps]
                               + gfb[:, gw + lo:gw + lo + LANES]
                               * spread[:, SSD_INNER + p * LANES:SSD_INNER + (p + 1) * LANES])
                    y_s[:, ps] = y_intra + y_inter
                    dec_rows.append(jnp.exp(pair_scalar(cum_f, p, CHUNK - 1)))
                state_update(hf_s, g, dec_rows, rc)

            xs32 = xbc_ref[rc, 0:SSD_INNER].astype(F32)
            y = (y_s[...] + xs32 * dexp_ref[...]) * _silu(z_ref[rc, :].astype(F32))
            y = y * lax.rsqrt(jnp.mean(jnp.square(y), axis=-1, keepdims=True) + EPS)
            o_ref[rc, :] = (y * nw_ref[...]).astype(BF16)

    return init_backward, backward_step, init_forward, forward_step


def _ssd_spread_constants():
    nh = SSD_HEADS
    r = jnp.arange(CHUNK)[:, None]
    c = jnp.arange(3 * SSD_INNER)[None, :]
    block_of_row = jnp.array([0, 1, 0, 1, 2, 2, -1, -1])[r // nh]
    sp = (block_of_row == c // SSD_INNER) & (r % nh == (c % SSD_INNER) // SSD_HEAD_DIM)
    c2 = jnp.arange(nh * CHUNK)[None, :]
    sg = (r < 6 * nh) & (r % nh == c2 // CHUNK)
    return sp.astype(BF16), sg.astype(BF16)


N_SSD_IN, N_SSD_SCRATCH = 9, 6
N_RET_IN, N_RET_SCRATCH = 7, 5


def _ssd_scan_specs(l, rows, in_row, nk):
    in_specs = [pl.BlockSpec((rows, SSD_CONV_CH), lambda b, sw, s: (in_row(b, sw, s), 0)),
                pl.BlockSpec((rows, LANES), lambda b, sw, s: (in_row(b, sw, s), 0)),
                pl.BlockSpec((rows, SSD_INNER), lambda b, sw, s: (in_row(b, sw, s), U_Z // SSD_INNER)),
                pl.BlockSpec((None, 2 * SSD_HEADS, CHUNK), lambda b, sw, s: (l, 0, 0)),
                pl.BlockSpec((None, 2 * SSD_HEADS, CHUNK), lambda b, sw, s: (l, 0, 0)),
                pl.BlockSpec((None, 1, SSD_INNER), lambda b, sw, s: (l, 0, 0)),
                pl.BlockSpec((None, 1, SSD_INNER), lambda b, sw, s: (l, 0, 0)),
                pl.BlockSpec((CHUNK, 3 * SSD_INNER), lambda b, sw, s: (0, 0)),
                pl.BlockSpec((CHUNK, SSD_HEADS * CHUNK), lambda b, sw, s: (0, 0))]
    scratch = [pltpu.VMEM((nk * SCAN_CHUNKS, SSD_STATE, SSD_INNER), BF16),
               pltpu.VMEM((SSD_STATE, SSD_INNER), F32),
               pltpu.VMEM((SSD_STATE, SSD_INNER), F32),
               pltpu.VMEM((CHUNK, SSD_INNER), F32),
               pltpu.VMEM((CHUNK, SSD_INNER), BF16),
               pltpu.VMEM((nk * SCAN_CHUNKS, 4 * SSD_HEADS, CHUNK), F32)]
    assert len(in_specs) == N_SSD_IN and len(scratch) == N_SSD_SCRATCH
    return in_specs, scratch


def _ret_scan_parts(q_ref, k_ref, v_ref, g_ref, cos_ref, sin_ref, dl_ref, o_ref,
                    hb_all, hf_s, hb_s, dec_s, tab_s, *, kpos):
    row, col, lower, upper = _tri_masks()
    lane_lo = col < RET_K_HEAD
    la = jax.nn.log_sigmoid(dl_ref[...])
    rowf = row.astype(F32)
    diff = (row - col).astype(F32)
    n_pairs = RET_HEADS // 2

    first16 = (col & 31) < 16

    def chunk_rows(c):
        return slice(c * CHUNK, (c + 1) * CHUNK)

    def rope(x, rc):
        swapped = jnp.where(first16, pltpu.roll(x, LANES - 16, 1), pltpu.roll(x, 16, 1))
        return x * cos_ref[rc, :] + swapped * sin_ref[rc, :]

    def head_scalar(d, h):
        return la[d:d + 1, h:h + 1]

    def pair_k(p, rc):
        return rope(k_ref[rc, p * LANES:(p + 1) * LANES].astype(F32), rc).astype(BF16)

    def state_update(h_s, d, p, kp, w_tab, rc):
        h0, h1 = 2 * p, 2 * p + 1
        v_of = lambda h: v_ref[rc, h * RET_V_HEAD:(h + 1) * RET_V_HEAD]
        vw = jnp.concatenate([(v_of(h0).astype(F32) * tab_s[w_tab, h0]).astype(BF16),
                              (v_of(h1).astype(F32) * tab_s[w_tab, h1]).astype(BF16)], axis=1)
        st = lax.dot_general(kp, vw, (((0,), (0,)), ((), ())), preferred_element_type=F32)
        top = row < RET_K_HEAD
        acc = jnp.where(top, st[:, 0:RET_V_HEAD], st[:, RET_V_HEAD:2 * RET_V_HEAD])
        dec = jnp.where(top, jnp.exp(float(CHUNK) * head_scalar(d, h0)),
                        jnp.exp(float(CHUNK) * head_scalar(d, h1)))
        rs = slice(p * LANES, (p + 1) * LANES)
        h_s[rs, :] = h_s[rs, :] * dec + acc

    def init_backward():
        hb_s[...] = jnp.zeros_like(hb_s)
        for h in range(RET_HEADS):
            laf = head_scalar(0, h)
            lab = head_scalar(1, h)
            dec_s[h] = (jnp.exp(jnp.where(lower, diff * laf, MASKED))
                        + jnp.exp(jnp.where(upper, -diff * lab, MASKED)))
            tab_s[0, h] = jnp.exp((rowf + 1.0) * laf)
            tab_s[1, h] = jnp.exp((float(CHUNK) - rowf) * lab)
            tab_s[2, h] = jnp.exp((float(CHUNK - 1) - rowf) * laf)
            tab_s[3, h] = jnp.exp(rowf * lab)

    def init_forward():
        hf_s[...] = jnp.zeros_like(hf_s)

    def backward_step():
        for c in reversed(range(SCAN_CHUNKS)):
            rc = chunk_rows(c)
            hb_all[kpos * SCAN_CHUNKS + c] = hb_s[...].astype(BF16)
            for p in range(n_pairs):
                state_update(hb_s, 1, p, pair_k(p, rc), 3, rc)

    def forward_step():
        for c in range(SCAN_CHUNKS):
            rc = chunk_rows(c)
            hb_in = hb_all[kpos * SCAN_CHUNKS + c]
            for p in range(n_pairs):
                h0, h1 = 2 * p, 2 * p + 1
                v_of = lambda h: v_ref[rc, h * RET_V_HEAD:(h + 1) * RET_V_HEAD]
                qp = rope(q_ref[rc, p * LANES:(p + 1) * LANES].astype(F32) * (RET_K_HEAD ** -0.5), rc)
                kp = pair_k(p, rc)
                rs = slice(p * LANES, (p + 1) * LANES)
                qcat = jnp.concatenate([jnp.where(lane_lo, qp, 0.0), jnp.where(lane_lo, 0.0, qp)],
                                       axis=0).astype(BF16)
                sc = lax.dot_general(qcat, kp, (((1,), (1,)), ((), ())), preferred_element_type=F32)
                hcat = jnp.concatenate([hf_s[rs, :].astype(BF16), hb_in[rs, :]], axis=1)
                gfb = jnp.dot(qcat, hcat, preferred_element_type=F32)
                m = jnp.concatenate([(sc[0:CHUNK] * dec_s[h0]).astype(BF16),
                                     (sc[CHUNK:2 * CHUNK] * dec_s[h1]).astype(BF16)], axis=1)
                zero = jnp.zeros((CHUNK, RET_V_HEAD), BF16)
                v_blk = jnp.concatenate([jnp.concatenate([v_of(h0), zero], axis=1),
                                         jnp.concatenate([zero, v_of(h1)], axis=1)], axis=0)
                y_intra = jnp.dot(m, v_blk, preferred_element_type=F32)
                for i, h in enumerate((h0, h1)):
                    vs = slice(h * RET_V_HEAD, (h + 1) * RET_V_HEAD)
                    g_h = gfb[i * CHUNK:(i + 1) * CHUNK]
                    y = (y_intra[:, i * RET_V_HEAD:(i + 1) * RET_V_HEAD]
                         + g_h[:, 0:RET_V_HEAD] * tab_s[0, h]
                         + g_h[:, RET_V_HEAD:2 * RET_V_HEAD] * tab_s[1, h])
                    mu = jnp.mean(y, axis=-1, keepdims=True)
                    yc = y - mu
                    var = jnp.mean(jnp.square(yc), axis=-1, keepdims=True)
                    yn = yc * lax.rsqrt(var + EPS)
                    o_ref[rc, vs] = (_silu(g_ref[rc, vs].astype(F32)) * yn).astype(BF16)
                state_update(hf_s, 0, p, kp, 2, rc)

    return init_backward, backward_step, init_forward, forward_step


def _ret_scan_specs(l, rows, in_row, kpos, nk):
    bw = BRANCH_WIDTH
    in_specs = [pl.BlockSpec((rows, RET_QK), lambda b, sw, s: (in_row(b, sw, s), U_RQ // RET_QK)),
                pl.BlockSpec((rows, RET_QK), lambda b, sw, s: (in_row(b, sw, s), U_RK // RET_QK)),
                pl.BlockSpec((rows, bw), lambda b, sw, s: (in_row(b, sw, s), U_RV // bw)),
                pl.BlockSpec((rows, bw), lambda b, sw, s: (in_row(b, sw, s), U_RG // bw)),
                pl.BlockSpec((rows, LANES), lambda b, sw, s: (kpos(sw, s), 0)),
                pl.BlockSpec((rows, LANES), lambda b, sw, s: (kpos(sw, s), 0)),
                pl.BlockSpec((None, 2, LANES), lambda b, sw, s: (l, 0, 0))]
    scratch = [pltpu.VMEM((nk * SCAN_CHUNKS, RET_QK, RET_V_HEAD), BF16),
               pltpu.VMEM((RET_QK, RET_V_HEAD), F32),
               pltpu.VMEM((RET_QK, RET_V_HEAD), F32),
               pltpu.VMEM((RET_HEADS, CHUNK, CHUNK), F32),
               pltpu.VMEM((4, RET_HEADS, CHUNK, LANES), F32)]
    assert len(in_specs) == N_RET_IN and len(scratch) == N_RET_SCRATCH
    return in_specs, scratch


N_POOL_IN, N_SCONV_IN = 7, 8


def _scans_kernel(*refs, ncc, ncl, nk, nb, bc, ctx, seq):
    bounds = [0]
    for n in (N_SSD_IN, N_RET_IN, N_POOL_IN, N_SCONV_IN):
        bounds.append(bounds[-1] + n)
    ssd_in, ret_in, pool_in, sconv_in = (refs[bounds[i]:bounds[i + 1]] for i in range(4))
    n_in = bounds[-1]
    ssd_out, ret_out, pool_out, sconv_out = refs[n_in:n_in + 4]
    scratch = refs[n_in + 4:]
    ssd_scr, ret_scr = scratch[:N_SSD_SCRATCH], scratch[N_SSD_SCRATCH:]
    sw = pl.program_id(1)
    s = pl.program_id(2)
    kpos = _scan_positions(ncc, nk)(sw, s)
    ssd = _ssd_scan_parts(*ssd_in, ssd_out, *ssd_scr, kpos=kpos)
    ret = _ret_scan_parts(*ret_in, ret_out, *ret_scr, kpos=kpos)
    r0 = _chunk_row(pl.program_id(0), kpos, ncc, ncl, nb) * (SCAN_CHUNKS * CHUNK)

    @pl.when(jnp.logical_and(sw == 0, s == 0))
    def _():
        ssd[0]()
        ret[0]()

    @pl.when(sw == 0)
    def _():
        _sconv_body(r0, *sconv_in, sconv_out, bc=bc, ctx=ctx, seq=seq)
        ssd[1]()
        _pool_body(r0, *pool_in, pool_out, bc=bc, ctx=ctx, seq=seq)
        ret[1]()

    @pl.when(jnp.logical_and(sw == 1, s == 0))
    def _():
        ssd[2]()
        ret[2]()

    @pl.when(sw == 1)
    def _():
        ssd[3]()
        ret[3]()


def _mixers_call(xbc, dt, u, dt_bias, a_log, d_exp, norm_w, cos_t, sin_t, decay_logit,
                 pool_w, pool_scale, sconv_w, l, nb, bc, ctx, seq):
    t = u.shape[0]
    rows = SCAN_CHUNKS * CHUNK
    assert rows == POOL_ROWS and SSD_INNER == BRANCH_WIDTH
    ncc, ncl = ctx // rows, seq // rows
    nk = ncc + ncl
    kpos = _scan_positions(ncc, nk)
    bw = BRANCH_WIDTH
    per = rows // HALO
    last = t // HALO - 1

    def in_row(b, sw, s):
        return _chunk_row(b, kpos(sw, s), ncc, ncl, nb)

    def out_row(b, sw, s):
        return _chunk_row(b, jnp.where(sw == 0, 0, s), ncc, ncl, nb)

    def once_row(b, sw, s):
        return in_row(b, 0, jnp.where(sw == 0, s, nk - 1))

    def halo(col):
        c = col // bw
        return [pl.BlockSpec((rows, bw), lambda b, sw, s: (once_row(b, sw, s), c)),
                pl.BlockSpec((HALO, bw), lambda b, sw, s: (jnp.maximum(once_row(b, sw, s) * per - 1, 0), c)),
                pl.BlockSpec((HALO, bw), lambda b, sw, s: (jnp.minimum((once_row(b, sw, s) + 1) * per, last), c))]

    ng = len(POOL_WINDOWS)
    ssd_in, ssd_scr = _ssd_scan_specs(l, rows, in_row, nk)
    ret_in, ret_scr = _ret_scan_specs(l, rows, in_row, kpos, nk)
    pool_in = halo(U_POOL) + [pl.BlockSpec((ng, rows, POOL_EXT), lambda b, sw, s: (0, 0, 0)),
                              pl.BlockSpec((ng, 4, rows, LANES), lambda b, sw, s: (0, 0, 0, 0)),
                              pl.BlockSpec((None, ng, POOL_GROUP, POOL_GROUP), lambda b, sw, s: (l, 0, 0, 0)),
                              pl.BlockSpec((None, 1, bw), lambda b, sw, s: (l, 0, 0))]
    sconv_in = ([pl.BlockSpec((rows, bw), lambda b, sw, s: (once_row(b, sw, s), U_SCB // bw))]
                + halo(U_SCC) + halo(U_SCX) + [pl.BlockSpec((None, 3, bw), lambda b, sw, s: (l, 0, 0))])
    assert len(pool_in) == N_POOL_IN and len(sconv_in) == N_SCONV_IN
    scan_out = pl.BlockSpec((rows, bw), lambda b, sw, s: (out_row(b, sw, s), 0))
    once_out = pl.BlockSpec((rows, bw), lambda b, sw, s: (once_row(b, sw, s), 0))
    return pl.pallas_call(
        functools.partial(_scans_kernel, ncc=ncc, ncl=ncl, nk=nk, nb=nb, bc=bc, ctx=ctx, seq=seq),
        grid=(nb, 2, nk),
        in_specs=ssd_in + ret_in + pool_in + sconv_in,
        out_specs=[scan_out, scan_out, once_out, once_out],
        out_shape=[jax.ShapeDtypeStruct((t, bw), BF16)] * 4,
        scratch_shapes=ssd_scr + ret_scr,
        compiler_params=_params(("parallel", "arbitrary", "arbitrary")),
        name="mixer_branches",
    )(xbc, dt, u, dt_bias, a_log, d_exp, norm_w, *_ssd_spread_constants(),
      u, u, u, u, cos_t, sin_t, decay_logit,
      u, u, u, _pool_bands(), _pool_counts(), pool_w, pool_scale,
      u, u, u, u, u, u, u, sconv_w)


def _rope_tables(ctx, seq):
    quarter = RET_K_HEAD // 4
    t = jnp.arange(seq)
    rowp = (t // GRID_W).astype(F32)
    colp = (t % GRID_W).astype(F32)
    inv = ROPE_BASE ** (-jnp.arange(quarter, dtype=F32) / quarter)
    ang_r = rowp[:, None] * inv[None, :]
    ang_c = colp[:, None] * inv[None, :]

    def blocks(ang):
        c, s = jnp.cos(ang), jnp.sin(ang)
        return jnp.concatenate([c, c], axis=-1), jnp.concatenate([-s, s], axis=-1)

    cr, sr = blocks(ang_r)
    cc, sc = blocks(ang_c)
    cos_h = jnp.concatenate([cr, cc], axis=-1)
    sin_h = jnp.concatenate([sr, sc], axis=-1)
    cos_l = jnp.concatenate([cos_h, cos_h], axis=-1)
    sin_l = jnp.concatenate([sin_h, sin_h], axis=-1)
    cos_t = jnp.concatenate([jnp.ones((ctx, LANES), F32), cos_l], axis=0)
    sin_t = jnp.concatenate([jnp.zeros((ctx, LANES), F32), sin_l], axis=0)
    return cos_t, sin_t


def _pad_lanes(a):
    return jnp.pad(a, [(0, 0)] * (a.ndim - 1) + [(0, LANES - a.shape[-1])])


def kernel(x, c, ctx, c_ctx, w_mod, b_mod, norm1_w, w_in, ssd_conv_w, ssd_conv_b, ssd_a_log,
           ssd_dt_bias, ssd_d, ssd_norm_w, pool_w, pool_scale, sconv_w, ret_decay_logit,
           w_branch, w_gate, b_gate, w_o, norm2_w, ffn_up, ffn_conv_w, ffn_conv_b, ffn_down,
           final_norm_w):
    nb, seq, d = x.shape
    nctx = ctx.shape[1]
    depth = w_mod.shape[0]
    assert d == D_MODEL and seq % GRID_W == 0
    assert nctx % POOL_ROWS == 0 and seq % POOL_ROWS == 0
    bc = nb * nctx
    t = bc + nb * seq
    tm = _pick((1024, 512, 256), bc, seq)
    rt = _pick((512, 256), bc, seq)

    def row_of(i):
        r = i * tm
        return jnp.where(r < bc, 0, 1 + (r - bc) // seq)

    sizes = (SSD_INNER, SSD_CONV_CH, SSD_HEADS, BRANCH_WIDTH, BRANCH_WIDTH, BRANCH_WIDTH,
             BRANCH_WIDTH, RET_QK, RET_QK, BRANCH_WIDTH, BRANCH_WIDTH)
    offs = [0]
    for sz in sizes:
        offs.append(offs[-1] + sz)
    parts = [w_in[:, :, offs[i]:offs[i + 1]] for i in range(len(sizes))]
    w_main = jnp.concatenate(parts[:2] + parts[3:], axis=-1).astype(BF16)
    w_dt = _pad_lanes(parts[2]).astype(BF16)
    assert w_main.shape[-1] == U_COLS
    wg_b = w_gate.astype(BF16)
    wb_b = w_branch.astype(BF16)
    wo_b = w_o.astype(BF16)
    up_b = ffn_up.astype(BF16)
    down_b = ffn_down.astype(BF16)
    poolw_b = pool_w.astype(BF16)

    nrows = 8 * ((1 + nb + 7) // 8)
    cvec = jnp.zeros((nrows, d), F32).at[0].set(c_ctx).at[1:1 + nb].set(c)
    mod = _mod_call(cvec, w_mod, b_mod)
    mod5 = mod.reshape(depth, nrows, 6, 1, d)

    cos_t, sin_t = _rope_tables(nctx, seq)
    rep = lambda a: jnp.broadcast_to(a.reshape(depth, 2 * SSD_HEADS, 1), (depth, 2 * SSD_HEADS, CHUNK))
    dt_bias_p = rep(ssd_dt_bias)
    a_log_p = rep(ssd_a_log)
    decay_p = _pad_lanes(ret_decay_logit)
    d_exp = jnp.repeat(ssd_d, SSD_HEAD_DIM, axis=-1)[:, None, :]

    r3 = lambda a: a[:, None, :]
    xs = jnp.concatenate([ctx.reshape(bc, d), x.reshape(nb * seq, d)], axis=0)
    for l in range(depth):
        u, h, dt = _in_proj_call(xs, r3(norm1_w), mod5, w_main, w_dt, l, 0, 1, tm, row_of)
        xbc = _ssd_conv_call(u, ssd_conv_w, r3(ssd_conv_b), l, rt, bc, nctx, seq)
        y_ssd, y_ret, y_pool, y_sc = _mixers_call(
            xbc, dt, u, dt_bias_p, a_log_p, d_exp, r3(ssd_norm_w), cos_t, sin_t, decay_p,
            poolw_b, r3(pool_scale), sconv_w, l, nb, bc, nctx, seq)
        lat = l == depth - 1
        bc_l = 0 if lat else bc
        row_l = _mod_row(tm, bc_l, seq)
        merged = _merge_call(h, (y_ssd, y_pool, y_sc, y_ret), wg_b, b_gate[:, :, None, :], wb_b, l, tm,
                             row0=bc if lat else 0)
        xs, h2 = _outproj_norm_call(merged, wo_b, xs, mod5, r3(norm2_w), l, 2, 3, 4, rt, bc, seq)
        gact = _ffn_gate_call(h2, up_b, ffn_conv_w, r3(ffn_conv_b), l, tm, bc_l, nctx, seq)
        xs = _res_call(gact, down_b, xs, mod5, l, 5, tm, row_l, "ffn_down_residual")
    out = _final_norm_call(xs, final_norm_w[None, :], 0, nb * seq, tm)
    return out.reshape(nb, seq, d)
```
